```python
import math
import jax, jax.numpy as jnp
from jax import lax
import numpy as np


D_MODEL = 1024
BATCH = 2
SEQ = 8192
DEPTH = 2

PLE_DIM = 256
N_EVEN = (DEPTH + 1) // 2
N_ODD = DEPTH // 2

H_A = 8
D_NOPE = 64
D_VA = 64
R_Q = 256
R_KV = 128
H_IDX = 8
D_IDX = 64
TOPK_MAX = 256
Q_BLOCK = 128

H_B = 4
D_B = 128
CONV_K = 4
DN_CHUNK = 64

H_C = 12
DH_C = 64
DILATED_PATTERNS = ((128, 1), (512, 4), (2048, 16))

POOL_WINDOWS = (2, 4, 8, 16)
POOL_GROUP = 64

N_EXPERTS = 32
TOP_K = 4
D_EXPERT = 1024
SWIGLU_LIMIT = 7.0
SWIGLU_ALPHA = 1.702
EXPERT_BLOCK = 128

ALPHA = (2 * DEPTH) ** 0.25
BETA = (8 * DEPTH) ** -0.25

AB_SPLITS = (R_Q, R_KV, D_IDX, H_IDX, H_B * D_B, H_B * D_B, H_B * D_B, H_B * D_B, H_B, H_B)
CD_SPLITS = (H_C * DH_C, H_C * DH_C, H_C * DH_C, len(POOL_WINDOWS) * POOL_GROUP)
W_AB = H_A * D_VA + H_B * D_B
W_CD = H_C * DH_C + len(POOL_WINDOWS) * POOL_GROUP

kernel_name = "hybrid_dsa_gdn_dilated_pool_moe"

F32 = jnp.float32


def _split(t, sizes):
    outs, start = [], 0
    for n in sizes:
        outs.append(t[..., start:start + n])
        start += n
    return outs


def _layernorm(x, g, b, eps=1e-5):
    xf = x.astype(F32)
    mu = xf.mean(-1, keepdims=True)
    var = jnp.square(xf - mu).mean(-1, keepdims=True)
    return ((xf - mu) * lax.rsqrt(var + eps) * g + b).astype(x.dtype)


def _rmsnorm(x, g, eps=1e-6):
    xf = x.astype(F32)
    return (xf * lax.rsqrt(jnp.mean(xf * xf, -1, keepdims=True) + eps) * g).astype(x.dtype)


def _l2norm(x, eps=1e-6):
    return x * lax.rsqrt(jnp.sum(x * x, -1, keepdims=True) + eps)


def _dsa_attention(c_q, c_kv, k_idx, w_idx, w_uq, w_uk, w_uv, w_qidx):
    B, S, _ = c_q.shape
    topk = min(TOPK_MAX, S // 4)
    nb = S // Q_BLOCK
    q = jnp.einsum('bsr,rhd->bshd', c_q, w_uq)
    q_lat = jnp.einsum('bshd,rhd->bshr', q, w_uk) * D_NOPE ** -0.5
    q_idx = jnp.einsum('bsr,rhd->bshd', c_q, w_qidx)
    w_idx = w_idx * (H_IDX ** -0.5 * D_IDX ** -0.5)
    key_pos = jnp.arange(S)
    q_pos = key_pos.reshape(nb, Q_BLOCK)

    def blocks(t):
        return jnp.moveaxis(t.reshape(B, nb, Q_BLOCK, *t.shape[2:]), 1, 0)

    def attend(args):
        qi_idx, wi, qi_lat, pos = args
        sc = jnp.einsum('bqhd,bkd->bqhk', qi_idx, k_idx)
        isc = jnp.einsum('bqhk,bqh->bqk', jax.nn.relu(sc), wi).astype(F32)
        causal = key_pos[None, :] <= pos[:, None]
        isc = jnp.where(causal[None], isc, -jnp.inf)
        _, sel = lax.top_k(isc, topk)
        kv_sel = jax.vmap(lambda c, i: c[i])(c_kv, sel)
        s = jnp.einsum('bqhr,bqkr->bqhk', qi_lat, kv_sel).astype(F32)
        valid = sel <= pos[None, :, None]
        s = jnp.where(valid[:, :, None, :], s, -jnp.inf)
        pr = jax.nn.softmax(s, axis=-1).astype(kv_sel.dtype)
        return jnp.einsum('bqhk,bqkr->bqhr', pr, kv_sel)

    o_lat = lax.map(attend, (blocks(q_idx), blocks(w_idx), blocks(q_lat), q_pos))
    o_lat = jnp.moveaxis(o_lat, 0, 1).reshape(B, S, H_A, R_KV)
    o = jnp.einsum('bshr,rhv->bshv', o_lat, w_uv)
    return o.reshape(B, S, H_A * D_VA)


def _causal_conv(x, w):
    C = x.shape[-1]
    K = w.shape[0]
    return lax.conv_general_dilated(x, w[:, None, :], window_strides=(1,), padding=[(K - 1, 0)],
                                    dimension_numbers=('NWC', 'WIO', 'NWC'), feature_group_count=C)


def _chunk_gated_delta(q, k, v, g, beta):
    B, S, H, Dk = q.shape
    Dv = v.shape[-1]
    C = DN_CHUNK
    n = S // C

    def chunks(t):
        return t.reshape(B, n, C, H, -1).transpose(1, 0, 3, 2, 4)

    q, k, v = chunks(q), chunks(k), chunks(v)
    g = g.reshape(B, n, C, H).transpose(1, 0, 3, 2)
    beta = beta.reshape(B, n, C, H).transpose(1, 0, 3, 2)
    gc = jnp.cumsum(g, axis=-1)
    lower = jnp.tril(jnp.ones((C, C), bool))
    strict = jnp.tril(jnp.ones((C, C), bool), -1)
    decay = jnp.exp(jnp.where(lower, gc[..., :, None] - gc[..., None, :], -jnp.inf))
    kb = k * beta[..., None]
    vb = v * beta[..., None]
    a_mat = jnp.where(strict, jnp.einsum('...id,...jd->...ij', kb, k) * decay, 0.0)
    eye = jnp.eye(C, dtype=F32)
    t_mat = lax.linalg.triangular_solve(a_mat + eye, jnp.broadcast_to(eye, a_mat.shape),
                                        left_side=True, lower=True, unit_diagonal=True)
    u = t_mat @ vb
    w = t_mat @ (kb * jnp.exp(gc)[..., None])
    qk = jnp.where(lower, jnp.einsum('...id,...jd->...ij', q, k) * decay, 0.0)
    q_dec = q * jnp.exp(gc)[..., None]
    k_dec = k * jnp.exp(gc[..., -1:] - gc)[..., None]
    g_last = jnp.exp(gc[..., -1])

    def step(state, xs):
        q_i, k_i, u_i, w_i, qk_i, gl_i = xs
        v_new = u_i - w_i @ state
        o_i = q_i @ state + qk_i @ v_new
        state = state * gl_i[..., None, None] + jnp.einsum('bhck,bhcv->bhkv', k_i, v_new)
        return state, o_i

    state0 = jnp.zeros((B, H, Dk, Dv), F32)
    _, o = lax.scan(step, state0, (q_dec, k_dec, u, w, qk, g_last))
    return o.transpose(1, 0, 3, 2, 4).reshape(B, S, H, Dv)


def _gated_deltanet(q, k, v, z, b, a, conv_w, a_log, dt_bias, norm_g):
    B, S, _ = q.shape
    W = H_B * D_B
    qkv = jax.nn.silu(_causal_conv(jnp.concatenate([q, k, v], -1), conv_w))
    q, k, v = [t.reshape(B, S, H_B, D_B).astype(F32) for t in _split(qkv, (W, W, W))]
    q = _l2norm(q) * D_B ** -0.5
    k = _l2norm(k)
    beta = jax.nn.sigmoid(b.astype(F32))
    g = -jnp.exp(a_log.astype(F32)) * jax.nn.softplus(a.astype(F32) + dt_bias.astype(F32))
    o = _chunk_gated_delta(q, k, v, g, beta)
    o = _rmsnorm(o, norm_g) * jax.nn.silu(z.reshape(B, S, H_B, D_B).astype(F32))
    return o.reshape(B, S, W).astype(z.dtype)


def _mixer_ab(x, w_in, q_norm_g, kv_norm_g, w_uq, w_uk, w_uv, w_qidx, kidx_norm_g, kidx_norm_b,
              conv_w, a_log, dt_bias, out_norm_g, w_out):
    c_q, c_kv, k_idx, w_idx, qb, kb, vb, zb, bb, ab = _split(x @ w_in, AB_SPLITS)
    c_q = _rmsnorm(c_q, q_norm_g)
    c_kv = _rmsnorm(c_kv, kv_norm_g)
    k_idx = _layernorm(k_idx, kidx_norm_g, kidx_norm_b)
    o_a = _dsa_attention(c_q, c_kv, k_idx, w_idx, w_uq, w_uk, w_uv, w_qidx)
    o_b = _gated_deltanet(qb, kb, vb, zb, bb, ab, conv_w, a_log, dt_bias, out_norm_g)
    return jnp.concatenate([o_a.astype(x.dtype), o_b.astype(x.dtype)], -1) @ w_out


def _dilated_branch(q, k, v, window, dilation):
    B, S, H, Dh = q.shape
    R = window // dilation
    seg = dilation * R
    S_pad = -(-S // seg) * seg
    N = S_pad // dilation
    nb = N // R

    def to_blocks(t):
        t = jnp.pad(t, ((0, 0), (0, S_pad - S), (0, 0), (0, 0)))
        t = t.reshape(B, N, dilation, H, Dh).transpose(0, 2, 1, 3, 4)
        return t.reshape(B, dilation, nb, R, H, Dh)

    def with_prev(t):
        prev = jnp.pad(t, ((0, 0), (0, 0), (1, 0), (0, 0), (0, 0), (0, 0)))[:, :, :-1]
        return jnp.concatenate([prev, t], axis=3)

    qb = to_blocks(q)
    k2 = with_prev(to_blocks(k))
    v2 = with_prev(to_blocks(v))
    s = jnp.einsum('bcnqhd,bcnkhd->bcnqhk', qb, k2).astype(F32) * Dh ** -0.5
    r = jnp.arange(R)[:, None]
    j = jnp.arange(2 * R)[None, :]
    dist = R + r - j
    band = (dist >= 0) & (dist <= R)
    has_prev = (jnp.arange(nb) > 0)[:, None, None] | (j >= R)[None]
    mask = band[None] & has_prev
    s = jnp.where(mask[:, :, None, :], s, -jnp.inf)
    m = s.max(-1)
    e = jnp.exp(s - m[..., None])
    den = e.sum(-1)
    o = jnp.einsum('bcnqhk,bcnkhd->bcnqhd', e, v2.astype(F32)) / den[..., None]

    def from_blocks(t):
        t = t.reshape(B, dilation, N, *t.shape[4:])
        t = jnp.moveaxis(t, 1, 2)
        return t.reshape(B, S_pad, *t.shape[3:])[:, :S]

    return from_blocks(o), from_blocks(m), from_blocks(den)


def _dilated_attention(q, k, v):
    res = [_dilated_branch(q, k, v, w, d) for (w, d) in DILATED_PATTERNS]
    o = jnp.stack([t[0] for t in res])
    m = jnp.stack([t[1] for t in res])
    den = jnp.stack([t[2] for t in res])
    wgt = den * jnp.exp(m - m.max(0))
    return (wgt[..., None] * o).sum(0) / wgt.sum(0)[..., None]


def _multiscale_pool(u, pool_w, pool_scale):
    B, S, _ = u.shape
    count = jnp.arange(1, S + 1).astype(F32)
    outs = []
    for gi, w in enumerate(POOL_WINDOWS):
        xg = u[..., gi * POOL_GROUP:(gi + 1) * POOL_GROUP].astype(F32)
        c = jnp.cumsum(xg, axis=1)
        c_prev = jnp.pad(c, ((0, 0), (w, 0), (0, 0)))[:, :S]
        mean = (c - c_prev) / jnp.minimum(count, w)[None, :, None]
        outs.append((mean - xg) @ pool_w[gi].astype(F32))
    return (jnp.concatenate(outs, -1) * pool_scale).astype(u.dtype)


def _mixer_cd(x, w_in, pool_w, pool_scale, w_out):
    B, S, _ = x.shape
    qc, kc, vc, ud = _split(x @ w_in, CD_SPLITS)
    shp = (B, S, H_C, DH_C)
    o_c = _dilated_attention(qc.reshape(shp), kc.reshape(shp), vc.reshape(shp))
    o_d = _multiscale_pool(ud, pool_w, pool_scale)
    return jnp.concatenate([o_c.reshape(B, S, H_C * DH_C).astype(x.dtype), o_d], -1) @ w_out


def _moe(x, router_w, router_b, w_gu, b_gu, w_down, b_down):
    B, S, D = x.shape
    xt = x.reshape(-1, D)
    T = xt.shape[0]
    logits = (xt @ router_w + router_b).astype(F32)
    top_logit, top_e = lax.top_k(logits, TOP_K)
    gate = jax.nn.softmax(top_logit, axis=-1)
    TK = T * TOP_K
    flat_e = top_e.reshape(-1).astype(jnp.int32)
    flat_tok = jnp.arange(TK, dtype=jnp.int32) // TOP_K
    flat_gate = gate.reshape(-1)
    order = jnp.argsort(flat_e)
    se = flat_e[order]
    counts = jnp.zeros((N_EXPERTS,), jnp.int32).at[flat_e].add(1)
    padded = (counts + EXPERT_BLOCK - 1) // EXPERT_BLOCK * EXPERT_BLOCK
    pad_end = jnp.cumsum(padded)
    pad_start = pad_end - padded
    start = jnp.cumsum(counts) - counts
    dest = pad_start[se] + jnp.arange(TK, dtype=jnp.int32) - start[se]
    nblk = -(-TK // EXPERT_BLOCK) + N_EXPERTS
    P = nblk * EXPERT_BLOCK
    slot_tok = jnp.full((P,), T, jnp.int32).at[dest].set(flat_tok[order])
    slot_gate = jnp.zeros((P,), F32).at[dest].set(flat_gate[order])
    blk_e = jnp.minimum(jnp.searchsorted(pad_end, jnp.arange(nblk) * EXPERT_BLOCK, side='right'),
                        N_EXPERTS - 1)
    x_pad = jnp.concatenate([xt, jnp.zeros((1, D), xt.dtype)], 0)
    xs = x_pad[slot_tok].reshape(nblk, EXPERT_BLOCK, D)

    def expert_block(args):
        xb, e = args
        gu = xb @ w_gu[e] + b_gu[e]
        gt = jnp.minimum(gu[:, ::2], SWIGLU_LIMIT)
        up = jnp.clip(gu[:, 1::2], -SWIGLU_LIMIT, SWIGLU_LIMIT)
        hid = (up + 1.0) * (gt * jax.nn.sigmoid(gt * SWIGLU_ALPHA))
        return hid @ w_down[e] + b_down[e]

    ys = lax.map(expert_block, (xs, blk_e)).reshape(P, D)
    out = jnp.zeros((T + 1, D), F32).at[slot_tok].add(ys.astype(F32) * slot_gate[:, None])[:T]
    return out.reshape(B, S, D).astype(x.dtype)


def setup_inputs(seed: int = 0) -> dict:
    key = jax.random.key(seed)
    ks = iter(jax.random.split(key, 48))

    def nrm(shape, scale=1.0):
        return jax.random.normal(next(ks), shape, F32) * scale

    def gain(shape):
        return 1.0 + nrm(shape, 0.02)

    d_ab_in = sum(AB_SPLITS)
    d_cd_in = sum(CD_SPLITS)
    n_pool = len(POOL_WINDOWS)
    x = nrm((BATCH, SEQ, D_MODEL))
    p = nrm((DEPTH, BATCH, SEQ, PLE_DIM))
    ab_w_in = nrm((N_EVEN, D_MODEL, d_ab_in), D_MODEL ** -0.5)
    ab_q_norm_g = gain((N_EVEN, R_Q))
    ab_kv_norm_g = gain((N_EVEN, R_KV))
    ab_w_uq = nrm((N_EVEN, R_Q, H_A, D_NOPE), R_Q ** -0.5)
    ab_w_uk = nrm((N_EVEN, R_KV, H_A, D_NOPE), R_KV ** -0.5)
    ab_w_uv = nrm((N_EVEN, R_KV, H_A, D_VA), R_KV ** -0.5)
    ab_w_qidx = nrm((N_EVEN, R_Q, H_IDX, D_IDX), R_Q ** -0.5)
    ab_kidx_norm_g = gain((N_EVEN, D_IDX))
    ab_kidx_norm_b = nrm((N_EVEN, D_IDX), 0.02)
    ab_conv_w = nrm((N_EVEN, CONV_K, 3 * H_B * D_B), CONV_K ** -0.5)
    ab_a_log = jnp.log(jax.random.uniform(next(ks), (N_EVEN, H_B), F32, 1.0, 16.0))
    dt = jnp.exp(jax.random.uniform(next(ks), (N_EVEN, H_B), F32, math.log(1e-3), math.log(1e-1)))
    ab_dt_bias = dt + jnp.log(-jnp.expm1(-dt))
    ab_out_norm_g = gain((N_EVEN, D_B))
    ab_w_out = nrm((N_EVEN, W_AB, D_MODEL), W_AB ** -0.5 * BETA)
    cd_w_in = nrm((N_ODD, D_MODEL, d_cd_in), D_MODEL ** -0.5)
    cd_pool_w = nrm((N_ODD, n_pool, POOL_GROUP, POOL_GROUP), POOL_GROUP ** -0.5)
    cd_pool_scale = gain((N_ODD, n_pool * POOL_GROUP))
    cd_w_out = nrm((N_ODD, W_CD, D_MODEL), W_CD ** -0.5 * BETA)
    ln_mix_g = gain((DEPTH, D_MODEL))
    ln_mix_b = nrm((DEPTH, D_MODEL), 0.02)
    router_w = nrm((DEPTH, D_MODEL, N_EXPERTS), D_MODEL ** -0.5)
    router_b = nrm((DEPTH, N_EXPERTS), 0.01)
    w_gu = nrm((DEPTH, N_EXPERTS, D_MODEL, 2 * D_EXPERT), D_MODEL ** -0.5)
    b_gu = nrm((DEPTH, N_EXPERTS, 2 * D_EXPERT), 0.01)
    w_down = nrm((DEPTH, N_EXPERTS, D_EXPERT, D_MODEL), D_EXPERT ** -0.5 * BETA)
    b_down = nrm((DEPTH, N_EXPERTS, D_MODEL), 0.01)
    ple_w_proj = nrm((DEPTH, PLE_DIM, D_MODEL), PLE_DIM ** -0.5)
    ple_w_gate = nrm((DEPTH, D_MODEL, D_MODEL), D_MODEL ** -0.5)
    ln_ffn_g = gain((DEPTH, D_MODEL))
    ln_ffn_b = nrm((DEPTH, D_MODEL), 0.02)
    return {
        'x': x, 'p': p,
        'ab_w_in': ab_w_in, 'ab_q_norm_g': ab_q_norm_g, 'ab_kv_norm_g': ab_kv_norm_g,
        'ab_w_uq': ab_w_uq, 'ab_w_uk': ab_w_uk, 'ab_w_uv': ab_w_uv, 'ab_w_qidx': ab_w_qidx,
        'ab_kidx_norm_g': ab_kidx_norm_g, 'ab_kidx_norm_b': ab_kidx_norm_b, 'ab_conv_w': ab_conv_w,
        'ab_a_log': ab_a_log, 'ab_dt_bias': ab_dt_bias, 'ab_out_norm_g': ab_out_norm_g, 'ab_w_out': ab_w_out,
        'cd_w_in': cd_w_in, 'cd_pool_w': cd_pool_w, 'cd_pool_scale': cd_pool_scale, 'cd_w_out': cd_w_out,
        'ln_mix_g': ln_mix_g, 'ln_mix_b': ln_mix_b, 'router_w': router_w, 'router_b': router_b,
        'w_gu': w_gu, 'b_gu': b_gu, 'w_down': w_down, 'b_down': b_down,
        'ple_w_proj': ple_w_proj, 'ple_w_gate': ple_w_gate, 'ln_ffn_g': ln_ffn_g, 'ln_ffn_b': ln_ffn_b,
    }


def reference(x, p, ab_w_in, ab_q_norm_g, ab_kv_norm_g, ab_w_uq, ab_w_uk, ab_w_uv, ab_w_qidx,
              ab_kidx_norm_g, ab_kidx_norm_b, ab_conv_w, ab_a_log, ab_dt_bias, ab_out_norm_g, ab_w_out,
              cd_w_in, cd_pool_w, cd_pool_scale, cd_w_out, ln_mix_g, ln_mix_b, router_w, router_b,
              w_gu, b_gu, w_down, b_down, ple_w_proj, ple_w_gate, ln_ffn_g, ln_ffn_b):
    h = x
    for i in range(DEPTH):
        j = i // 2
        if i % 2 == 0:
            mix = _mixer_ab(h, ab_w_in[j], ab_q_norm_g[j], ab_kv_norm_g[j], ab_w_uq[j], ab_w_uk[j],
                            ab_w_uv[j], ab_w_qidx[j], ab_kidx_norm_g[j], ab_kidx_norm_b[j],
                            ab_conv_w[j], ab_a_log[j], ab_dt_bias[j], ab_out_norm_g[j], ab_w_out[j])
        else:
            mix = _mixer_cd(h, cd_w_in[j], cd_pool_w[j], cd_pool_scale[j], cd_w_out[j])
        h = _layernorm(ALPHA * h + mix, ln_mix_g[i], ln_mix_b[i])
        ffn = _moe(h, router_w[i], router_b[i], w_gu[i], b_gu[i], w_down[i], b_down[i])
        ple = jax.nn.sigmoid(h @ ple_w_gate[i]) * (p[i] @ ple_w_proj[i])
        h = _layernorm(ALPHA * h + ffn + ple, ln_ffn_g[i], ln_ffn_b[i])
    return h
```

```python
import functools
import math

import jax
import jax.numpy as jnp
from jax import lax
from jax.experimental import pallas as pl
from jax.experimental.pallas import tpu as pltpu

F32 = jnp.float32
BF16 = jnp.bfloat16
I32 = jnp.int32

DEPTH = 2
H_A, D_NOPE, D_VA, R_Q, R_KV, H_IDX, D_IDX = 8, 64, 64, 256, 128, 8, 64
TOPK_MAX = 256
H_B, D_B, CONV_K, DN_CHUNK = 4, 128, 4, 64
H_C, DH_C = 12, 64
DILATED_PATTERNS = ((128, 1), (512, 4), (2048, 16))
POOL_WINDOWS = (2, 4, 8, 16)
POOL_GROUP = 64
N_EXPERTS, TOP_K = 32, 4
SWIGLU_LIMIT, SWIGLU_ALPHA = 7.0, 1.702
ALPHA = (2 * DEPTH) ** 0.25

VMEM_LIMIT_BYTES = 56 * 1024 * 1024
HIGHEST = lax.Precision.HIGHEST
NEG_INF = float("-inf")
INT_MIN = -2 ** 31
KEY_NEG_INF = (0xFF800000 ^ 0x7FFFFFFF) - 2 ** 32


def _cparams(sem):
    return pltpu.CompilerParams(dimension_semantics=sem, vmem_limit_bytes=VMEM_LIMIT_BYTES)


def _dot(a, b, precision=None):
    return jnp.dot(a, b, preferred_element_type=F32, precision=precision)


def _dot_nt(a, b, precision=None):
    return lax.dot_general(a, b, (((1,), (1,)), ((), ())), preferred_element_type=F32,
                           precision=precision)


def _dot_tn(a, b, precision=None):
    return lax.dot_general(a, b, (((0,), (0,)), ((), ())), preferred_element_type=F32,
                           precision=precision)


def _sigmoid(x):
    return 1.0 / (1.0 + jnp.exp(-x))


def _silu(x):
    return x * _sigmoid(x)


def _layernorm_rows(x, g, b, eps=1e-5):
    mu = jnp.mean(x, axis=-1, keepdims=True)
    xc = x - mu
    var = jnp.mean(xc * xc, axis=-1, keepdims=True)
    return xc * lax.rsqrt(var + eps) * g + b


AB_SMALL = 128


def _ab_proj_kernel(x_ref, w_ref, gq_ref, gkv_ref, gk_ref, bk_ref,
                    cq_ref, ckv_ref, kidx_ref, small_ref, qkvz_ref):
    x = x_ref[...].astype(BF16)
    cq = _dot(x, w_ref[:, 0:R_Q])
    cq = cq * lax.rsqrt(jnp.mean(cq * cq, axis=-1, keepdims=True) + 1e-6) * gq_ref[...]
    cq_ref[...] = cq.astype(cq_ref.dtype)
    ckv = _dot(x, w_ref[:, R_Q:R_Q + R_KV])
    ckv = ckv * lax.rsqrt(jnp.mean(ckv * ckv, axis=-1, keepdims=True) + 1e-6) * gkv_ref[...]
    ckv_ref[...] = ckv.astype(ckv_ref.dtype)
    off = R_Q + R_KV
    sm = _dot(x, w_ref[:, off:off + AB_SMALL])
    small_ref[...] = sm
    lane = lax.broadcasted_iota(I32, sm.shape, 1)
    is_k = lane < D_IDX
    mu = jnp.sum(jnp.where(is_k, sm, 0.0), axis=-1, keepdims=True) * (1.0 / D_IDX)
    xc = jnp.where(is_k, sm - mu, 0.0)
    var = jnp.sum(xc * xc, axis=-1, keepdims=True) * (1.0 / D_IDX)
    kn = xc * lax.rsqrt(var + 1e-5) * gk_ref[...] + bk_ref[...]
    kidx_ref[...] = kn[:, :D_IDX].astype(kidx_ref.dtype)
    off += AB_SMALL
    qkvz_ref[...] = _dot(x, w_ref[:, off:])


def _ab_in_proj(h2d, w_in, q_norm_g, kv_norm_g, kidx_g, kidx_b, tm=512):
    T, D = h2d.shape
    W = H_B * D_B
    o = [0, R_Q, R_Q + R_KV, R_Q + R_KV + D_IDX, R_Q + R_KV + D_IDX + H_IDX]
    o_q = o[4]
    o_b = o_q + 4 * W
    pad = AB_SMALL - (D_IDX + H_IDX + 2 * H_B)
    w_perm = jnp.concatenate([
        w_in[:, o[0]:o[2]],
        w_in[:, o[2]:o[4]], w_in[:, o_b:o_b + 2 * H_B],
        jnp.zeros((D, pad), w_in.dtype),
        w_in[:, o_q:o_b],
    ], axis=1).astype(BF16)
    n_all = w_perm.shape[1]
    gk = jnp.zeros((1, AB_SMALL), F32).at[0, :D_IDX].set(kidx_g)
    bk = jnp.zeros((1, AB_SMALL), F32).at[0, :D_IDX].set(kidx_b)
    row = lambda i: (i, 0)
    fixed = lambda i: (0, 0)
    return pl.pallas_call(
        _ab_proj_kernel,
        grid=(T // tm,),
        in_specs=[pl.BlockSpec((tm, D), row), pl.BlockSpec((D, n_all), fixed),
                  pl.BlockSpec((1, R_Q), fixed), pl.BlockSpec((1, R_KV), fixed),
                  pl.BlockSpec((1, AB_SMALL), fixed), pl.BlockSpec((1, AB_SMALL), fixed)],
        out_specs=[pl.BlockSpec((tm, R_Q), row), pl.BlockSpec((tm, R_KV), row),
                   pl.BlockSpec((tm, D_IDX), row), pl.BlockSpec((tm, AB_SMALL), row),
                   pl.BlockSpec((tm, 4 * W), row)],
        out_shape=[jax.ShapeDtypeStruct((T, R_Q), BF16), jax.ShapeDtypeStruct((T, R_KV), BF16),
                   jax.ShapeDtypeStruct((T, D_IDX), BF16), jax.ShapeDtypeStruct((T, AB_SMALL), F32),
                   jax.ShapeDtypeStruct((T, 4 * W), F32)],
        compiler_params=_cparams(("parallel",)),
        name="ab_in_proj",
    )(h2d, w_perm, q_norm_g.reshape(1, -1), kv_norm_g.reshape(1, -1), gk, bk)


def _sort_key(x):
    bits = pltpu.bitcast(x + 0.0, I32)
    return jnp.where(bits < 0, bits ^ 0x7FFFFFFF, bits)


def _dsa_kernel(cq_ref, ckv_ref, kidx_ref, widx_ref, wqidx_ref, wuq_ref, wukbd_ref, wuvbd_ref,
                o_ref, key_ref, m_ref, l_ref, acc_ref, *, qb, kc, topk, seq_bits):
    i = pl.program_id(1)
    nck = ((i + 1) * qb + kc - 1) // kc
    n_lt = kc // 128
    cq = cq_ref[0]
    widx = widx_ref[0] * (H_IDX ** -0.5 * D_IDX ** -0.5)
    q_pos = i * qb + lax.broadcasted_iota(I32, (qb, 1), 0)

    qidx = [_dot(cq, wqidx_ref[h]).astype(BF16) for h in range(H_IDX)]

    def idx_body(c, carry):
        kblk = kidx_ref[0, pl.ds(pl.multiple_of(c * kc, kc), kc), :]
        isc = jnp.zeros((qb, kc), F32)
        for h in range(H_IDX):
            sc = _dot_nt(qidx[h], kblk)
            isc = isc + jnp.maximum(sc, 0.0) * widx[:, h:h + 1]
        k_pos = c * kc + lax.broadcasted_iota(I32, (1, kc), 1)
        key_ref[c] = jnp.where(k_pos <= q_pos, _sort_key(isc), KEY_NEG_INF)
        return carry

    lax.fori_loop(0, nck, idx_body, 0)

    def count_where(pred):
        def body(c, acc):
            k = key_ref[c]
            for j in range(n_lt):
                acc = acc + jnp.where(pred(k[:, j * 128:(j + 1) * 128], c, j), 1, 0)
            return acc
        acc = lax.fori_loop(0, nck, body, jnp.zeros((qb, 128), I32))
        return jnp.sum(acc, axis=1, keepdims=True)

    def bit_body(b, t):
        cand = t + (jnp.int32(1) << (31 - b))
        cnt = count_where(lambda k, c, j: k >= cand)
        return jnp.where(cnt >= topk, cand, t)

    thr = lax.fori_loop(0, 32, bit_body, jnp.full((qb, 1), INT_MIN, I32))
    thr = jnp.maximum(thr, KEY_NEG_INF + 1)

    c_gt = count_where(lambda k, c, j: k > thr)
    c_ge = count_where(lambda k, c, j: k >= thr)
    need = topk - c_gt
    excess = jnp.max(c_ge - c_gt - need)

    @pl.when(excess > 0)
    def _():
        def idx_of(c, j):
            return c * kc + j * 128 + lax.broadcasted_iota(I32, (1, 128), 1)

        def pos_body(b, m):
            cand = m + (jnp.int32(1) << (seq_bits - 1 - b))
            cnt = count_where(lambda k, c, j: (k == thr) & (idx_of(c, j) < cand))
            return jnp.where(cnt < need, cand, m)

        m = lax.fori_loop(0, seq_bits, pos_body, jnp.zeros((qb, 1), I32))

        def demote(c, carry):
            k = key_ref[c]
            idx = c * kc + lax.broadcasted_iota(I32, (1, kc), 1)
            key_ref[c] = jnp.where((k == thr) & (idx > m), thr - 1, k)
            return carry

        lax.fori_loop(0, nck, demote, 0)

    q = _dot(cq, wuq_ref[...]).astype(BF16)
    qlat = (_dot(q, wukbd_ref[...]) * (D_NOPE ** -0.5)).astype(BF16)
    m_ref[...] = jnp.full(m_ref.shape, -1e30, F32)
    l_ref[...] = jnp.zeros(l_ref.shape, F32)
    acc_ref[...] = jnp.zeros(acc_ref.shape, F32)

    def att_body(c, carry):
        kv = ckv_ref[0, pl.ds(pl.multiple_of(c * kc, kc), kc), :]
        bias = jnp.where(key_ref[c] >= thr, 0.0, NEG_INF)
        for h in range(H_A):
            s = _dot_nt(qlat[:, h * R_KV:(h + 1) * R_KV], kv) + bias
            m_old = m_ref[h]
            m_new = jnp.maximum(m_old, jnp.max(s, axis=1, keepdims=True))
            p = jnp.exp(s - m_new[:, 0:1])
            a = jnp.exp(m_old - m_new)
            l_ref[h] = a * l_ref[h] + jnp.sum(p, axis=1, keepdims=True)
            acc_ref[h] = a * acc_ref[h] + _dot(p.astype(BF16), kv)
            m_ref[h] = m_new
        return carry

    lax.fori_loop(0, nck, att_body, 0)
    o_lat = jnp.concatenate([acc_ref[h] / l_ref[h] for h in range(H_A)], axis=1)
    o_ref[0] = _dot(o_lat.astype(BF16), wuvbd_ref[...]).astype(o_ref.dtype)


def _block_diag(blocks):
    n = len(blocks)
    r, c = blocks[0].shape
    out = jnp.zeros((n * r, n * c), blocks[0].dtype)
    for k, blk in enumerate(blocks):
        out = out.at[k * r:(k + 1) * r, k * c:(k + 1) * c].set(blk)
    return out


def _dsa(cq, ckv, kidx, widx, w_uq, w_uk, w_uv, w_qidx, qb=128, kc=512):
    B, S, _ = cq.shape
    kc = min(kc, S)
    topk = min(TOPK_MAX, S // 4)
    assert S % kc == 0 and kc % qb == 0 and kc >= topk
    seq_bits = max(1, (S - 1).bit_length())
    wqidx = jnp.transpose(w_qidx, (1, 0, 2)).astype(BF16)
    wuq = w_uq.reshape(R_Q, H_A * D_NOPE).astype(BF16)
    wukbd = _block_diag([w_uk[:, h, :].T for h in range(H_A)]).astype(BF16)
    wuvbd = _block_diag([w_uv[:, h, :] for h in range(H_A)]).astype(BF16)
    kern = functools.partial(_dsa_kernel, qb=qb, kc=kc, topk=topk, seq_bits=seq_bits)
    blk_q = lambda b, i: (b, i, 0)
    seq = lambda b, i: (b, 0, 0)
    fix2 = lambda b, i: (0, 0)
    fix3 = lambda b, i: (0, 0, 0)
    return pl.pallas_call(
        kern,
        grid=(B, S // qb),
        in_specs=[pl.BlockSpec((1, qb, R_Q), blk_q), pl.BlockSpec((1, S, R_KV), seq),
                  pl.BlockSpec((1, S, D_IDX), seq), pl.BlockSpec((1, qb, H_IDX), blk_q),
                  pl.BlockSpec(wqidx.shape, fix3), pl.BlockSpec(wuq.shape, fix2),
                  pl.BlockSpec(wukbd.shape, fix2), pl.BlockSpec(wuvbd.shape, fix2)],
        out_specs=pl.BlockSpec((1, qb, H_A * D_VA), blk_q),
        out_shape=jax.ShapeDtypeStruct((B, S, H_A * D_VA), BF16),
        scratch_shapes=[pltpu.VMEM((S // kc, qb, kc), I32),
                        pltpu.VMEM((H_A, qb, R_KV), F32), pltpu.VMEM((H_A, qb, R_KV), F32),
                        pltpu.VMEM((H_A, qb, R_KV), F32)],
        compiler_params=_cparams(("parallel", "arbitrary")),
        name="dsa_attention",
    )(cq, ckv, kidx, widx, wqidx, wuq, wukbd, wuvbd)


GDN_HALO = 8


def _softplus(x):
    return jnp.maximum(x, 0.0) + jnp.log1p(jnp.exp(-jnp.abs(x)))


def _gdn_kernel(qkvz_ref, abc_ref, abr_ref, convw_ref, prm_c_ref, prm_r_ref, ng_ref, tri_ref,
                o_ref, xbuf_ref, state_ref, conv_ref, *, cb):
    C = DN_CHUNK
    W = H_B * D_B
    j = pl.program_id(1)

    @pl.when(j == 0)
    def _():
        xbuf_ref[0:GDN_HALO, :] = jnp.zeros((GDN_HALO, 3 * W), F32)
        state_ref[...] = jnp.zeros(state_ref.shape, F32)

    xbuf_ref[GDN_HALO:, :] = qkvz_ref[0, :, 0:3 * W]
    acc = xbuf_ref[GDN_HALO:, :] * convw_ref[CONV_K - 1:CONV_K, :]
    for t in range(CONV_K - 1):
        sh = CONV_K - 1 - t
        acc = acc + xbuf_ref[pl.ds(GDN_HALO - sh, cb), :] * convw_ref[t:t + 1, :]
    conv_ref[...] = _silu(acc)
    xbuf_ref[0:GDN_HALO, :] = xbuf_ref[cb:cb + GDN_HALO, :]

    abc = abc_ref[0]
    abr = abr_ref[0]
    beta_c = _sigmoid(abc)
    g_c = -jnp.exp(prm_r_ref[0:1, :]) * _softplus(abc + prm_r_ref[1:2, :])
    g_r = -jnp.exp(prm_c_ref[:, 0:1]) * _softplus(abr + prm_c_ref[:, 1:2])
    gc_c = _dot(tri_ref[...], g_c, precision=HIGHEST)
    gc_r = _dot_nt(g_r, tri_ref[...], precision=HIGHEST)

    ri = lax.broadcasted_iota(I32, (C, C), 0)
    ci = lax.broadcasted_iota(I32, (C, C), 1)
    lower = ri >= ci
    strict = ri > ci
    eye = jnp.where(ri == ci, 1.0, 0.0)

    for n in range(cb // C):
        r0 = n * C
        for h in range(H_B):
            q = conv_ref[r0:r0 + C, h * D_B:(h + 1) * D_B]
            k = conv_ref[r0:r0 + C, W + h * D_B:W + (h + 1) * D_B]
            v = conv_ref[r0:r0 + C, 2 * W + h * D_B:2 * W + (h + 1) * D_B]
            q = q * lax.rsqrt(jnp.sum(q * q, axis=-1, keepdims=True) + 1e-6) * (D_B ** -0.5)
            k = k * lax.rsqrt(jnp.sum(k * k, axis=-1, keepdims=True) + 1e-6)
            beta = beta_c[r0:r0 + C, h:h + 1]
            gcol = gc_c[r0:r0 + C, H_B + h:H_B + h + 1]
            grow = gc_r[H_B + h:H_B + h + 1, r0:r0 + C]
            glast = gcol[C - 1:C, :]
            decay = jnp.exp(jnp.where(lower, gcol - grow, NEG_INF))
            kb = k * beta
            vb = v * beta
            a_mat = jnp.where(strict, _dot_nt(kb, k, precision=HIGHEST) * decay, 0.0)
            xm = -a_mat
            t_mat = eye + xm
            for _ in range(int(math.log2(C)) - 1):
                xm = _dot(xm, xm, precision=HIGHEST)
                t_mat = t_mat + _dot(t_mat, xm, precision=HIGHEST)
            egc = jnp.exp(gcol)
            u = _dot(t_mat, vb, precision=HIGHEST)
            w = _dot(t_mat, kb * egc, precision=HIGHEST)
            qk = jnp.where(lower, _dot_nt(q, k, precision=HIGHEST) * decay, 0.0)
            q_dec = q * egc
            k_dec = k * jnp.exp(glast - gcol)
            st = state_ref[h]
            v_new = u - _dot(w, st, precision=HIGHEST)
            o = _dot(q_dec, st, precision=HIGHEST) + _dot(qk, v_new, precision=HIGHEST)
            state_ref[h] = st * jnp.exp(glast) + _dot_tn(k_dec, v_new, precision=HIGHEST)
            o = o * lax.rsqrt(jnp.mean(o * o, axis=-1, keepdims=True) + 1e-6) * ng_ref[...]
            z = qkvz_ref[0, r0:r0 + C, 3 * W + h * D_B:3 * W + (h + 1) * D_B]
            o_ref[0, r0:r0 + C, h * D_B:(h + 1) * D_B] = (o * _silu(z)).astype(o_ref.dtype)


def _gdn(qkvz, b, a, conv_w, a_log, dt_bias, norm_g, cb=256):
    B, S, _ = qkvz.shape
    W = H_B * D_B
    cb = min(cb, S)
    abc = jnp.concatenate([b, a], axis=-1)
    abr = jnp.transpose(abc, (0, 2, 1))
    zeros = jnp.zeros((H_B,), F32)
    prm = jnp.stack([jnp.concatenate([zeros, a_log]), jnp.concatenate([zeros, dt_bias])])
    idx = jnp.arange(cb)
    tri = ((idx[:, None] >= idx[None, :]) & (idx[:, None] // DN_CHUNK == idx[None, :] // DN_CHUNK)).astype(F32)
    kern = functools.partial(_gdn_kernel, cb=cb)
    blk = lambda bi, j: (bi, j, 0)
    fix = lambda bi, j: (0, 0)
    return pl.pallas_call(
        kern,
        grid=(B, S // cb),
        in_specs=[pl.BlockSpec((1, cb, 4 * W), blk), pl.BlockSpec((1, cb, 2 * H_B), blk),
                  pl.BlockSpec((1, 2 * H_B, cb), lambda bi, j: (bi, 0, j)),
                  pl.BlockSpec((CONV_K, 3 * W), fix), pl.BlockSpec((2 * H_B, 2), fix),
                  pl.BlockSpec((2, 2 * H_B), fix), pl.BlockSpec((1, D_B), fix),
                  pl.BlockSpec((cb, cb), fix)],
        out_specs=pl.BlockSpec((1, cb, W), blk),
        out_shape=jax.ShapeDtypeStruct((B, S, W), BF16),
        scratch_shapes=[pltpu.VMEM((cb + GDN_HALO, 3 * W), F32), pltpu.VMEM((H_B, D_B, D_B), F32),
                        pltpu.VMEM((cb, 3 * W), F32)],
        compiler_params=_cparams(("parallel", "arbitrary")),
        name="gated_deltanet",
    )(qkvz, abc, abr, conv_w, prm.T, prm, norm_g.reshape(1, D_B), tri)


def _post_mixer_kernel(a1_ref, a2_ref, h_ref, w1_ref, w2_ref, g_ref, b_ref, rw_ref, rb_ref, triu_ref,
                       h1_ref, h1b_ref, tope_ref, gate_ref, rank_ref, cnt_ref, carry_ref):
    i = pl.program_id(0)
    E, tm = rw_ref.shape[0], h_ref.shape[0]

    @pl.when(i == 0)
    def _():
        carry_ref[...] = jnp.zeros(carry_ref.shape, F32)

    mix = _dot(a1_ref[...].astype(BF16), w1_ref[...]) + _dot(a2_ref[...].astype(BF16), w2_ref[...])
    h1 = _layernorm_rows(ALPHA * h_ref[...] + mix, g_ref[...], b_ref[...])
    h1_ref[...] = h1
    h1b_ref[...] = h1.astype(BF16)

    logits = _dot_nt(rw_ref[...], h1, precision=HIGHEST) + rb_ref[...]
    erow = lax.broadcasted_iota(I32, (E, tm), 0)
    sel = jnp.zeros((E, tm), F32)
    onehots, tops = [], []
    for k in range(TOP_K):
        mx = jnp.max(logits, axis=0, keepdims=True)
        idx = jnp.min(jnp.where(logits == mx, erow, E), axis=0, keepdims=True)
        oh = erow == idx
        logits = jnp.where(oh, NEG_INF, logits)
        sel = sel + jnp.where(oh, 1.0, 0.0)
        onehots.append(oh)
        tops.append(mx)
        tope_ref[k:k + 1, :] = idx
    exps = [jnp.exp(t - tops[0]) for t in tops]
    den = exps[0] + exps[1] + exps[2] + exps[3]
    for k in range(TOP_K):
        gate_ref[k:k + 1, :] = exps[k] / den
    incl = _dot(sel.astype(BF16), triu_ref[...])
    excl = incl - sel + carry_ref[:, 0:1]
    for k in range(TOP_K):
        rank_ref[k:k + 1, :] = jnp.sum(jnp.where(onehots[k], excl, 0.0), axis=0, keepdims=True).astype(I32)
    carry_ref[...] = carry_ref[...] + jnp.sum(sel, axis=1, keepdims=True)
    cnt_ref[...] = carry_ref[...]


def _post_mixer(a1, a2, h2d, w_out, ln_g, ln_b, router_w, router_b, tm=512):
    T, D = h2d.shape
    E = router_w.shape[1]
    n1, n2 = a1.shape[1], a2.shape[1]
    w1 = w_out[:n1].astype(BF16)
    w2 = w_out[n1:].astype(BF16)
    idx = jnp.arange(tm)
    triu = (idx[:, None] <= idx[None, :]).astype(BF16)
    row = lambda i: (i, 0)
    col = lambda i: (0, i)
    fix = lambda i: (0, 0)
    return pl.pallas_call(
        _post_mixer_kernel,
        grid=(T // tm,),
        in_specs=[pl.BlockSpec((tm, n1), row), pl.BlockSpec((tm, n2), row), pl.BlockSpec((tm, D), row),
                  pl.BlockSpec((n1, D), fix), pl.BlockSpec((n2, D), fix),
                  pl.BlockSpec((1, D), fix), pl.BlockSpec((1, D), fix),
                  pl.BlockSpec((E, D), fix), pl.BlockSpec((E, 1), fix), pl.BlockSpec((tm, tm), fix)],
        out_specs=[pl.BlockSpec((tm, D), row), pl.BlockSpec((tm, D), row),
                   pl.BlockSpec((TOP_K, tm), col), pl.BlockSpec((TOP_K, tm), col),
                   pl.BlockSpec((TOP_K, tm), col), pl.BlockSpec((E, 128), fix)],
        out_shape=[jax.ShapeDtypeStruct((T, D), F32), jax.ShapeDtypeStruct((T, D), BF16),
                   jax.ShapeDtypeStruct((TOP_K, T), I32), jax.ShapeDtypeStruct((TOP_K, T), F32),
                   jax.ShapeDtypeStruct((TOP_K, T), I32), jax.ShapeDtypeStruct((E, 128), F32)],
        scratch_shapes=[pltpu.VMEM((E, 128), F32)],
        compiler_params=_cparams(("arbitrary",)),
        name="post_mixer_router",
    )(a1, a2, h2d, w1, w2, ln_g.reshape(1, D), ln_b.reshape(1, D), router_w.T, router_b.reshape(E, 1), triu)


MOE_BLOCK = 256


def _moe_kernel(be_ref, nu_ref, x_ref, wg_ref, wu_ref, bg_ref, bu_ref, wd_ref, bd_ref, y_ref):
    i = pl.program_id(0)

    @pl.when(i < nu_ref[0])
    def _():
        x = x_ref[...]
        g = _dot(x, wg_ref[0]) + bg_ref[0]
        u = _dot(x, wu_ref[0]) + bu_ref[0]
        gt = jnp.minimum(g, SWIGLU_LIMIT)
        up = jnp.clip(u, -SWIGLU_LIMIT, SWIGLU_LIMIT)
        hid = (up + 1.0) * (gt * _sigmoid(gt * SWIGLU_ALPHA))
        y_ref[...] = _dot(hid.astype(BF16), wd_ref[0]) + bd_ref[0]

    @pl.when(i >= nu_ref[0])
    def _():
        y_ref[...] = jnp.zeros(y_ref.shape, y_ref.dtype)


def _moe_experts(xs, blk_e, n_used, wg, wu, bg, bu, wd, bd):
    P, D = xs.shape
    F = wg.shape[2]
    bm = MOE_BLOCK
    wsel = lambda i, be, nu: (be[i], 0, 0)
    row = lambda i, be, nu: (i, 0)
    return pl.pallas_call(
        _moe_kernel,
        grid_spec=pltpu.PrefetchScalarGridSpec(
            num_scalar_prefetch=2,
            grid=(P // bm,),
            in_specs=[pl.BlockSpec((bm, D), row),
                      pl.BlockSpec((1, D, F), wsel), pl.BlockSpec((1, D, F), wsel),
                      pl.BlockSpec((1, 1, F), wsel), pl.BlockSpec((1, 1, F), wsel),
                      pl.BlockSpec((1, F, D), wsel), pl.BlockSpec((1, 1, D), wsel)],
            out_specs=pl.BlockSpec((bm, D), row)),
        out_shape=jax.ShapeDtypeStruct((P, D), F32),
        compiler_params=_cparams(("arbitrary",)),
        name="moe_experts",
    )(blk_e, n_used, xs, wg, wu, bg, bu, wd, bd)


def _moe(h1b, top_e, rank, cnt, w_gu, b_gu, w_down, b_down):
    T, D = h1b.shape
    E = w_gu.shape[0]
    bm = MOE_BLOCK
    counts = cnt[:, 0].astype(I32)
    padded = (counts + bm - 1) // bm * bm
    pad_end = jnp.cumsum(padded)
    pad_start = pad_end - padded
    dest = pad_start[top_e] + rank
    nblk = (T * TOP_K) // bm + E
    P = nblk * bm
    tok = jnp.tile(jnp.arange(T, dtype=I32), TOP_K)
    slot_tok = jnp.full((P,), T, I32).at[dest.reshape(-1)].set(tok)
    blk_e = jnp.minimum(jnp.searchsorted(pad_end, jnp.arange(nblk, dtype=I32) * bm, side='right'),
                        E - 1).astype(I32)
    n_used = (pad_end[-1:] // bm).astype(I32)
    xs = jnp.concatenate([h1b, jnp.zeros((1, D), h1b.dtype)], 0)[slot_tok]
    wg = w_gu[:, :, 0::2].astype(BF16)
    wu = w_gu[:, :, 1::2].astype(BF16)
    bg = b_gu[:, None, 0::2]
    bu = b_gu[:, None, 1::2]
    ys = _moe_experts(xs, blk_e, n_used, wg, wu, bg, bu, w_down.astype(BF16), b_down[:, None, :])
    return ys[dest]


def _tail_kernel(h1_ref, h1b_ref, yk_ref, gate_ref, p_ref, wg_ref, wp_ref, g_ref, b_ref, o_ref):
    ffn = yk_ref[0] * gate_ref[:, 0:1]
    for k in range(1, TOP_K):
        ffn = ffn + yk_ref[k] * gate_ref[:, k:k + 1]
    ple = _sigmoid(_dot(h1b_ref[...], wg_ref[...])) * _dot(p_ref[...].astype(BF16), wp_ref[...])
    o_ref[...] = _layernorm_rows(ALPHA * h1_ref[...] + ffn + ple, g_ref[...], b_ref[...])


def _layer_tail(h1, h1b, yk, gate_t, p2d, ple_w_gate, ple_w_proj, ln_g, ln_b, tm=256):
    T, D = h1.shape
    PD = p2d.shape[1]
    row = lambda i: (i, 0)
    fix = lambda i: (0, 0)
    return pl.pallas_call(
        _tail_kernel,
        grid=(T // tm,),
        in_specs=[pl.BlockSpec((tm, D), row), pl.BlockSpec((tm, D), row),
                  pl.BlockSpec((TOP_K, tm, D), lambda i: (0, i, 0)), pl.BlockSpec((tm, TOP_K), row),
                  pl.BlockSpec((tm, PD), row), pl.BlockSpec((D, D), fix), pl.BlockSpec((PD, D), fix),
                  pl.BlockSpec((1, D), fix), pl.BlockSpec((1, D), fix)],
        out_specs=pl.BlockSpec((tm, D), row),
        out_shape=jax.ShapeDtypeStruct((T, D), F32),
        compiler_params=_cparams(("parallel",)),
        name="layer_tail",
    )(h1, h1b, yk, gate_t.T, p2d, ple_w_gate.astype(BF16), ple_w_proj.astype(BF16),
      ln_g.reshape(1, D), ln_b.reshape(1, D))


def _cd_proj_kernel(x_ref, w_ref, q_ref, k_ref, v_ref, u_ref):
    x = x_ref[...].astype(BF16)
    n = H_C * DH_C
    q_ref[...] = _dot(x, w_ref[:, 0:n]).astype(q_ref.dtype)
    k_ref[...] = _dot(x, w_ref[:, n:2 * n]).astype(k_ref.dtype)
    v_ref[...] = _dot(x, w_ref[:, 2 * n:3 * n]).astype(v_ref.dtype)
    u_ref[...] = _dot(x, w_ref[:, 3 * n:])


def _cd_in_proj(h2d, w_in, tm=512):
    T, D = h2d.shape
    n = H_C * DH_C
    nu = w_in.shape[1] - 3 * n
    row = lambda i: (i, 0)
    return pl.pallas_call(
        _cd_proj_kernel,
        grid=(T // tm,),
        in_specs=[pl.BlockSpec((tm, D), row), pl.BlockSpec(w_in.shape, lambda i: (0, 0))],
        out_specs=[pl.BlockSpec((tm, n), row)] * 3 + [pl.BlockSpec((tm, nu), row)],
        out_shape=[jax.ShapeDtypeStruct((T, n), BF16)] * 3 + [jax.ShapeDtypeStruct((T, nu), F32)],
        compiler_params=_cparams(("parallel",)),
        name="cd_in_proj",
    )(h2d, w_in.astype(BF16))


def _dilated_bias_table(qb):
    import numpy as np
    max_w = max(w for w, _ in DILATED_PATTERNS)
    ndc = max_w // qb + 1
    r = np.arange(qb)[:, None]
    j = np.arange(qb)[None, :]
    tbl = np.empty((ndc, qb, qb), np.float32)
    for dc in range(ndc):
        delta = dc * qb + r - j
        mult = np.zeros((qb, qb), np.float64)
        for w, d in DILATED_PATTERNS:
            mult += (delta >= 0) & (delta <= w) & (delta % d == 0)
        with np.errstate(divide="ignore"):
            tbl[dc] = np.log(mult)
    return jnp.asarray(tbl)


def _dilated_kernel(q_ref, k_ref, v_ref, bias_ref, o_ref, m_ref, l_ref, acc_ref, *, qb, ndc):
    i = pl.program_id(2)
    q = q_ref[0]
    lo = lax.broadcasted_iota(I32, q.shape, 1) < DH_C
    zero = jnp.zeros(q.shape, q.dtype)
    q_lo = jnp.where(lo, q, zero)
    q_hi = jnp.where(lo, zero, q)
    m_ref[...] = jnp.full(m_ref.shape, -1e30, F32)
    l_ref[...] = jnp.zeros(l_ref.shape, F32)
    acc_ref[...] = jnp.zeros(acc_ref.shape, F32)
    scale = DH_C ** -0.5

    def body(c, carry):
        r0 = pl.multiple_of(c * qb, qb)
        kk = k_ref[0, pl.ds(r0, qb), :]
        vv = v_ref[0, pl.ds(r0, qb), :]
        bias = bias_ref[i - c]
        ps, alphas, sums = [], [], []
        for n, qh in enumerate((q_lo, q_hi)):
            s = _dot_nt(qh, kk) * scale + bias
            m_old = m_ref[n]
            m_new = jnp.maximum(m_old, jnp.max(s, axis=1, keepdims=True))
            p = jnp.exp(s - m_new[:, 0:1])
            alphas.append(jnp.exp(m_old - m_new))
            sums.append(jnp.sum(p, axis=1, keepdims=True))
            ps.append(_dot(p.astype(BF16), vv))
            m_ref[n] = m_new
        a = jnp.where(lo, alphas[0], alphas[1])
        l_ref[...] = a * l_ref[...] + jnp.where(lo, sums[0], sums[1])
        acc_ref[...] = a * acc_ref[...] + jnp.where(lo, ps[0], ps[1])
        return carry

    lax.fori_loop(jnp.maximum(i - (ndc - 1), 0), i + 1, body, 0)
    o_ref[0] = (acc_ref[...] / l_ref[...]).astype(o_ref.dtype)


def _dilated_attention(q, k, v, qb=256):
    B, S, n = q.shape
    qb = min(qb, S)
    bias = _dilated_bias_table(qb)
    ndc = bias.shape[0]
    pw = 2 * DH_C
    kern = functools.partial(_dilated_kernel, qb=qb, ndc=ndc)
    blk = lambda b, pr, i: (b, i, pr)
    seq = lambda b, pr, i: (b, 0, pr)
    return pl.pallas_call(
        kern,
        grid=(B, n // pw, S // qb),
        in_specs=[pl.BlockSpec((1, qb, pw), blk), pl.BlockSpec((1, S, pw), seq),
                  pl.BlockSpec((1, S, pw), seq), pl.BlockSpec(bias.shape, lambda b, pr, i: (0, 0, 0))],
        out_specs=pl.BlockSpec((1, qb, pw), blk),
        out_shape=jax.ShapeDtypeStruct((B, S, n), BF16),
        scratch_shapes=[pltpu.VMEM((2, qb, pw), F32), pltpu.VMEM((qb, pw), F32), pltpu.VMEM((qb, pw), F32)],
        compiler_params=_cparams(("parallel", "parallel", "arbitrary")),
        name="dilated_attention",
    )(q, k, v, bias)


POOL_HALO = 16


def _pool_kernel(u_ref, w_ref, sc_ref, o_ref, xbuf_ref, *, tm):
    j = pl.program_id(1)

    @pl.when(j == 0)
    def _():
        xbuf_ref[0:POOL_HALO, :] = jnp.zeros((POOL_HALO, xbuf_ref.shape[1]), F32)

    xbuf_ref[POOL_HALO:, :] = u_ref[0]
    x = xbuf_ref[POOL_HALO:, :]
    grp = lax.broadcasted_iota(I32, (1, x.shape[1]), 1) // POOL_GROUP
    run = x
    sel = jnp.zeros(x.shape, F32)
    win = jnp.zeros((1, x.shape[1]), F32)
    for d in range(1, max(POOL_WINDOWS)):
        run = run + xbuf_ref[pl.ds(POOL_HALO - d, tm), :]
        if d + 1 in POOL_WINDOWS:
            gi = POOL_WINDOWS.index(d + 1)
            sel = jnp.where(grp == gi, run, sel)
            win = jnp.where(grp == gi, float(d + 1), win)
    pos = j * tm + lax.broadcasted_iota(I32, (tm, 1), 0)
    mean = sel / jnp.minimum((pos + 1).astype(F32), win)
    o_ref[0] = _dot((mean - x).astype(BF16), w_ref[...]) * sc_ref[...]
    xbuf_ref[0:POOL_HALO, :] = xbuf_ref[tm:tm + POOL_HALO, :]


def _multiscale_pool(u, pool_w, pool_scale, tm=512):
    B, S, n = u.shape
    tm = min(tm, S)
    wbd = _block_diag([pool_w[g] for g in range(pool_w.shape[0])]).astype(BF16)
    blk = lambda b, j: (b, j, 0)
    fix = lambda b, j: (0, 0)
    return pl.pallas_call(
        functools.partial(_pool_kernel, tm=tm),
        grid=(B, S // tm),
        in_specs=[pl.BlockSpec((1, tm, n), blk), pl.BlockSpec((n, n), fix), pl.BlockSpec((1, n), fix)],
        out_specs=pl.BlockSpec((1, tm, n), blk),
        out_shape=jax.ShapeDtypeStruct((B, S, n), F32),
        scratch_shapes=[pltpu.VMEM((tm + POOL_HALO, n), F32)],
        compiler_params=_cparams(("parallel", "arbitrary")),
        name="multiscale_pool",
    )(u, wbd, pool_scale.reshape(1, n))


def kernel(x, p, ab_w_in, ab_q_norm_g, ab_kv_norm_g, ab_w_uq, ab_w_uk, ab_w_uv, ab_w_qidx,
           ab_kidx_norm_g, ab_kidx_norm_b, ab_conv_w, ab_a_log, ab_dt_bias, ab_out_norm_g, ab_w_out,
           cd_w_in, cd_pool_w, cd_pool_scale, cd_w_out, ln_mix_g, ln_mix_b, router_w, router_b,
           w_gu, b_gu, w_down, b_down, ple_w_proj, ple_w_gate, ln_ffn_g, ln_ffn_b):
    B, S, D = x.shape
    T = B * S
    h = x.reshape(T, D)
    for i in range(DEPTH):
        j = i // 2
        if i % 2 == 0:
            cq, ckv, kidx, small, qkvz = _ab_in_proj(h, ab_w_in[j], ab_q_norm_g[j], ab_kv_norm_g[j],
                                                     ab_kidx_norm_g[j], ab_kidx_norm_b[j])
            sh = lambda t: t.reshape(B, S, -1)
            o_w = D_IDX + H_IDX
            o_a = _dsa(sh(cq), sh(ckv), sh(kidx), sh(small[:, D_IDX:o_w]),
                       ab_w_uq[j], ab_w_uk[j], ab_w_uv[j], ab_w_qidx[j])
            o_b = _gdn(sh(qkvz), sh(small[:, o_w:o_w + H_B]), sh(small[:, o_w + H_B:o_w + 2 * H_B]),
                       ab_conv_w[j], ab_a_log[j], ab_dt_bias[j], ab_out_norm_g[j])
            a1, a2, w_out = o_a.reshape(T, -1), o_b.reshape(T, -1), ab_w_out[j]
        else:
            q, k, v, u = _cd_in_proj(h, cd_w_in[j])
            sh = lambda t: t.reshape(B, S, -1)
            o_c = _dilated_attention(sh(q), sh(k), sh(v))
            o_d = _multiscale_pool(sh(u), cd_pool_w[j], cd_pool_scale[j])
            a1, a2, w_out = o_c.reshape(T, -1), o_d.reshape(T, -1), cd_w_out[j]
        h1, h1b, top_e, gate_t, rank, cnt = _post_mixer(a1, a2, h, w_out, ln_mix_g[i], ln_mix_b[i],
                                                        router_w[i], router_b[i])
        yk = _moe(h1b, top_e, rank, cnt, w_gu[i], b_gu[i], w_down[i], b_down[i])
        h = _layer_tail(h1, h1b, yk, gate_t, p[i].reshape(T, -1), ple_w_gate[i], ple_w_proj[i],
                        ln_ffn_g[i], ln_ffn_b[i])
    return h.reshape(B, S, D)
```

```python
import functools
import math

import jax
import jax.numpy as jnp
from jax import lax
from jax.experimental import pallas as pl
from jax.experimental.pallas import tpu as pltpu

F32 = jnp.float32
BF16 = jnp.bfloat16
I32 = jnp.int32

DEPTH = 2
H_A, D_NOPE, D_VA, R_Q, R_KV, H_IDX, D_IDX = 8, 64, 64, 256, 128, 8, 64
TOPK_MAX = 256
H_B, D_B, CONV_K, DN_CHUNK = 4, 128, 4, 64
H_C, DH_C = 12, 64
DILATED_PATTERNS = ((128, 1), (512, 4), (2048, 16))
POOL_WINDOWS = (2, 4, 8, 16)
POOL_GROUP = 64
N_EXPERTS, TOP_K = 32, 4
SWIGLU_LIMIT, SWIGLU_ALPHA = 7.0, 1.702
ALPHA = (2 * DEPTH) ** 0.25

VMEM_LIMIT_BYTES = 56 * 1024 * 1024
HIGHEST = lax.Precision.HIGHEST
NEG_INF = float("-inf")
LOG2E = math.log2(math.e)
INT_MIN = -2 ** 31
KEY_NEG_INF = (0xFF800000 ^ 0x7FFFFFFF) - 2 ** 32


def _cparams(sem):
    return pltpu.CompilerParams(dimension_semantics=sem, vmem_limit_bytes=VMEM_LIMIT_BYTES)


def _dot(a, b, precision=None):
    return jnp.dot(a, b, preferred_element_type=F32, precision=precision)


def _dot_nt(a, b, precision=None):
    return lax.dot_general(a, b, (((1,), (1,)), ((), ())), preferred_element_type=F32,
                           precision=precision)


def _dot_tn(a, b, precision=None):
    return lax.dot_general(a, b, (((0,), (0,)), ((), ())), preferred_element_type=F32,
                           precision=precision)


def _sigmoid(x):
    return 1.0 / (1.0 + jnp.exp(-x))


def _silu(x):
    return x * _sigmoid(x)


def _layernorm_rows(x, g, b, eps=1e-5):
    mu = jnp.mean(x, axis=-1, keepdims=True)
    xc = x - mu
    var = jnp.mean(xc * xc, axis=-1, keepdims=True)
    return xc * lax.rsqrt(var + eps) * g + b


AB_SMALL = 128


def _ab_proj_kernel(x_ref, w_ref, gq_ref, gkv_ref, gk_ref, bk_ref,
                    cq_ref, ckv_ref, kidx_ref, small_ref, qkvz_ref):
    x = x_ref[...].astype(BF16)
    cq = _dot(x, w_ref[:, 0:R_Q])
    cq = cq * lax.rsqrt(jnp.mean(cq * cq, axis=-1, keepdims=True) + 1e-6) * gq_ref[...]
    cq_ref[...] = cq.astype(cq_ref.dtype)
    ckv = _dot(x, w_ref[:, R_Q:R_Q + R_KV])
    ckv = ckv * lax.rsqrt(jnp.mean(ckv * ckv, axis=-1, keepdims=True) + 1e-6) * gkv_ref[...]
    ckv_ref[...] = ckv.astype(ckv_ref.dtype)
    off = R_Q + R_KV
    sm = _dot(x, w_ref[:, off:off + AB_SMALL])
    small_ref[...] = sm
    lane = lax.broadcasted_iota(I32, sm.shape, 1)
    is_k = lane < D_IDX
    mu = jnp.sum(jnp.where(is_k, sm, 0.0), axis=-1, keepdims=True) * (1.0 / D_IDX)
    xc = jnp.where(is_k, sm - mu, 0.0)
    var = jnp.sum(xc * xc, axis=-1, keepdims=True) * (1.0 / D_IDX)
    kn = xc * lax.rsqrt(var + 1e-5) * gk_ref[...] + bk_ref[...]
    kidx_ref[...] = kn[:, :D_IDX].astype(kidx_ref.dtype)
    off += AB_SMALL
    qkvz_ref[...] = _dot(x, w_ref[:, off:])


def _ab_in_proj(h2d, w_in, q_norm_g, kv_norm_g, kidx_g, kidx_b, tm=512):
    T, D = h2d.shape
    W = H_B * D_B
    o = [0, R_Q, R_Q + R_KV, R_Q + R_KV + D_IDX, R_Q + R_KV + D_IDX + H_IDX]
    o_q = o[4]
    o_b = o_q + 4 * W
    pad = AB_SMALL - (D_IDX + H_IDX + 2 * H_B)
    w_perm = jnp.concatenate([
        w_in[:, o[0]:o[2]],
        w_in[:, o[2]:o[4]], w_in[:, o_b:o_b + 2 * H_B],
        jnp.zeros((D, pad), w_in.dtype),
        w_in[:, o_q:o_b],
    ], axis=1).astype(BF16)
    n_all = w_perm.shape[1]
    gk = jnp.zeros((1, AB_SMALL), F32).at[0, :D_IDX].set(kidx_g)
    bk = jnp.zeros((1, AB_SMALL), F32).at[0, :D_IDX].set(kidx_b)
    row = lambda i: (i, 0)
    fixed = lambda i: (0, 0)
    return pl.pallas_call(
        _ab_proj_kernel,
        grid=(T // tm,),
        in_specs=[pl.BlockSpec((tm, D), row), pl.BlockSpec((D, n_all), fixed),
                  pl.BlockSpec((1, R_Q), fixed), pl.BlockSpec((1, R_KV), fixed),
                  pl.BlockSpec((1, AB_SMALL), fixed), pl.BlockSpec((1, AB_SMALL), fixed)],
        out_specs=[pl.BlockSpec((tm, R_Q), row), pl.BlockSpec((tm, R_KV), row),
                   pl.BlockSpec((tm, D_IDX), row), pl.BlockSpec((tm, AB_SMALL), row),
                   pl.BlockSpec((tm, 4 * W), row)],
        out_shape=[jax.ShapeDtypeStruct((T, R_Q), BF16), jax.ShapeDtypeStruct((T, R_KV), BF16),
                   jax.ShapeDtypeStruct((T, D_IDX), BF16), jax.ShapeDtypeStruct((T, AB_SMALL), F32),
                   jax.ShapeDtypeStruct((T, 4 * W), F32)],
        compiler_params=_cparams(("parallel",)),
        name="ab_in_proj",
    )(h2d, w_perm, q_norm_g.reshape(1, -1), kv_norm_g.reshape(1, -1), gk, bk)


def _sort_key(x):
    bits = pltpu.bitcast(x + 0.0, I32)
    return jnp.where(bits < 0, bits ^ 0x7FFFFFFF, bits)


def _dsa_kernel(cq_ref, ckv_ref, kidx_ref, widx_ref, wqidx_ref, wuq_ref, wukbd_ref, wuvbd_ref,
                o_ref, key_ref, qlat_ref, m_ref, l_ref, acc_ref, *, qb, kc, topk, seq_bits):
    i = pl.program_id(1)
    nck = ((i + 1) * qb + kc - 1) // kc
    n_lt = kc // 128
    cq = cq_ref[0]
    widx = widx_ref[0] * (H_IDX ** -0.5 * D_IDX ** -0.5)
    q_pos = i * qb + lax.broadcasted_iota(I32, (qb, 1), 0)

    qidx = [_dot(cq, wqidx_ref[h]).astype(BF16) for h in range(H_IDX)]

    def idx_body(c, carry):
        kblk = kidx_ref[0, pl.ds(pl.multiple_of(c * kc, kc), kc), :]
        isc = jnp.zeros((qb, kc), F32)
        for h in range(H_IDX):
            sc = _dot_nt(qidx[h], kblk)
            isc = isc + jnp.maximum(sc, 0.0) * widx[:, h:h + 1]
        k_pos = c * kc + lax.broadcasted_iota(I32, (1, kc), 1)
        key_ref[c] = jnp.where(k_pos <= q_pos, _sort_key(isc), KEY_NEG_INF)
        return carry

    lax.fori_loop(0, nck, idx_body, 0)

    def count_where(pred):
        def body(c, acc):
            k = key_ref[c]
            for j in range(n_lt):
                acc = acc + jnp.where(pred(k[:, j * 128:(j + 1) * 128], c, j), 1, 0)
            return acc
        acc = lax.fori_loop(0, nck, body, jnp.zeros((qb, 128), I32))
        return jnp.sum(acc, axis=1, keepdims=True)

    def bit_body(b, t):
        cand = t + (jnp.int32(1) << (31 - b))
        cnt = count_where(lambda k, c, j: k >= cand)
        return jnp.where(cnt >= topk, cand, t)

    thr = lax.fori_loop(0, 32, bit_body, jnp.full((qb, 1), INT_MIN, I32))
    thr = jnp.maximum(thr, KEY_NEG_INF + 1)

    c_gt = count_where(lambda k, c, j: k > thr)
    c_ge = count_where(lambda k, c, j: k >= thr)
    need = topk - c_gt
    excess = jnp.max(c_ge - c_gt - need)

    @pl.when(excess > 0)
    def _():
        def idx_of(c, j):
            return c * kc + j * 128 + lax.broadcasted_iota(I32, (1, 128), 1)

        def pos_body(b, m):
            cand = m + (jnp.int32(1) << (seq_bits - 1 - b))
            cnt = count_where(lambda k, c, j: (k == thr) & (idx_of(c, j) < cand))
            return jnp.where(cnt < need, cand, m)

        m = lax.fori_loop(0, seq_bits, pos_body, jnp.zeros((qb, 1), I32))

        def demote(c, carry):
            k = key_ref[c]
            idx = c * kc + lax.broadcasted_iota(I32, (1, kc), 1)
            key_ref[c] = jnp.where((k == thr) & (idx > m), thr - 1, k)
            return carry

        lax.fori_loop(0, nck, demote, 0)

    q = _dot(cq, wuq_ref[...]).astype(BF16)
    qlat = _dot(q, wukbd_ref[...]) * (D_NOPE ** -0.5 * LOG2E)
    for h in range(H_A):
        qlat_ref[h * qb:(h + 1) * qb, :] = qlat[:, h * R_KV:(h + 1) * R_KV].astype(BF16)
    m_ref[...] = jnp.full(m_ref.shape, -1e30, F32)
    l_ref[...] = jnp.zeros(l_ref.shape, F32)
    acc_ref[...] = jnp.zeros(acc_ref.shape, F32)

    def att_body(c, carry):
        kv = ckv_ref[0, pl.ds(pl.multiple_of(c * kc, kc), kc), :]
        bias = jnp.where(key_ref[c] >= thr, 0.0, NEG_INF)
        s = _dot_nt(qlat_ref[...], kv)
        s = (s.reshape(H_A, qb, kc) + bias[None]).reshape(H_A * qb, kc)
        tiles = [s[:, j * 128:(j + 1) * 128] for j in range(n_lt)]
        m_cur = tiles[0]
        for t in tiles[1:]:
            m_cur = jnp.maximum(m_cur, t)
        m_old = m_ref[...]
        m_new = jnp.maximum(m_old, jnp.max(m_cur, axis=1, keepdims=True))
        ps = [jnp.exp2(t - m_new) for t in tiles]
        a = jnp.exp2(m_old - m_new)
        psum = ps[0]
        for t in ps[1:]:
            psum = psum + t
        l_ref[...] = a * l_ref[...] + psum
        p = jnp.concatenate([t.astype(BF16) for t in ps], axis=1)
        acc_ref[...] = a * acc_ref[...] + _dot(p, kv)
        m_ref[...] = m_new
        return carry

    lax.fori_loop(0, nck, att_body, 0)
    o_all = acc_ref[...] / jnp.sum(l_ref[...], axis=1, keepdims=True)
    o_lat = jnp.concatenate([o_all[h * qb:(h + 1) * qb, :] for h in range(H_A)], axis=1)
    o_ref[0] = _dot(o_lat.astype(BF16), wuvbd_ref[...]).astype(o_ref.dtype)


def _block_diag(blocks):
    n = len(blocks)
    r, c = blocks[0].shape
    out = jnp.zeros((n * r, n * c), blocks[0].dtype)
    for k, blk in enumerate(blocks):
        out = out.at[k * r:(k + 1) * r, k * c:(k + 1) * c].set(blk)
    return out


def _dsa(cq, ckv, kidx, widx, w_uq, w_uk, w_uv, w_qidx, qb=128, kc=512):
    B, S, _ = cq.shape
    kc = min(kc, S)
    topk = min(TOPK_MAX, S // 4)
    assert S % kc == 0 and kc % qb == 0 and kc >= topk
    seq_bits = max(1, (S - 1).bit_length())
    wqidx = jnp.transpose(w_qidx, (1, 0, 2)).astype(BF16)
    wuq = w_uq.reshape(R_Q, H_A * D_NOPE).astype(BF16)
    wukbd = _block_diag([w_uk[:, h, :].T for h in range(H_A)]).astype(BF16)
    wuvbd = _block_diag([w_uv[:, h, :] for h in range(H_A)]).astype(BF16)
    kern = functools.partial(_dsa_kernel, qb=qb, kc=kc, topk=topk, seq_bits=seq_bits)
    blk_q = lambda b, i: (b, i, 0)
    seq = lambda b, i: (b, 0, 0)
    fix2 = lambda b, i: (0, 0)
    fix3 = lambda b, i: (0, 0, 0)
    return pl.pallas_call(
        kern,
        grid=(B, S // qb),
        in_specs=[pl.BlockSpec((1, qb, R_Q), blk_q), pl.BlockSpec((1, S, R_KV), seq),
                  pl.BlockSpec((1, S, D_IDX), seq), pl.BlockSpec((1, qb, H_IDX), blk_q),
                  pl.BlockSpec(wqidx.shape, fix3), pl.BlockSpec(wuq.shape, fix2),
                  pl.BlockSpec(wukbd.shape, fix2), pl.BlockSpec(wuvbd.shape, fix2)],
        out_specs=pl.BlockSpec((1, qb, H_A * D_VA), blk_q),
        out_shape=jax.ShapeDtypeStruct((B, S, H_A * D_VA), BF16),
        scratch_shapes=[pltpu.VMEM((S // kc, qb, kc), I32),
                        pltpu.VMEM((H_A * qb, R_KV), BF16),
                        pltpu.VMEM((H_A * qb, R_KV), F32), pltpu.VMEM((H_A * qb, R_KV), F32),
                        pltpu.VMEM((H_A * qb, R_KV), F32)],
        compiler_params=_cparams(("parallel", "arbitrary")),
        name="dsa_attention",
    )(cq, ckv, kidx, widx, wqidx, wuq, wukbd, wuvbd)


GDN_HALO = 8


def _softplus(x):
    return jnp.maximum(x, 0.0) + jnp.log1p(jnp.exp(-jnp.abs(x)))


def _gdn_kernel(qkvz_ref, abc_ref, abr_ref, convw_ref, prm_c_ref, prm_r_ref, ng_ref, tri_ref,
                o_ref, xbuf_ref, state_ref, conv_ref, *, cb):
    C = DN_CHUNK
    W = H_B * D_B
    j = pl.program_id(1)

    @pl.when(j == 0)
    def _():
        xbuf_ref[0:GDN_HALO, :] = jnp.zeros((GDN_HALO, 3 * W), F32)
        state_ref[...] = jnp.zeros(state_ref.shape, F32)

    xbuf_ref[GDN_HALO:, :] = qkvz_ref[0, :, 0:3 * W]
    acc = xbuf_ref[GDN_HALO:, :] * convw_ref[CONV_K - 1:CONV_K, :]
    for t in range(CONV_K - 1):
        sh = CONV_K - 1 - t
        acc = acc + xbuf_ref[pl.ds(GDN_HALO - sh, cb), :] * convw_ref[t:t + 1, :]
    conv_ref[...] = _silu(acc)
    xbuf_ref[0:GDN_HALO, :] = xbuf_ref[cb:cb + GDN_HALO, :]

    abc = abc_ref[0]
    abr = abr_ref[0]
    beta_c = _sigmoid(abc)
    g_c = -jnp.exp(prm_r_ref[0:1, :]) * _softplus(abc + prm_r_ref[1:2, :])
    g_r = -jnp.exp(prm_c_ref[:, 0:1]) * _softplus(abr + prm_c_ref[:, 1:2])
    gc_c = _dot(tri_ref[...], g_c, precision=HIGHEST)
    gc_r = _dot_nt(g_r, tri_ref[...], precision=HIGHEST)

    ri = lax.broadcasted_iota(I32, (C, C), 0)
    ci = lax.broadcasted_iota(I32, (C, C), 1)
    lower = ri >= ci
    strict = ri > ci
    eye = jnp.where(ri == ci, 1.0, 0.0)

    for n in range(cb // C):
        r0 = n * C
        for h in range(H_B):
            q = conv_ref[r0:r0 + C, h * D_B:(h + 1) * D_B]
            k = conv_ref[r0:r0 + C, W + h * D_B:W + (h + 1) * D_B]
            v = conv_ref[r0:r0 + C, 2 * W + h * D_B:2 * W + (h + 1) * D_B]
            q = q * lax.rsqrt(jnp.sum(q * q, axis=-1, keepdims=True) + 1e-6) * (D_B ** -0.5)
            k = k * lax.rsqrt(jnp.sum(k * k, axis=-1, keepdims=True) + 1e-6)
            beta = beta_c[r0:r0 + C, h:h + 1]
            gcol = gc_c[r0:r0 + C, H_B + h:H_B + h + 1]
            grow = gc_r[H_B + h:H_B + h + 1, r0:r0 + C]
            glast = gcol[C - 1:C, :]
            decay = jnp.exp(jnp.where(lower, gcol - grow, NEG_INF))
            kb = k * beta
            vb = v * beta
            a_mat = jnp.where(strict, _dot_nt(kb, k, precision=HIGHEST) * decay, 0.0)
            xm = -a_mat
            t_mat = eye + xm
            for _ in range(int(math.log2(C)) - 1):
                xm = _dot(xm, xm, precision=HIGHEST)
                t_mat = t_mat + _dot(t_mat, xm, precision=HIGHEST)
            egc = jnp.exp(gcol)
            u = _dot(t_mat, vb, precision=HIGHEST)
            w = _dot(t_mat, kb * egc, precision=HIGHEST)
            qk = jnp.where(lower, _dot_nt(q, k, precision=HIGHEST) * decay, 0.0)
            q_dec = q * egc
            k_dec = k * jnp.exp(glast - gcol)
            st = state_ref[h]
            v_new = u - _dot(w, st, precision=HIGHEST)
            o = _dot(q_dec, st, precision=HIGHEST) + _dot(qk, v_new, precision=HIGHEST)
            state_ref[h] = st * jnp.exp(glast) + _dot_tn(k_dec, v_new, precision=HIGHEST)
            o = o * lax.rsqrt(jnp.mean(o * o, axis=-1, keepdims=True) + 1e-6) * ng_ref[...]
            z = qkvz_ref[0, r0:r0 + C, 3 * W + h * D_B:3 * W + (h + 1) * D_B]
            o_ref[0, r0:r0 + C, h * D_B:(h + 1) * D_B] = (o * _silu(z)).astype(o_ref.dtype)


def _gdn(qkvz, b, a, conv_w, a_log, dt_bias, norm_g, cb=256):
    B, S, _ = qkvz.shape
    W = H_B * D_B
    cb = min(cb, S)
    abc = jnp.concatenate([b, a], axis=-1)
    abr = jnp.transpose(abc, (0, 2, 1))
    zeros = jnp.zeros((H_B,), F32)
    prm = jnp.stack([jnp.concatenate([zeros, a_log]), jnp.concatenate([zeros, dt_bias])])
    idx = jnp.arange(cb)
    tri = ((idx[:, None] >= idx[None, :]) & (idx[:, None] // DN_CHUNK == idx[None, :] // DN_CHUNK)).astype(F32)
    kern = functools.partial(_gdn_kernel, cb=cb)
    blk = lambda bi, j: (bi, j, 0)
    fix = lambda bi, j: (0, 0)
    return pl.pallas_call(
        kern,
        grid=(B, S // cb),
        in_specs=[pl.BlockSpec((1, cb, 4 * W), blk), pl.BlockSpec((1, cb, 2 * H_B), blk),
                  pl.BlockSpec((1, 2 * H_B, cb), lambda bi, j: (bi, 0, j)),
                  pl.BlockSpec((CONV_K, 3 * W), fix), pl.BlockSpec((2 * H_B, 2), fix),
                  pl.BlockSpec((2, 2 * H_B), fix), pl.BlockSpec((1, D_B), fix),
                  pl.BlockSpec((cb, cb), fix)],
        out_specs=pl.BlockSpec((1, cb, W), blk),
        out_shape=jax.ShapeDtypeStruct((B, S, W), BF16),
        scratch_shapes=[pltpu.VMEM((cb + GDN_HALO, 3 * W), F32), pltpu.VMEM((H_B, D_B, D_B), F32),
                        pltpu.VMEM((cb, 3 * W), F32)],
        compiler_params=_cparams(("parallel", "arbitrary")),
        name="gated_deltanet",
    )(qkvz, abc, abr, conv_w, prm.T, prm, norm_g.reshape(1, D_B), tri)


def _post_mixer_kernel(a1_ref, a2_ref, h_ref, w1_ref, w2_ref, g_ref, b_ref, rw_ref, rb_ref, triu_ref,
                       h1_ref, h1b_ref, tope_ref, gate_ref, rank_ref, cnt_ref, carry_ref):
    i = pl.program_id(0)
    E, tm = rw_ref.shape[0], h_ref.shape[0]

    @pl.when(i == 0)
    def _():
        carry_ref[...] = jnp.zeros(carry_ref.shape, F32)

    mix = _dot(a1_ref[...].astype(BF16), w1_ref[...]) + _dot(a2_ref[...].astype(BF16), w2_ref[...])
    h1 = _layernorm_rows(ALPHA * h_ref[...] + mix, g_ref[...], b_ref[...])
    h1_ref[...] = h1
    h1b_ref[...] = h1.astype(BF16)

    logits = _dot_nt(rw_ref[...], h1, precision=HIGHEST) + rb_ref[...]
    erow = lax.broadcasted_iota(I32, (E, tm), 0)
    sel = jnp.zeros((E, tm), F32)
    onehots, tops = [], []
    for k in range(TOP_K):
        mx = jnp.max(logits, axis=0, keepdims=True)
        idx = jnp.min(jnp.where(logits == mx, erow, E), axis=0, keepdims=True)
        oh = erow == idx
        logits = jnp.where(oh, NEG_INF, logits)
        sel = sel + jnp.where(oh, 1.0, 0.0)
        onehots.append(oh)
        tops.append(mx)
        tope_ref[k:k + 1, :] = idx
    exps = [jnp.exp(t - tops[0]) for t in tops]
    den = exps[0] + exps[1] + exps[2] + exps[3]
    for k in range(TOP_K):
        gate_ref[k:k + 1, :] = exps[k] / den
    incl = _dot(sel.astype(BF16), triu_ref[...])
    excl = incl - sel + carry_ref[:, 0:1]
    for k in range(TOP_K):
        rank_ref[k:k + 1, :] = jnp.sum(jnp.where(onehots[k], excl, 0.0), axis=0, keepdims=True).astype(I32)
    carry_ref[...] = carry_ref[...] + jnp.sum(sel, axis=1, keepdims=True)
    cnt_ref[...] = carry_ref[...]


def _post_mixer(a1, a2, h2d, w_out, ln_g, ln_b, router_w, router_b, tm=512):
    T, D = h2d.shape
    E = router_w.shape[1]
    n1, n2 = a1.shape[1], a2.shape[1]
    w1 = w_out[:n1].astype(BF16)
    w2 = w_out[n1:].astype(BF16)
    idx = jnp.arange(tm)
    triu = (idx[:, None] <= idx[None, :]).astype(BF16)
    row = lambda i: (i, 0)
    col = lambda i: (0, i)
    fix = lambda i: (0, 0)
    return pl.pallas_call(
        _post_mixer_kernel,
        grid=(T // tm,),
        in_specs=[pl.BlockSpec((tm, n1), row), pl.BlockSpec((tm, n2), row), pl.BlockSpec((tm, D), row),
                  pl.BlockSpec((n1, D), fix), pl.BlockSpec((n2, D), fix),
                  pl.BlockSpec((1, D), fix), pl.BlockSpec((1, D), fix),
                  pl.BlockSpec((E, D), fix), pl.BlockSpec((E, 1), fix), pl.BlockSpec((tm, tm), fix)],
        out_specs=[pl.BlockSpec((tm, D), row), pl.BlockSpec((tm, D), row),
                   pl.BlockSpec((TOP_K, tm), col), pl.BlockSpec((TOP_K, tm), col),
                   pl.BlockSpec((TOP_K, tm), col), pl.BlockSpec((E, 128), fix)],
        out_shape=[jax.ShapeDtypeStruct((T, D), F32), jax.ShapeDtypeStruct((T, D), BF16),
                   jax.ShapeDtypeStruct((TOP_K, T), I32), jax.ShapeDtypeStruct((TOP_K, T), F32),
                   jax.ShapeDtypeStruct((TOP_K, T), I32), jax.ShapeDtypeStruct((E, 128), F32)],
        scratch_shapes=[pltpu.VMEM((E, 128), F32)],
        compiler_params=_cparams(("arbitrary",)),
        name="post_mixer_router",
    )(a1, a2, h2d, w1, w2, ln_g.reshape(1, D), ln_b.reshape(1, D), router_w.T, router_b.reshape(E, 1), triu)


MOE_BLOCK = 256


def _moe_kernel(be_ref, nu_ref, x_ref, wgu_ref, bgu_ref, wd_ref, bd_ref, y_ref):
    i = pl.program_id(0)

    @pl.when(i < nu_ref[0])
    def _():
        gu = _dot(x_ref[...], wgu_ref[0]) + bgu_ref[0]
        up = pltpu.roll(gu, gu.shape[1] - 1, axis=1)
        gt = jnp.minimum(gu, SWIGLU_LIMIT)
        up = jnp.clip(up, -SWIGLU_LIMIT, SWIGLU_LIMIT)
        hid = (up + 1.0) * (gt * _sigmoid(gt * SWIGLU_ALPHA))
        y_ref[...] = (_dot(hid.astype(BF16), wd_ref[0]) + bd_ref[0]).astype(y_ref.dtype)

    @pl.when(i >= nu_ref[0])
    def _():
        y_ref[...] = jnp.zeros(y_ref.shape, y_ref.dtype)


def _moe_experts(xs, blk_e, n_used, wgu, bgu, wdx, bd):
    P, D = xs.shape
    F2 = wgu.shape[2]
    bm = MOE_BLOCK
    wsel = lambda i, be, nu: (be[i], 0, 0)
    row = lambda i, be, nu: (i, 0)
    return pl.pallas_call(
        _moe_kernel,
        grid_spec=pltpu.PrefetchScalarGridSpec(
            num_scalar_prefetch=2,
            grid=(P // bm,),
            in_specs=[pl.BlockSpec((bm, D), row),
                      pl.BlockSpec((1, D, F2), wsel), pl.BlockSpec((1, 1, F2), wsel),
                      pl.BlockSpec((1, F2, D), wsel), pl.BlockSpec((1, 1, D), wsel)],
            out_specs=pl.BlockSpec((bm, D), row)),
        out_shape=jax.ShapeDtypeStruct((P, D), BF16),
        compiler_params=_cparams(("arbitrary",)),
        name="moe_experts",
    )(blk_e, n_used, xs, wgu, bgu, wdx, bd)


def _moe(h1b, top_e, rank, cnt, w_gu, b_gu, w_down, b_down):
    T, D = h1b.shape
    E = w_gu.shape[0]
    bm = MOE_BLOCK
    counts = cnt[:, 0].astype(I32)
    padded = (counts + bm - 1) // bm * bm
    pad_end = jnp.cumsum(padded)
    pad_start = pad_end - padded
    dest = pad_start[top_e] + rank
    nblk = (T * TOP_K) // bm + E
    P = nblk * bm
    tok = jnp.tile(jnp.arange(T, dtype=I32), TOP_K)
    slot_tok = jnp.full((P,), T, I32).at[dest.reshape(-1)].set(tok)
    blk_e = jnp.minimum(jnp.searchsorted(pad_end, jnp.arange(nblk, dtype=I32) * bm, side='right'),
                        E - 1).astype(I32)
    n_used = (pad_end[-1:] // bm).astype(I32)
    xs = jnp.concatenate([h1b, jnp.zeros((1, D), h1b.dtype)], 0)[slot_tok]
    wd = w_down.astype(BF16)
    wdx = jnp.stack([wd, jnp.zeros_like(wd)], axis=2).reshape(E, 2 * wd.shape[1], D)
    ys = _moe_experts(xs, blk_e, n_used, w_gu.astype(BF16), b_gu[:, None, :], wdx, b_down[:, None, :])
    return ys[dest]


def _tail_kernel(h1_ref, h1b_ref, yk_ref, gate_ref, p_ref, wg_ref, wp_ref, g_ref, b_ref, o_ref):
    ffn = yk_ref[0] * gate_ref[:, 0:1]
    for k in range(1, TOP_K):
        ffn = ffn + yk_ref[k] * gate_ref[:, k:k + 1]
    ple = _sigmoid(_dot(h1b_ref[...], wg_ref[...])) * _dot(p_ref[...].astype(BF16), wp_ref[...])
    o_ref[...] = _layernorm_rows(ALPHA * h1_ref[...] + ffn + ple, g_ref[...], b_ref[...])


def _layer_tail(h1, h1b, yk, gate_t, p2d, ple_w_gate, ple_w_proj, ln_g, ln_b, tm=256):
    T, D = h1.shape
    PD = p2d.shape[1]
    row = lambda i: (i, 0)
    fix = lambda i: (0, 0)
    return pl.pallas_call(
        _tail_kernel,
        grid=(T // tm,),
        in_specs=[pl.BlockSpec((tm, D), row), pl.BlockSpec((tm, D), row),
                  pl.BlockSpec((TOP_K, tm, D), lambda i: (0, i, 0)), pl.BlockSpec((tm, TOP_K), row),
                  pl.BlockSpec((tm, PD), row), pl.BlockSpec((D, D), fix), pl.BlockSpec((PD, D), fix),
                  pl.BlockSpec((1, D), fix), pl.BlockSpec((1, D), fix)],
        out_specs=pl.BlockSpec((tm, D), row),
        out_shape=jax.ShapeDtypeStruct((T, D), F32),
        compiler_params=_cparams(("parallel",)),
        name="layer_tail",
    )(h1, h1b, yk, gate_t.T, p2d, ple_w_gate.astype(BF16), ple_w_proj.astype(BF16),
      ln_g.reshape(1, D), ln_b.reshape(1, D))


def _cd_proj_kernel(x_ref, w_ref, q_ref, k_ref, v_ref, u_ref):
    x = x_ref[...].astype(BF16)
    n = H_C * DH_C
    q_ref[...] = (_dot(x, w_ref[:, 0:n]) * (DH_C ** -0.5 * LOG2E)).astype(q_ref.dtype)
    k_ref[...] = _dot(x, w_ref[:, n:2 * n]).astype(k_ref.dtype)
    v_ref[...] = _dot(x, w_ref[:, 2 * n:3 * n]).astype(v_ref.dtype)
    u_ref[...] = _dot(x, w_ref[:, 3 * n:])


def _cd_in_proj(h2d, w_in, tm=512):
    T, D = h2d.shape
    n = H_C * DH_C
    nu = w_in.shape[1] - 3 * n
    row = lambda i: (i, 0)
    return pl.pallas_call(
        _cd_proj_kernel,
        grid=(T // tm,),
        in_specs=[pl.BlockSpec((tm, D), row), pl.BlockSpec(w_in.shape, lambda i: (0, 0))],
        out_specs=[pl.BlockSpec((tm, n), row)] * 3 + [pl.BlockSpec((tm, nu), row)],
        out_shape=[jax.ShapeDtypeStruct((T, n), BF16)] * 3 + [jax.ShapeDtypeStruct((T, nu), F32)],
        compiler_params=_cparams(("parallel",)),
        name="cd_in_proj",
    )(h2d, w_in.astype(BF16))


def _dilated_bias_table(qb):
    import numpy as np
    max_w = max(w for w, _ in DILATED_PATTERNS)
    ndc = max_w // qb + 1
    r = np.arange(qb)[:, None]
    j = np.arange(qb)[None, :]
    tbl = np.empty((ndc, qb, qb), np.float32)
    for dc in range(ndc):
        delta = dc * qb + r - j
        mult = np.zeros((qb, qb), np.float64)
        for w, d in DILATED_PATTERNS:
            mult += (delta >= 0) & (delta <= w) & (delta % d == 0)
        with np.errstate(divide="ignore"):
            tbl[dc] = np.log2(mult)
    return jnp.asarray(tbl)


def _dilated_kernel(q_ref, k_ref, v_ref, bias_ref, o_ref, q2_ref, m_ref, l_ref, acc_ref, *, qb, ndc):
    i = pl.program_id(2)
    n_lt = qb // 128
    q = q_ref[0]
    lo = lax.broadcasted_iota(I32, q.shape, 1) < DH_C
    zero = jnp.zeros(q.shape, q.dtype)
    q2_ref[0:qb, :] = jnp.where(lo, q, zero)
    q2_ref[qb:, :] = jnp.where(lo, zero, q)
    m_ref[...] = jnp.full(m_ref.shape, -1e30, F32)
    l_ref[...] = jnp.zeros(l_ref.shape, F32)
    acc_ref[...] = jnp.zeros(acc_ref.shape, F32)

    def body(c, carry):
        r0 = pl.multiple_of(c * qb, qb)
        kk = k_ref[0, pl.ds(r0, qb), :]
        vv = v_ref[0, pl.ds(r0, qb), :]
        s = _dot_nt(q2_ref[...], kk)
        s = (s.reshape(2, qb, qb) + bias_ref[i - c][None]).reshape(2 * qb, qb)
        tiles = [s[:, j * 128:(j + 1) * 128] for j in range(n_lt)]
        m_cur = tiles[0]
        for t in tiles[1:]:
            m_cur = jnp.maximum(m_cur, t)
        m_old = m_ref[...]
        m_new = jnp.maximum(m_old, jnp.max(m_cur, axis=1, keepdims=True))
        ps = [jnp.exp2(t - m_new) for t in tiles]
        a = jnp.exp2(m_old - m_new)
        psum = ps[0]
        for t in ps[1:]:
            psum = psum + t
        l_ref[...] = a * l_ref[...] + psum
        p = jnp.concatenate([t.astype(BF16) for t in ps], axis=1)
        acc_ref[...] = a * acc_ref[...] + _dot(p, vv)
        m_ref[...] = m_new
        return carry

    lax.fori_loop(jnp.maximum(i - (ndc - 1), 0), i + 1, body, 0)
    o_all = acc_ref[...] / jnp.sum(l_ref[...], axis=1, keepdims=True)
    o_ref[0] = jnp.where(lo, o_all[0:qb, :], o_all[qb:, :]).astype(o_ref.dtype)


def _dilated_attention(q, k, v, qb=512):
    B, S, n = q.shape
    qb = min(qb, S)
    bias = _dilated_bias_table(qb)
    ndc = bias.shape[0]
    pw = 2 * DH_C
    kern = functools.partial(_dilated_kernel, qb=qb, ndc=ndc)
    blk = lambda b, pr, i: (b, i, pr)
    seq = lambda b, pr, i: (b, 0, pr)
    return pl.pallas_call(
        kern,
        grid=(B, n // pw, S // qb),
        in_specs=[pl.BlockSpec((1, qb, pw), blk), pl.BlockSpec((1, S, pw), seq),
                  pl.BlockSpec((1, S, pw), seq), pl.BlockSpec(bias.shape, lambda b, pr, i: (0, 0, 0))],
        out_specs=pl.BlockSpec((1, qb, pw), blk),
        out_shape=jax.ShapeDtypeStruct((B, S, n), BF16),
        scratch_shapes=[pltpu.VMEM((2 * qb, pw), BF16), pltpu.VMEM((2 * qb, pw), F32),
                        pltpu.VMEM((2 * qb, pw), F32), pltpu.VMEM((2 * qb, pw), F32)],
        compiler_params=_cparams(("parallel", "parallel", "arbitrary")),
        name="dilated_attention",
    )(q, k, v, bias)


POOL_HALO = 16


def _pool_kernel(u_ref, w_ref, sc_ref, o_ref, xbuf_ref, *, tm):
    j = pl.program_id(1)

    @pl.when(j == 0)
    def _():
        xbuf_ref[0:POOL_HALO, :] = jnp.zeros((POOL_HALO, xbuf_ref.shape[1]), F32)

    xbuf_ref[POOL_HALO:, :] = u_ref[0]
    x = xbuf_ref[POOL_HALO:, :]
    grp = lax.broadcasted_iota(I32, (1, x.shape[1]), 1) // POOL_GROUP
    run = x
    sel = jnp.zeros(x.shape, F32)
    win = jnp.zeros((1, x.shape[1]), F32)
    for d in range(1, max(POOL_WINDOWS)):
        run = run + xbuf_ref[pl.ds(POOL_HALO - d, tm), :]
        if d + 1 in POOL_WINDOWS:
            gi = POOL_WINDOWS.index(d + 1)
            sel = jnp.where(grp == gi, run, sel)
            win = jnp.where(grp == gi, float(d + 1), win)
    pos = j * tm + lax.broadcasted_iota(I32, (tm, 1), 0)
    mean = sel / jnp.minimum((pos + 1).astype(F32), win)
    o_ref[0] = _dot((mean - x).astype(BF16), w_ref[...]) * sc_ref[...]
    xbuf_ref[0:POOL_HALO, :] = xbuf_ref[tm:tm + POOL_HALO, :]


def _multiscale_pool(u, pool_w, pool_scale, tm=512):
    B, S, n = u.shape
    tm = min(tm, S)
    wbd = _block_diag([pool_w[g] for g in range(pool_w.shape[0])]).astype(BF16)
    blk = lambda b, j: (b, j, 0)
    fix = lambda b, j: (0, 0)
    return pl.pallas_call(
        functools.partial(_pool_kernel, tm=tm),
        grid=(B, S // tm),
        in_specs=[pl.BlockSpec((1, tm, n), blk), pl.BlockSpec((n, n), fix), pl.BlockSpec((1, n), fix)],
        out_specs=pl.BlockSpec((1, tm, n), blk),
        out_shape=jax.ShapeDtypeStruct((B, S, n), F32),
        scratch_shapes=[pltpu.VMEM((tm + POOL_HALO, n), F32)],
        compiler_params=_cparams(("parallel", "arbitrary")),
        name="multiscale_pool",
    )(u, wbd, pool_scale.reshape(1, n))


def kernel(x, p, ab_w_in, ab_q_norm_g, ab_kv_norm_g, ab_w_uq, ab_w_uk, ab_w_uv, ab_w_qidx,
           ab_kidx_norm_g, ab_kidx_norm_b, ab_conv_w, ab_a_log, ab_dt_bias, ab_out_norm_g, ab_w_out,
           cd_w_in, cd_pool_w, cd_pool_scale, cd_w_out, ln_mix_g, ln_mix_b, router_w, router_b,
           w_gu, b_gu, w_down, b_down, ple_w_proj, ple_w_gate, ln_ffn_g, ln_ffn_b):
    B, S, D = x.shape
    T = B * S
    h = x.reshape(T, D)
    for i in range(DEPTH):
        j = i // 2
        if i % 2 == 0:
            cq, ckv, kidx, small, qkvz = _ab_in_proj(h, ab_w_in[j], ab_q_norm_g[j], ab_kv_norm_g[j],
                                                     ab_kidx_norm_g[j], ab_kidx_norm_b[j])
            sh = lambda t: t.reshape(B, S, -1)
            o_w = D_IDX + H_IDX
            o_a = _dsa(sh(cq), sh(ckv), sh(kidx), sh(small[:, D_IDX:o_w]),
                       ab_w_uq[j], ab_w_uk[j], ab_w_uv[j], ab_w_qidx[j])
            o_b = _gdn(sh(qkvz), sh(small[:, o_w:o_w + H_B]), sh(small[:, o_w + H_B:o_w + 2 * H_B]),
                       ab_conv_w[j], ab_a_log[j], ab_dt_bias[j], ab_out_norm_g[j])
            a1, a2, w_out = o_a.reshape(T, -1), o_b.reshape(T, -1), ab_w_out[j]
        else:
            q, k, v, u = _cd_in_proj(h, cd_w_in[j])
            sh = lambda t: t.reshape(B, S, -1)
            o_c = _dilated_attention(sh(q), sh(k), sh(v))
            o_d = _multiscale_pool(sh(u), cd_pool_w[j], cd_pool_scale[j])
            a1, a2, w_out = o_c.reshape(T, -1), o_d.reshape(T, -1), cd_w_out[j]
        h1, h1b, top_e, gate_t, rank, cnt = _post_mixer(a1, a2, h, w_out, ln_mix_g[i], ln_mix_b[i],
                                                        router_w[i], router_b[i])
        yk = _moe(h1b, top_e, rank, cnt, w_gu[i], b_gu[i], w_down[i], b_down[i])
        h = _layer_tail(h1, h1b, yk, gate_t, p[i].reshape(T, -1), ple_w_gate[i], ple_w_proj[i],
                        ln_ffn_g[i], ln_ffn_b[i])
    return h.reshape(B, S, D)
```

```python
import functools
import math

import jax
import jax.numpy as jnp
from jax import lax
from jax.experimental import pallas as pl
from jax.experimental.pallas import tpu as pltpu

F32 = jnp.float32
BF16 = jnp.bfloat16
I32 = jnp.int32

DEPTH = 2
H_A, D_NOPE, D_VA, R_Q, R_KV, H_IDX, D_IDX = 8, 64, 64, 256, 128, 8, 64
TOPK_MAX = 256
H_B, D_B, CONV_K, DN_CHUNK = 4, 128, 4, 64
H_C, DH_C = 12, 64
DILATED_PATTERNS = ((128, 1), (512, 4), (2048, 16))
POOL_WINDOWS = (2, 4, 8, 16)
POOL_GROUP = 64
N_EXPERTS, TOP_K = 32, 4
SWIGLU_LIMIT, SWIGLU_ALPHA = 7.0, 1.702
ALPHA = (2 * DEPTH) ** 0.25

VMEM_LIMIT_BYTES = 56 * 1024 * 1024
HIGHEST = lax.Precision.HIGHEST
NEG_INF = float("-inf")
LOG2E = math.log2(math.e)
INT_MIN = -2 ** 31
KEY_NEG_INF = (0xFF800000 ^ 0x7FFFFFFF) - 2 ** 32


def _cparams(sem):
    return pltpu.CompilerParams(dimension_semantics=sem, vmem_limit_bytes=VMEM_LIMIT_BYTES)


def _dot(a, b, precision=None):
    return jnp.dot(a, b, preferred_element_type=F32, precision=precision)


def _dot_nt(a, b, precision=None):
    return lax.dot_general(a, b, (((1,), (1,)), ((), ())), preferred_element_type=F32,
                           precision=precision)


def _dot_tn(a, b, precision=None):
    return lax.dot_general(a, b, (((0,), (0,)), ((), ())), preferred_element_type=F32,
                           precision=precision)


def _sigmoid(x):
    return 1.0 / (1.0 + jnp.exp(-x))


def _silu(x):
    return x * _sigmoid(x)


def _layernorm_rows(x, g, b, eps=1e-5):
    mu = jnp.mean(x, axis=-1, keepdims=True)
    xc = x - mu
    var = jnp.mean(xc * xc, axis=-1, keepdims=True)
    return xc * lax.rsqrt(var + eps) * g + b


AB_SMALL = 128


def _ab_proj_kernel(x_ref, w_ref, gq_ref, gkv_ref, gk_ref, bk_ref,
                    cq_ref, ckv_ref, kidx_ref, small_ref, qkvz_ref):
    x = x_ref[...].astype(BF16)
    cq = _dot(x, w_ref[:, 0:R_Q])
    cq = cq * lax.rsqrt(jnp.mean(cq * cq, axis=-1, keepdims=True) + 1e-6) * gq_ref[...]
    cq_ref[...] = cq.astype(cq_ref.dtype)
    ckv = _dot(x, w_ref[:, R_Q:R_Q + R_KV])
    ckv = ckv * lax.rsqrt(jnp.mean(ckv * ckv, axis=-1, keepdims=True) + 1e-6) * gkv_ref[...]
    ckv_ref[...] = ckv.astype(ckv_ref.dtype)
    off = R_Q + R_KV
    sm = _dot(x, w_ref[:, off:off + AB_SMALL])
    small_ref[...] = sm
    lane = lax.broadcasted_iota(I32, sm.shape, 1)
    is_k = lane < D_IDX
    mu = jnp.sum(jnp.where(is_k, sm, 0.0), axis=-1, keepdims=True) * (1.0 / D_IDX)
    xc = jnp.where(is_k, sm - mu, 0.0)
    var = jnp.sum(xc * xc, axis=-1, keepdims=True) * (1.0 / D_IDX)
    kn = xc * lax.rsqrt(var + 1e-5) * gk_ref[...] + bk_ref[...]
    kidx_ref[...] = kn[:, :D_IDX].astype(kidx_ref.dtype)
    off += AB_SMALL
    qkvz_ref[...] = _dot(x, w_ref[:, off:])


def _ab_in_proj(h2d, w_in, q_norm_g, kv_norm_g, kidx_g, kidx_b, tm=512):
    T, D = h2d.shape
    W = H_B * D_B
    o = [0, R_Q, R_Q + R_KV, R_Q + R_KV + D_IDX, R_Q + R_KV + D_IDX + H_IDX]
    o_q = o[4]
    o_b = o_q + 4 * W
    pad = AB_SMALL - (D_IDX + H_IDX + 2 * H_B)
    w_perm = jnp.concatenate([
        w_in[:, o[0]:o[2]],
        w_in[:, o[2]:o[4]], w_in[:, o_b:o_b + 2 * H_B],
        jnp.zeros((D, pad), w_in.dtype),
        w_in[:, o_q:o_b],
    ], axis=1).astype(BF16)
    n_all = w_perm.shape[1]
    gk = jnp.zeros((1, AB_SMALL), F32).at[0, :D_IDX].set(kidx_g)
    bk = jnp.zeros((1, AB_SMALL), F32).at[0, :D_IDX].set(kidx_b)
    row = lambda i: (i, 0)
    fixed = lambda i: (0, 0)
    return pl.pallas_call(
        _ab_proj_kernel,
        grid=(T // tm,),
        in_specs=[pl.BlockSpec((tm, D), row), pl.BlockSpec((D, n_all), fixed),
                  pl.BlockSpec((1, R_Q), fixed), pl.BlockSpec((1, R_KV), fixed),
                  pl.BlockSpec((1, AB_SMALL), fixed), pl.BlockSpec((1, AB_SMALL), fixed)],
        out_specs=[pl.BlockSpec((tm, R_Q), row), pl.BlockSpec((tm, R_KV), row),
                   pl.BlockSpec((tm, D_IDX), row), pl.BlockSpec((tm, AB_SMALL), row),
                   pl.BlockSpec((tm, 4 * W), row)],
        out_shape=[jax.ShapeDtypeStruct((T, R_Q), BF16), jax.ShapeDtypeStruct((T, R_KV), BF16),
                   jax.ShapeDtypeStruct((T, D_IDX), BF16), jax.ShapeDtypeStruct((T, AB_SMALL), F32),
                   jax.ShapeDtypeStruct((T, 4 * W), F32)],
        compiler_params=_cparams(("parallel",)),
        name="ab_in_proj",
    )(h2d, w_perm, q_norm_g.reshape(1, -1), kv_norm_g.reshape(1, -1), gk, bk)


def _sort_key(x):
    bits = pltpu.bitcast(x + 0.0, I32)
    return jnp.where(bits < 0, bits ^ 0x7FFFFFFF, bits)


def _dsa_kernel(cq_ref, ckv_ref, kidx_ref, widx_ref, wqidx_ref, wuq_ref, wukbd_ref, wuvbd_ref,
                o_ref, key_ref, qlat_ref, m_ref, l_ref, acc_ref, *, qb, kc, topk, seq_bits):
    i = pl.program_id(1)
    nck = ((i + 1) * qb + kc - 1) // kc
    n_lt = kc // 128
    cq = cq_ref[0]
    widx = widx_ref[0] * (H_IDX ** -0.5 * D_IDX ** -0.5)
    q_pos = i * qb + lax.broadcasted_iota(I32, (qb, 1), 0)

    qidx = [_dot(cq, wqidx_ref[h]).astype(BF16) for h in range(H_IDX)]

    def idx_body(c, carry):
        kblk = kidx_ref[0, pl.ds(pl.multiple_of(c * kc, kc), kc), :]
        isc = jnp.zeros((qb, kc), F32)
        for h in range(H_IDX):
            sc = _dot_nt(qidx[h], kblk)
            isc = isc + jnp.maximum(sc, 0.0) * widx[:, h:h + 1]
        k_pos = c * kc + lax.broadcasted_iota(I32, (1, kc), 1)
        key_ref[c] = jnp.where(k_pos <= q_pos, _sort_key(isc), KEY_NEG_INF)
        return carry

    lax.fori_loop(0, nck, idx_body, 0)

    def count_where(pred):
        def body(c, acc):
            k = key_ref[c]
            for j in range(n_lt):
                acc = acc + jnp.where(pred(k[:, j * 128:(j + 1) * 128], c, j), 1, 0)
            return acc
        acc = lax.fori_loop(0, nck, body, jnp.zeros((qb, 128), I32))
        return jnp.sum(acc, axis=1, keepdims=True)

    def bit_body(b, t):
        cand = t + (jnp.int32(1) << (31 - b))
        cnt = count_where(lambda k, c, j: k >= cand)
        return jnp.where(cnt >= topk, cand, t)

    thr = lax.fori_loop(0, 32, bit_body, jnp.full((qb, 1), INT_MIN, I32))
    thr = jnp.maximum(thr, KEY_NEG_INF + 1)

    c_gt = count_where(lambda k, c, j: k > thr)
    c_ge = count_where(lambda k, c, j: k >= thr)
    need = topk - c_gt
    excess = jnp.max(c_ge - c_gt - need)

    @pl.when(excess > 0)
    def _():
        def idx_of(c, j):
            return c * kc + j * 128 + lax.broadcasted_iota(I32, (1, 128), 1)

        def pos_body(b, m):
            cand = m + (jnp.int32(1) << (seq_bits - 1 - b))
            cnt = count_where(lambda k, c, j: (k == thr) & (idx_of(c, j) < cand))
            return jnp.where(cnt < need, cand, m)

        m = lax.fori_loop(0, seq_bits, pos_body, jnp.zeros((qb, 1), I32))

        def demote(c, carry):
            k = key_ref[c]
            idx = c * kc + lax.broadcasted_iota(I32, (1, kc), 1)
            key_ref[c] = jnp.where((k == thr) & (idx > m), thr - 1, k)
            return carry

        lax.fori_loop(0, nck, demote, 0)

    q = _dot(cq, wuq_ref[...]).astype(BF16)
    qlat = _dot(q, wukbd_ref[...]) * (D_NOPE ** -0.5 * LOG2E)
    for h in range(H_A):
        qlat_ref[h * qb:(h + 1) * qb, :] = qlat[:, h * R_KV:(h + 1) * R_KV].astype(BF16)
    m_ref[...] = jnp.full(m_ref.shape, -1e30, F32)
    l_ref[...] = jnp.zeros(l_ref.shape, F32)
    acc_ref[...] = jnp.zeros(acc_ref.shape, F32)

    def att_body(c, carry):
        kv = ckv_ref[0, pl.ds(pl.multiple_of(c * kc, kc), kc), :]
        bias = jnp.where(key_ref[c] >= thr, 0.0, NEG_INF)
        s = _dot_nt(qlat_ref[...], kv)
        s = (s.reshape(H_A, qb, kc) + bias[None]).reshape(H_A * qb, kc)
        tiles = [s[:, j * 128:(j + 1) * 128] for j in range(n_lt)]
        m_cur = tiles[0]
        for t in tiles[1:]:
            m_cur = jnp.maximum(m_cur, t)
        m_old = m_ref[...]
        m_new = jnp.maximum(m_old, jnp.max(m_cur, axis=1, keepdims=True))
        ps = [jnp.exp2(t - m_new) for t in tiles]
        a = jnp.exp2(m_old - m_new)
        psum = ps[0]
        for t in ps[1:]:
            psum = psum + t
        l_ref[...] = a * l_ref[...] + psum
        p = jnp.concatenate([t.astype(BF16) for t in ps], axis=1)
        acc_ref[...] = a * acc_ref[...] + _dot(p, kv)
        m_ref[...] = m_new
        return carry

    lax.fori_loop(0, nck, att_body, 0)
    o_all = acc_ref[...] / jnp.sum(l_ref[...], axis=1, keepdims=True)
    o_lat = jnp.concatenate([o_all[h * qb:(h + 1) * qb, :] for h in range(H_A)], axis=1)
    o_ref[0] = _dot(o_lat.astype(BF16), wuvbd_ref[...]).astype(o_ref.dtype)


def _block_diag(blocks):
    n = len(blocks)
    r, c = blocks[0].shape
    out = jnp.zeros((n * r, n * c), blocks[0].dtype)
    for k, blk in enumerate(blocks):
        out = out.at[k * r:(k + 1) * r, k * c:(k + 1) * c].set(blk)
    return out


def _dsa(cq, ckv, kidx, widx, w_uq, w_uk, w_uv, w_qidx, qb=128, kc=512):
    B, S, _ = cq.shape
    kc = min(kc, S)
    topk = min(TOPK_MAX, S // 4)
    assert S % kc == 0 and kc % qb == 0 and kc >= topk
    seq_bits = max(1, (S - 1).bit_length())
    wqidx = jnp.transpose(w_qidx, (1, 0, 2)).astype(BF16)
    wuq = w_uq.reshape(R_Q, H_A * D_NOPE).astype(BF16)
    wukbd = _block_diag([w_uk[:, h, :].T for h in range(H_A)]).astype(BF16)
    wuvbd = _block_diag([w_uv[:, h, :] for h in range(H_A)]).astype(BF16)
    kern = functools.partial(_dsa_kernel, qb=qb, kc=kc, topk=topk, seq_bits=seq_bits)
    blk_q = lambda b, i: (b, i, 0)
    seq = lambda b, i: (b, 0, 0)
    fix2 = lambda b, i: (0, 0)
    fix3 = lambda b, i: (0, 0, 0)
    return pl.pallas_call(
        kern,
        grid=(B, S // qb),
        in_specs=[pl.BlockSpec((1, qb, R_Q), blk_q), pl.BlockSpec((1, S, R_KV), seq),
                  pl.BlockSpec((1, S, D_IDX), seq), pl.BlockSpec((1, qb, H_IDX), blk_q),
                  pl.BlockSpec(wqidx.shape, fix3), pl.BlockSpec(wuq.shape, fix2),
                  pl.BlockSpec(wukbd.shape, fix2), pl.BlockSpec(wuvbd.shape, fix2)],
        out_specs=pl.BlockSpec((1, qb, H_A * D_VA), blk_q),
        out_shape=jax.ShapeDtypeStruct((B, S, H_A * D_VA), BF16),
        scratch_shapes=[pltpu.VMEM((S // kc, qb, kc), I32),
                        pltpu.VMEM((H_A * qb, R_KV), BF16),
                        pltpu.VMEM((H_A * qb, R_KV), F32), pltpu.VMEM((H_A * qb, R_KV), F32),
                        pltpu.VMEM((H_A * qb, R_KV), F32)],
        compiler_params=_cparams(("parallel", "arbitrary")),
        name="dsa_attention",
    )(cq, ckv, kidx, widx, wqidx, wuq, wukbd, wuvbd)


GDN_HALO = 8
INV_PRECISION = HIGHEST


def _softplus(x):
    return jnp.maximum(x, 0.0) + jnp.log1p(jnp.exp(-jnp.abs(x)))


def _gdn_kernel(qkvz_ref, abc_ref, abr_ref, convw_ref, prm_c_ref, prm_r_ref, ng_ref, tri_ref,
                o_ref, xbuf_ref, state_ref, conv_ref, *, cb):
    C = DN_CHUNK
    W = H_B * D_B
    j = pl.program_id(1)

    @pl.when(j == 0)
    def _():
        xbuf_ref[0:GDN_HALO, :] = jnp.zeros((GDN_HALO, 3 * W), F32)
        state_ref[...] = jnp.zeros(state_ref.shape, F32)

    xbuf_ref[GDN_HALO:, :] = qkvz_ref[0, :, 0:3 * W]
    acc = xbuf_ref[GDN_HALO:, :] * convw_ref[CONV_K - 1:CONV_K, :]
    for t in range(CONV_K - 1):
        sh = CONV_K - 1 - t
        acc = acc + xbuf_ref[pl.ds(GDN_HALO - sh, cb), :] * convw_ref[t:t + 1, :]
    conv_ref[...] = _silu(acc)
    xbuf_ref[0:GDN_HALO, :] = xbuf_ref[cb:cb + GDN_HALO, :]

    abc = abc_ref[0]
    abr = abr_ref[0]
    beta_c = _sigmoid(abc)
    g_c = -jnp.exp(prm_r_ref[0:1, :]) * _softplus(abc + prm_r_ref[1:2, :])
    g_r = -jnp.exp(prm_c_ref[:, 0:1]) * _softplus(abr + prm_c_ref[:, 1:2])
    gc_c = _dot(tri_ref[...], g_c, precision=HIGHEST)
    gc_r = _dot_nt(g_r, tri_ref[...], precision=HIGHEST)

    ri = lax.broadcasted_iota(I32, (cb, cb), 0)
    ci = lax.broadcasted_iota(I32, (cb, cb), 1)
    same = (ri // C) == (ci // C)
    lower = same & (ri >= ci)
    strict = same & (ri > ci)
    eye = jnp.where(ri == ci, 1.0, 0.0)
    bf = lambda t: t.astype(BF16)

    for h in range(H_B):
        q = conv_ref[:, h * D_B:(h + 1) * D_B]
        k = conv_ref[:, W + h * D_B:W + (h + 1) * D_B]
        v = conv_ref[:, 2 * W + h * D_B:2 * W + (h + 1) * D_B]
        q = q * lax.rsqrt(jnp.sum(q * q, axis=-1, keepdims=True) + 1e-6) * (D_B ** -0.5)
        k = k * lax.rsqrt(jnp.sum(k * k, axis=-1, keepdims=True) + 1e-6)
        beta = beta_c[:, h:h + 1]
        gcol = gc_c[:, H_B + h:H_B + h + 1]
        grow = gc_r[H_B + h:H_B + h + 1, :]
        decay = jnp.exp(jnp.where(lower, gcol - grow, NEG_INF))
        kb = k * beta
        vb = v * beta
        kq = bf(k)
        a_mat = jnp.where(strict, _dot_nt(bf(kb), kq) * decay, 0.0)
        xm = -a_mat
        t_mat = eye + xm
        for _ in range(int(math.log2(C)) - 1):
            xm = _dot(xm, xm, precision=INV_PRECISION)
            t_mat = t_mat + _dot(t_mat, xm, precision=INV_PRECISION)
        egc = jnp.exp(gcol)
        tq = bf(t_mat)
        u = _dot(tq, bf(vb))
        w = bf(_dot(tq, bf(kb * egc)))
        qk = bf(jnp.where(lower, _dot_nt(bf(q), kq) * decay, 0.0))
        q_dec = bf(q * egc)
        for n in range(cb // C):
            r0 = n * C
            glast = gcol[r0 + C - 1:r0 + C, :]
            k_dec = bf(k[r0:r0 + C] * jnp.exp(glast - gcol[r0:r0 + C]))
            st = state_ref[h]
            stq = bf(st)
            v_new = u[r0:r0 + C] - _dot(w[r0:r0 + C], stq)
            vq = bf(v_new)
            o = _dot(q_dec[r0:r0 + C], stq) + _dot(qk[r0:r0 + C, r0:r0 + C], vq)
            state_ref[h] = st * jnp.exp(glast) + _dot_tn(k_dec, vq)
            o = o * lax.rsqrt(jnp.mean(o * o, axis=-1, keepdims=True) + 1e-6) * ng_ref[...]
            z = qkvz_ref[0, r0:r0 + C, 3 * W + h * D_B:3 * W + (h + 1) * D_B]
            o_ref[0, r0:r0 + C, h * D_B:(h + 1) * D_B] = (o * _silu(z)).astype(o_ref.dtype)


def _gdn(qkvz, b, a, conv_w, a_log, dt_bias, norm_g, cb=256):
    B, S, _ = qkvz.shape
    W = H_B * D_B
    cb = min(cb, S)
    abc = jnp.concatenate([b, a], axis=-1)
    abr = jnp.transpose(abc, (0, 2, 1))
    zeros = jnp.zeros((H_B,), F32)
    prm = jnp.stack([jnp.concatenate([zeros, a_log]), jnp.concatenate([zeros, dt_bias])])
    idx = jnp.arange(cb)
    tri = ((idx[:, None] >= idx[None, :]) & (idx[:, None] // DN_CHUNK == idx[None, :] // DN_CHUNK)).astype(F32)
    kern = functools.partial(_gdn_kernel, cb=cb)
    blk = lambda bi, j: (bi, j, 0)
    fix = lambda bi, j: (0, 0)
    return pl.pallas_call(
        kern,
        grid=(B, S // cb),
        in_specs=[pl.BlockSpec((1, cb, 4 * W), blk), pl.BlockSpec((1, cb, 2 * H_B), blk),
                  pl.BlockSpec((1, 2 * H_B, cb), lambda bi, j: (bi, 0, j)),
                  pl.BlockSpec((CONV_K, 3 * W), fix), pl.BlockSpec((2 * H_B, 2), fix),
                  pl.BlockSpec((2, 2 * H_B), fix), pl.BlockSpec((1, D_B), fix),
                  pl.BlockSpec((cb, cb), fix)],
        out_specs=pl.BlockSpec((1, cb, W), blk),
        out_shape=jax.ShapeDtypeStruct((B, S, W), BF16),
        scratch_shapes=[pltpu.VMEM((cb + GDN_HALO, 3 * W), F32), pltpu.VMEM((H_B, D_B, D_B), F32),
                        pltpu.VMEM((cb, 3 * W), F32)],
        compiler_params=_cparams(("parallel", "arbitrary")),
        name="gated_deltanet",
    )(qkvz, abc, abr, conv_w, prm.T, prm, norm_g.reshape(1, D_B), tri)


def _post_mixer_kernel(a1_ref, a2_ref, h_ref, w1_ref, w2_ref, g_ref, b_ref, rw_ref, rb_ref, triu_ref,
                       h1_ref, tope_ref, gate_ref, rank_ref, cnt_ref, carry_ref):
    i = pl.program_id(0)
    E, tm = rw_ref.shape[0], h_ref.shape[0]

    @pl.when(i == 0)
    def _():
        carry_ref[...] = jnp.zeros(carry_ref.shape, F32)

    mix = _dot(a1_ref[...].astype(BF16), w1_ref[...]) + _dot(a2_ref[...].astype(BF16), w2_ref[...])
    h1 = _layernorm_rows(ALPHA * h_ref[...] + mix, g_ref[...], b_ref[...])
    h1_ref[...] = h1

    logits = _dot_nt(rw_ref[...], h1, precision=HIGHEST) + rb_ref[...]
    erow = lax.broadcasted_iota(I32, (E, tm), 0)
    sel = jnp.zeros((E, tm), F32)
    onehots, tops = [], []
    for k in range(TOP_K):
        mx = jnp.max(logits, axis=0, keepdims=True)
        idx = jnp.min(jnp.where(logits == mx, erow, E), axis=0, keepdims=True)
        oh = erow == idx
        logits = jnp.where(oh, NEG_INF, logits)
        sel = sel + jnp.where(oh, 1.0, 0.0)
        onehots.append(oh)
        tops.append(mx)
        tope_ref[k:k + 1, :] = idx
    exps = [jnp.exp(t - tops[0]) for t in tops]
    den = exps[0] + exps[1] + exps[2] + exps[3]
    for k in range(TOP_K):
        gate_ref[k:k + 1, :] = exps[k] / den
    incl = _dot(sel.astype(BF16), triu_ref[...])
    excl = incl - sel + carry_ref[:, 0:1]
    for k in range(TOP_K):
        rank_ref[k:k + 1, :] = jnp.sum(jnp.where(onehots[k], excl, 0.0), axis=0, keepdims=True).astype(I32)
    carry_ref[...] = carry_ref[...] + jnp.sum(sel, axis=1, keepdims=True)
    cnt_ref[...] = carry_ref[...]


def _post_mixer(a1, a2, h2d, w_out, ln_g, ln_b, router_w, router_b, tm=512):
    T, D = h2d.shape
    E = router_w.shape[1]
    n1, n2 = a1.shape[1], a2.shape[1]
    w1 = w_out[:n1].astype(BF16)
    w2 = w_out[n1:].astype(BF16)
    idx = jnp.arange(tm)
    triu = (idx[:, None] <= idx[None, :]).astype(BF16)
    row = lambda i: (i, 0)
    col = lambda i: (0, i)
    fix = lambda i: (0, 0)
    return pl.pallas_call(
        _post_mixer_kernel,
        grid=(T // tm,),
        in_specs=[pl.BlockSpec((tm, n1), row), pl.BlockSpec((tm, n2), row), pl.BlockSpec((tm, D), row),
                  pl.BlockSpec((n1, D), fix), pl.BlockSpec((n2, D), fix),
                  pl.BlockSpec((1, D), fix), pl.BlockSpec((1, D), fix),
                  pl.BlockSpec((E, D), fix), pl.BlockSpec((E, 1), fix), pl.BlockSpec((tm, tm), fix)],
        out_specs=[pl.BlockSpec((tm, D), row),
                   pl.BlockSpec((TOP_K, tm), col), pl.BlockSpec((TOP_K, tm), col),
                   pl.BlockSpec((TOP_K, tm), col), pl.BlockSpec((E, 128), fix)],
        out_shape=[jax.ShapeDtypeStruct((T, D), F32),
                   jax.ShapeDtypeStruct((TOP_K, T), I32), jax.ShapeDtypeStruct((TOP_K, T), F32),
                   jax.ShapeDtypeStruct((TOP_K, T), I32), jax.ShapeDtypeStruct((E, 128), F32)],
        scratch_shapes=[pltpu.VMEM((E, 128), F32)],
        compiler_params=_cparams(("arbitrary",)),
        name="post_mixer_router",
    )(a1, a2, h2d, w1, w2, ln_g.reshape(1, D), ln_b.reshape(1, D), router_w.T, router_b.reshape(E, 1), triu)


def _dispatch_kernel(dest_ref, h_hbm, xs_init_hbm, xs_hbm, sem, *, tt, n_tok):
    del xs_init_hbm
    base = pl.program_id(0) * tt

    def row_copy(r, k):
        t = base + r
        return pltpu.make_async_copy(h_hbm.at[pl.ds(t, 1)], xs_hbm.at[pl.ds(dest_ref[k * n_tok + t], 1)], sem)

    def start(r, c):
        for k in range(TOP_K):
            row_copy(r, k).start()
        return c

    def wait(r, c):
        for k in range(TOP_K):
            row_copy(r, k).wait()
        return c

    lax.fori_loop(0, tt, start, 0)
    lax.fori_loop(0, tt, wait, 0)


def _moe_dispatch(h1, dest_flat, n_slots, tt=512):
    T, D = h1.shape
    any_spec = pl.BlockSpec(memory_space=pl.ANY)
    return pl.pallas_call(
        functools.partial(_dispatch_kernel, tt=tt, n_tok=T),
        grid_spec=pltpu.PrefetchScalarGridSpec(
            num_scalar_prefetch=1, grid=(T // tt,),
            in_specs=[any_spec, any_spec], out_specs=any_spec,
            scratch_shapes=[pltpu.SemaphoreType.DMA(())]),
        out_shape=jax.ShapeDtypeStruct((n_slots, D), h1.dtype),
        input_output_aliases={2: 0},
        compiler_params=_cparams(("arbitrary",)),
        name="moe_dispatch",
    )(dest_flat, h1, jnp.zeros((n_slots, D), h1.dtype))


MOE_BLOCK = 256


def _moe_kernel(be_ref, nu_ref, x_ref, wgu_ref, bgu_ref, wd_ref, bd_ref, y_ref):
    i = pl.program_id(0)

    @pl.when(i < nu_ref[0])
    def _():
        gu = _dot(x_ref[...].astype(BF16), wgu_ref[0]) + bgu_ref[0]
        up = pltpu.roll(gu, gu.shape[1] - 1, axis=1)
        gt = jnp.minimum(gu, SWIGLU_LIMIT)
        up = jnp.clip(up, -SWIGLU_LIMIT, SWIGLU_LIMIT)
        hid = (up + 1.0) * (gt * _sigmoid(gt * SWIGLU_ALPHA))
        y_ref[...] = (_dot(hid.astype(BF16), wd_ref[0]) + bd_ref[0]).astype(y_ref.dtype)

    @pl.when(i >= nu_ref[0])
    def _():
        y_ref[...] = jnp.zeros(y_ref.shape, y_ref.dtype)


def _moe_experts(xs, blk_e, n_used, wgu, bgu, wdx, bd):
    P, D = xs.shape
    F2 = wgu.shape[2]
    bm = MOE_BLOCK
    wsel = lambda i, be, nu: (be[i], 0, 0)
    row = lambda i, be, nu: (i, 0)
    return pl.pallas_call(
        _moe_kernel,
        grid_spec=pltpu.PrefetchScalarGridSpec(
            num_scalar_prefetch=2,
            grid=(P // bm,),
            in_specs=[pl.BlockSpec((bm, D), row),
                      pl.BlockSpec((1, D, F2), wsel), pl.BlockSpec((1, 1, F2), wsel),
                      pl.BlockSpec((1, F2, D), wsel), pl.BlockSpec((1, 1, D), wsel)],
            out_specs=pl.BlockSpec((bm, D), row)),
        out_shape=jax.ShapeDtypeStruct((P, D), F32),
        compiler_params=_cparams(("arbitrary",)),
        name="moe_experts",
    )(blk_e, n_used, xs, wgu, bgu, wdx, bd)


def _moe(h1, top_e, rank, cnt, w_gu, b_gu, w_down, b_down):
    T, D = h1.shape
    E = w_gu.shape[0]
    bm = MOE_BLOCK
    counts = cnt[:, 0].astype(I32)
    padded = (counts + bm - 1) // bm * bm
    pad_end = jnp.cumsum(padded)
    pad_start = pad_end - padded
    dest = (pad_start[top_e] + rank).reshape(-1)
    nblk = (T * TOP_K) // bm + E
    blk_e = jnp.minimum(jnp.searchsorted(pad_end, jnp.arange(nblk, dtype=I32) * bm, side='right'),
                        E - 1).astype(I32)
    n_used = (pad_end[-1:] // bm).astype(I32)
    xs = _moe_dispatch(h1, dest, nblk * bm)
    wd = w_down.astype(BF16)
    wdx = jnp.stack([wd, jnp.zeros_like(wd)], axis=2).reshape(E, 2 * wd.shape[1], D)
    ys = _moe_experts(xs, blk_e, n_used, w_gu.astype(BF16), b_gu[:, None, :], wdx, b_down[:, None, :])
    return ys, dest


def _tail_kernel(dest_ref, h1_ref, gate_ref, p_ref, wg_ref, wp_ref, g_ref, b_ref, ys_hbm, o_ref,
                 ybuf, sem, *, tm, n_tok):
    i = pl.program_id(0)

    def row_copy(tile, slot, r, k):
        d = dest_ref[k * n_tok + tile * tm + r]
        return pltpu.make_async_copy(ys_hbm.at[pl.ds(d, 1)], ybuf.at[slot, k, pl.ds(r, 1)], sem.at[slot])

    def start_tile(tile, slot):
        def body(r, c):
            for k in range(TOP_K):
                row_copy(tile, slot, r, k).start()
            return c
        lax.fori_loop(0, tm, body, 0)

    def wait_tile(tile, slot):
        def body(r, c):
            for k in range(TOP_K):
                row_copy(tile, slot, r, k).wait()
            return c
        lax.fori_loop(0, tm, body, 0)

    @pl.when(i == 0)
    def _():
        start_tile(0, 0)

    @pl.when(i + 1 < pl.num_programs(0))
    def _():
        start_tile(i + 1, (i + 1) % 2)

    slot = i % 2
    h1 = h1_ref[...]
    ple = _sigmoid(_dot(h1.astype(BF16), wg_ref[...])) * _dot(p_ref[...].astype(BF16), wp_ref[...])
    wait_tile(i, slot)
    ffn = ybuf[slot, 0] * gate_ref[:, 0:1]
    for k in range(1, TOP_K):
        ffn = ffn + ybuf[slot, k] * gate_ref[:, k:k + 1]
    o_ref[...] = _layernorm_rows(ALPHA * h1 + ffn + ple, g_ref[...], b_ref[...])


def _layer_tail(h1, ys, dest_flat, gate_t, p2d, ple_w_gate, ple_w_proj, ln_g, ln_b, tm=256):
    T, D = h1.shape
    PD = p2d.shape[1]
    row = lambda i, d: (i, 0)
    fix = lambda i, d: (0, 0)
    return pl.pallas_call(
        functools.partial(_tail_kernel, tm=tm, n_tok=T),
        grid_spec=pltpu.PrefetchScalarGridSpec(
            num_scalar_prefetch=1, grid=(T // tm,),
            in_specs=[pl.BlockSpec((tm, D), row), pl.BlockSpec((tm, TOP_K), row),
                      pl.BlockSpec((tm, PD), row), pl.BlockSpec((D, D), fix), pl.BlockSpec((PD, D), fix),
                      pl.BlockSpec((1, D), fix), pl.BlockSpec((1, D), fix),
                      pl.BlockSpec(memory_space=pl.ANY)],
            out_specs=pl.BlockSpec((tm, D), row),
            scratch_shapes=[pltpu.VMEM((2, TOP_K, tm, D), F32), pltpu.SemaphoreType.DMA((2,))]),
        out_shape=jax.ShapeDtypeStruct((T, D), F32),
        compiler_params=_cparams(("arbitrary",)),
        name="layer_tail",
    )(dest_flat, h1, gate_t.T, p2d, ple_w_gate.astype(BF16), ple_w_proj.astype(BF16),
      ln_g.reshape(1, D), ln_b.reshape(1, D), ys)


def _cd_proj_kernel(x_ref, w_ref, q_ref, k_ref, v_ref, u_ref):
    x = x_ref[...].astype(BF16)
    n = H_C * DH_C
    q_ref[...] = (_dot(x, w_ref[:, 0:n]) * (DH_C ** -0.5 * LOG2E)).astype(q_ref.dtype)
    k_ref[...] = _dot(x, w_ref[:, n:2 * n]).astype(k_ref.dtype)
    v_ref[...] = _dot(x, w_ref[:, 2 * n:3 * n]).astype(v_ref.dtype)
    u_ref[...] = _dot(x, w_ref[:, 3 * n:])


def _cd_in_proj(h2d, w_in, tm=512):
    T, D = h2d.shape
    n = H_C * DH_C
    nu = w_in.shape[1] - 3 * n
    row = lambda i: (i, 0)
    return pl.pallas_call(
        _cd_proj_kernel,
        grid=(T // tm,),
        in_specs=[pl.BlockSpec((tm, D), row), pl.BlockSpec(w_in.shape, lambda i: (0, 0))],
        out_specs=[pl.BlockSpec((tm, n), row)] * 3 + [pl.BlockSpec((tm, nu), row)],
        out_shape=[jax.ShapeDtypeStruct((T, n), BF16)] * 3 + [jax.ShapeDtypeStruct((T, nu), F32)],
        compiler_params=_cparams(("parallel",)),
        name="cd_in_proj",
    )(h2d, w_in.astype(BF16))


def _dilated_bias_table(qb):
    import numpy as np
    max_w = max(w for w, _ in DILATED_PATTERNS)
    ndc = max_w // qb + 1
    r = np.arange(qb)[:, None]
    j = np.arange(qb)[None, :]
    tbl = np.empty((ndc, qb, qb), np.float32)
    for dc in range(ndc):
        delta = dc * qb + r - j
        mult = np.zeros((qb, qb), np.float64)
        for w, d in DILATED_PATTERNS:
            mult += (delta >= 0) & (delta <= w) & (delta % d == 0)
        with np.errstate(divide="ignore"):
            tbl[dc] = np.log2(mult)
    return jnp.asarray(tbl)


def _dilated_kernel(q_ref, k_ref, v_ref, bias_ref, o_ref, q2_ref, m_ref, l_ref, acc_ref, *, qb, ndc):
    i = pl.program_id(2)
    n_lt = qb // 128
    q = q_ref[0]
    lo = lax.broadcasted_iota(I32, q.shape, 1) < DH_C
    zero = jnp.zeros(q.shape, q.dtype)
    q2_ref[0:qb, :] = jnp.where(lo, q, zero)
    q2_ref[qb:, :] = jnp.where(lo, zero, q)
    m_ref[...] = jnp.full(m_ref.shape, -1e30, F32)
    l_ref[...] = jnp.zeros(l_ref.shape, F32)
    acc_ref[...] = jnp.zeros(acc_ref.shape, F32)

    def body(c, carry):
        r0 = pl.multiple_of(c * qb, qb)
        kk = k_ref[0, pl.ds(r0, qb), :]
        vv = v_ref[0, pl.ds(r0, qb), :]
        s = _dot_nt(q2_ref[...], kk)
        s = (s.reshape(2, qb, qb) + bias_ref[i - c][None]).reshape(2 * qb, qb)
        tiles = [s[:, j * 128:(j + 1) * 128] for j in range(n_lt)]
        m_cur = tiles[0]
        for t in tiles[1:]:
            m_cur = jnp.maximum(m_cur, t)
        m_old = m_ref[...]
        m_new = jnp.maximum(m_old, jnp.max(m_cur, axis=1, keepdims=True))
        ps = [jnp.exp2(t - m_new) for t in tiles]
        a = jnp.exp2(m_old - m_new)
        psum = ps[0]
        for t in ps[1:]:
            psum = psum + t
        l_ref[...] = a * l_ref[...] + psum
        p = jnp.concatenate([t.astype(BF16) for t in ps], axis=1)
        acc_ref[...] = a * acc_ref[...] + _dot(p, vv)
        m_ref[...] = m_new
        return carry

    lax.fori_loop(jnp.maximum(i - (ndc - 1), 0), i + 1, body, 0)
    o_all = acc_ref[...] / jnp.sum(l_ref[...], axis=1, keepdims=True)
    o_ref[0] = jnp.where(lo, o_all[0:qb, :], o_all[qb:, :]).astype(o_ref.dtype)


def _dilated_attention(q, k, v, qb=512):
    B, S, n = q.shape
    qb = min(qb, S)
    bias = _dilated_bias_table(qb)
    ndc = bias.shape[0]
    pw = 2 * DH_C
    kern = functools.partial(_dilated_kernel, qb=qb, ndc=ndc)
    blk = lambda b, pr, i: (b, i, pr)
    seq = lambda b, pr, i: (b, 0, pr)
    return pl.pallas_call(
        kern,
        grid=(B, n // pw, S // qb),
        in_specs=[pl.BlockSpec((1, qb, pw), blk), pl.BlockSpec((1, S, pw), seq),
                  pl.BlockSpec((1, S, pw), seq), pl.BlockSpec(bias.shape, lambda b, pr, i: (0, 0, 0))],
        out_specs=pl.BlockSpec((1, qb, pw), blk),
        out_shape=jax.ShapeDtypeStruct((B, S, n), BF16),
        scratch_shapes=[pltpu.VMEM((2 * qb, pw), BF16), pltpu.VMEM((2 * qb, pw), F32),
                        pltpu.VMEM((2 * qb, pw), F32), pltpu.VMEM((2 * qb, pw), F32)],
        compiler_params=_cparams(("parallel", "parallel", "arbitrary")),
        name="dilated_attention",
    )(q, k, v, bias)


POOL_HALO = 16


def _pool_kernel(u_ref, w_ref, sc_ref, o_ref, xbuf_ref, *, tm):
    j = pl.program_id(1)

    @pl.when(j == 0)
    def _():
        xbuf_ref[0:POOL_HALO, :] = jnp.zeros((POOL_HALO, xbuf_ref.shape[1]), F32)

    xbuf_ref[POOL_HALO:, :] = u_ref[0]
    x = xbuf_ref[POOL_HALO:, :]
    grp = lax.broadcasted_iota(I32, (1, x.shape[1]), 1) // POOL_GROUP
    run = x
    sel = jnp.zeros(x.shape, F32)
    win = jnp.zeros((1, x.shape[1]), F32)
    for d in range(1, max(POOL_WINDOWS)):
        run = run + xbuf_ref[pl.ds(POOL_HALO - d, tm), :]
        if d + 1 in POOL_WINDOWS:
            gi = POOL_WINDOWS.index(d + 1)
            sel = jnp.where(grp == gi, run, sel)
            win = jnp.where(grp == gi, float(d + 1), win)
    pos = j * tm + lax.broadcasted_iota(I32, (tm, 1), 0)
    mean = sel / jnp.minimum((pos + 1).astype(F32), win)
    o_ref[0] = _dot((mean - x).astype(BF16), w_ref[...]) * sc_ref[...]
    xbuf_ref[0:POOL_HALO, :] = xbuf_ref[tm:tm + POOL_HALO, :]


def _multiscale_pool(u, pool_w, pool_scale, tm=512):
    B, S, n = u.shape
    tm = min(tm, S)
    wbd = _block_diag([pool_w[g] for g in range(pool_w.shape[0])]).astype(BF16)
    blk = lambda b, j: (b, j, 0)
    fix = lambda b, j: (0, 0)
    return pl.pallas_call(
        functools.partial(_pool_kernel, tm=tm),
        grid=(B, S // tm),
        in_specs=[pl.BlockSpec((1, tm, n), blk), pl.BlockSpec((n, n), fix), pl.BlockSpec((1, n), fix)],
        out_specs=pl.BlockSpec((1, tm, n), blk),
        out_shape=jax.ShapeDtypeStruct((B, S, n), F32),
        scratch_shapes=[pltpu.VMEM((tm + POOL_HALO, n), F32)],
        compiler_params=_cparams(("parallel", "arbitrary")),
        name="multiscale_pool",
    )(u, wbd, pool_scale.reshape(1, n))


def kernel(x, p, ab_w_in, ab_q_norm_g, ab_kv_norm_g, ab_w_uq, ab_w_uk, ab_w_uv, ab_w_qidx,
           ab_kidx_norm_g, ab_kidx_norm_b, ab_conv_w, ab_a_log, ab_dt_bias, ab_out_norm_g, ab_w_out,
           cd_w_in, cd_pool_w, cd_pool_scale, cd_w_out, ln_mix_g, ln_mix_b, router_w, router_b,
           w_gu, b_gu, w_down, b_down, ple_w_proj, ple_w_gate, ln_ffn_g, ln_ffn_b):
    B, S, D = x.shape
    T = B * S
    h = x.reshape(T, D)
    for i in range(DEPTH):
        j = i // 2
        if i % 2 == 0:
            cq, ckv, kidx, small, qkvz = _ab_in_proj(h, ab_w_in[j], ab_q_norm_g[j], ab_kv_norm_g[j],
                                                     ab_kidx_norm_g[j], ab_kidx_norm_b[j])
            sh = lambda t: t.reshape(B, S, -1)
            o_w = D_IDX + H_IDX
            o_a = _dsa(sh(cq), sh(ckv), sh(kidx), sh(small[:, D_IDX:o_w]),
                       ab_w_uq[j], ab_w_uk[j], ab_w_uv[j], ab_w_qidx[j])
            o_b = _gdn(sh(qkvz), sh(small[:, o_w:o_w + H_B]), sh(small[:, o_w + H_B:o_w + 2 * H_B]),
                       ab_conv_w[j], ab_a_log[j], ab_dt_bias[j], ab_out_norm_g[j])
            a1, a2, w_out = o_a.reshape(T, -1), o_b.reshape(T, -1), ab_w_out[j]
        else:
            q, k, v, u = _cd_in_proj(h, cd_w_in[j])
            sh = lambda t: t.reshape(B, S, -1)
            o_c = _dilated_attention(sh(q), sh(k), sh(v))
            o_d = _multiscale_pool(sh(u), cd_pool_w[j], cd_pool_scale[j])
            a1, a2, w_out = o_c.reshape(T, -1), o_d.reshape(T, -1), cd_w_out[j]
        h1, top_e, gate_t, rank, cnt = _post_mixer(a1, a2, h, w_out, ln_mix_g[i], ln_mix_b[i],
                                                   router_w[i], router_b[i])
        ys, dest = _moe(h1, top_e, rank, cnt, w_gu[i], b_gu[i], w_down[i], b_down[i])
        h = _layer_tail(h1, ys, dest, gate_t, p[i].reshape(T, -1), ple_w_gate[i], ple_w_proj[i],
                        ln_ffn_g[i], ln_ffn_b[i])
    return h.reshape(B, S, D)
```

```python
import functools
import math

import jax
import jax.numpy as jnp
from jax import lax
from jax.experimental import pallas as pl
from jax.experimental.pallas import tpu as pltpu

F32 = jnp.float32
BF16 = jnp.bfloat16
I32 = jnp.int32

DEPTH = 2
H_A, D_NOPE, D_VA, R_Q, R_KV, H_IDX, D_IDX = 8, 64, 64, 256, 128, 8, 64
TOPK_MAX = 256
H_B, D_B, CONV_K, DN_CHUNK = 4, 128, 4, 64
H_C, DH_C = 12, 64
DILATED_PATTERNS = ((128, 1), (512, 4), (2048, 16))
POOL_WINDOWS = (2, 4, 8, 16)
POOL_GROUP = 64
N_EXPERTS, TOP_K = 32, 4
SWIGLU_LIMIT, SWIGLU_ALPHA = 7.0, 1.702
ALPHA = (2 * DEPTH) ** 0.25

VMEM_LIMIT_BYTES = 56 * 1024 * 1024
HIGHEST = lax.Precision.HIGHEST
NEG_INF = float("-inf")
LOG2E = math.log2(math.e)
INT_MIN = -2 ** 31
KEY_NEG_INF = (0xFF800000 ^ 0x7FFFFFFF) - 2 ** 32


def _cparams(sem):
    return pltpu.CompilerParams(dimension_semantics=sem, vmem_limit_bytes=VMEM_LIMIT_BYTES)


def _dot(a, b, precision=None):
    return jnp.dot(a, b, preferred_element_type=F32, precision=precision)


def _dot_nt(a, b, precision=None):
    return lax.dot_general(a, b, (((1,), (1,)), ((), ())), preferred_element_type=F32,
                           precision=precision)


def _dot_tn(a, b, precision=None):
    return lax.dot_general(a, b, (((0,), (0,)), ((), ())), preferred_element_type=F32,
                           precision=precision)


def _sigmoid(x):
    return 1.0 / (1.0 + jnp.exp(-x))


def _silu(x):
    return x * _sigmoid(x)


def _layernorm_rows(x, g, b, eps=1e-5):
    mu = jnp.mean(x, axis=-1, keepdims=True)
    xc = x - mu
    var = jnp.mean(xc * xc, axis=-1, keepdims=True)
    return xc * lax.rsqrt(var + eps) * g + b


AB_SMALL = 128


def _ab_proj_kernel(x_ref, w_ref, gq_ref, gkv_ref, gk_ref, bk_ref,
                    cq_ref, ckv_ref, kidx_ref, small_ref, qkvz_ref):
    x = x_ref[...].astype(BF16)
    cq = _dot(x, w_ref[:, 0:R_Q])
    cq = cq * lax.rsqrt(jnp.mean(cq * cq, axis=-1, keepdims=True) + 1e-6) * gq_ref[...]
    cq_ref[...] = cq.astype(cq_ref.dtype)
    ckv = _dot(x, w_ref[:, R_Q:R_Q + R_KV])
    ckv = ckv * lax.rsqrt(jnp.mean(ckv * ckv, axis=-1, keepdims=True) + 1e-6) * gkv_ref[...]
    ckv_ref[...] = ckv.astype(ckv_ref.dtype)
    off = R_Q + R_KV
    sm = _dot(x, w_ref[:, off:off + AB_SMALL])
    small_ref[...] = sm
    lane = lax.broadcasted_iota(I32, sm.shape, 1)
    is_k = lane < D_IDX
    mu = jnp.sum(jnp.where(is_k, sm, 0.0), axis=-1, keepdims=True) * (1.0 / D_IDX)
    xc = jnp.where(is_k, sm - mu, 0.0)
    var = jnp.sum(xc * xc, axis=-1, keepdims=True) * (1.0 / D_IDX)
    kn = xc * lax.rsqrt(var + 1e-5) * gk_ref[...] + bk_ref[...]
    kidx_ref[...] = kn[:, :D_IDX].astype(kidx_ref.dtype)
    off += AB_SMALL
    qkvz_ref[...] = _dot(x, w_ref[:, off:])


def _ab_in_proj(h2d, w_in, q_norm_g, kv_norm_g, kidx_g, kidx_b, tm=512):
    T, D = h2d.shape
    W = H_B * D_B
    o = [0, R_Q, R_Q + R_KV, R_Q + R_KV + D_IDX, R_Q + R_KV + D_IDX + H_IDX]
    o_q = o[4]
    o_b = o_q + 4 * W
    pad = AB_SMALL - (D_IDX + H_IDX + 2 * H_B)
    w_perm = jnp.concatenate([
        w_in[:, o[0]:o[2]],
        w_in[:, o[2]:o[4]], w_in[:, o_b:o_b + 2 * H_B],
        jnp.zeros((D, pad), w_in.dtype),
        w_in[:, o_q:o_b],
    ], axis=1).astype(BF16)
    n_all = w_perm.shape[1]
    gk = jnp.zeros((1, AB_SMALL), F32).at[0, :D_IDX].set(kidx_g)
    bk = jnp.zeros((1, AB_SMALL), F32).at[0, :D_IDX].set(kidx_b)
    row = lambda i: (i, 0)
    fixed = lambda i: (0, 0)
    return pl.pallas_call(
        _ab_proj_kernel,
        grid=(T // tm,),
        in_specs=[pl.BlockSpec((tm, D), row), pl.BlockSpec((D, n_all), fixed),
                  pl.BlockSpec((1, R_Q), fixed), pl.BlockSpec((1, R_KV), fixed),
                  pl.BlockSpec((1, AB_SMALL), fixed), pl.BlockSpec((1, AB_SMALL), fixed)],
        out_specs=[pl.BlockSpec((tm, R_Q), row), pl.BlockSpec((tm, R_KV), row),
                   pl.BlockSpec((tm, D_IDX), row), pl.BlockSpec((tm, AB_SMALL), row),
                   pl.BlockSpec((tm, 4 * W), row)],
        out_shape=[jax.ShapeDtypeStruct((T, R_Q), BF16), jax.ShapeDtypeStruct((T, R_KV), BF16),
                   jax.ShapeDtypeStruct((T, D_IDX), BF16), jax.ShapeDtypeStruct((T, AB_SMALL), F32),
                   jax.ShapeDtypeStruct((T, 4 * W), F32)],
        compiler_params=_cparams(("parallel",)),
        name="ab_in_proj",
    )(h2d, w_perm, q_norm_g.reshape(1, -1), kv_norm_g.reshape(1, -1), gk, bk)


def _sort_key(x):
    bits = pltpu.bitcast(x + 0.0, I32)
    return jnp.where(bits < 0, bits ^ 0x7FFFFFFF, bits)


def _dsa_kernel(cq_ref, ckv_ref, kidx_ref, widx_ref, wqidx_ref, wuq_ref, wukbd_ref, wuvbd_ref,
                o_ref, key_ref, qlat_ref, m_ref, l_ref, acc_ref, *, qb, kc, topk, seq_bits):
    i = pl.program_id(1)
    nck = ((i + 1) * qb + kc - 1) // kc
    n_lt = kc // 128
    cq = cq_ref[0]
    widx = widx_ref[0] * (H_IDX ** -0.5 * D_IDX ** -0.5)
    q_pos = i * qb + lax.broadcasted_iota(I32, (qb, 1), 0)

    qidx = [_dot(cq, wqidx_ref[h]).astype(BF16) for h in range(H_IDX)]

    def idx_body(c, carry):
        kblk = kidx_ref[0, pl.ds(pl.multiple_of(c * kc, kc), kc), :]
        isc = jnp.zeros((qb, kc), F32)
        for h in range(H_IDX):
            sc = _dot_nt(qidx[h], kblk)
            isc = isc + jnp.maximum(sc, 0.0) * widx[:, h:h + 1]
        k_pos = c * kc + lax.broadcasted_iota(I32, (1, kc), 1)
        key_ref[c] = jnp.where(k_pos <= q_pos, _sort_key(isc), KEY_NEG_INF)
        return carry

    lax.fori_loop(0, nck, idx_body, 0)

    def count_where(pred):
        def body(c, acc):
            k = key_ref[c]
            for j in range(n_lt):
                acc = acc + jnp.where(pred(k[:, j * 128:(j + 1) * 128], c, j), 1, 0)
            return acc
        acc = lax.fori_loop(0, nck, body, jnp.zeros((qb, 128), I32))
        return jnp.sum(acc, axis=1, keepdims=True)

    def bit_body(b, t):
        cand = t + (jnp.int32(1) << (31 - b))
        cnt = count_where(lambda k, c, j: k >= cand)
        return jnp.where(cnt >= topk, cand, t)

    thr = lax.fori_loop(0, 32, bit_body, jnp.full((qb, 1), INT_MIN, I32))
    thr = jnp.maximum(thr, KEY_NEG_INF + 1)

    c_gt = count_where(lambda k, c, j: k > thr)
    c_ge = count_where(lambda k, c, j: k >= thr)
    need = topk - c_gt
    excess = jnp.max(c_ge - c_gt - need)

    @pl.when(excess > 0)
    def _():
        def idx_of(c, j):
            return c * kc + j * 128 + lax.broadcasted_iota(I32, (1, 128), 1)

        def pos_body(b, m):
            cand = m + (jnp.int32(1) << (seq_bits - 1 - b))
            cnt = count_where(lambda k, c, j: (k == thr) & (idx_of(c, j) < cand))
            return jnp.where(cnt < need, cand, m)

        m = lax.fori_loop(0, seq_bits, pos_body, jnp.zeros((qb, 1), I32))

        def demote(c, carry):
            k = key_ref[c]
            idx = c * kc + lax.broadcasted_iota(I32, (1, kc), 1)
            key_ref[c] = jnp.where((k == thr) & (idx > m), thr - 1, k)
            return carry

        lax.fori_loop(0, nck, demote, 0)

    q = _dot(cq, wuq_ref[...]).astype(BF16)
    qlat = _dot(q, wukbd_ref[...]) * (D_NOPE ** -0.5 * LOG2E)
    for h in range(H_A):
        qlat_ref[h * qb:(h + 1) * qb, :] = qlat[:, h * R_KV:(h + 1) * R_KV].astype(BF16)
    m_ref[...] = jnp.full(m_ref.shape, -1e30, F32)
    l_ref[...] = jnp.zeros(l_ref.shape, F32)
    acc_ref[...] = jnp.zeros(acc_ref.shape, F32)

    def att_body(c, carry):
        kv = ckv_ref[0, pl.ds(pl.multiple_of(c * kc, kc), kc), :]
        bias = jnp.where(key_ref[c] >= thr, 0.0, NEG_INF)
        s = _dot_nt(qlat_ref[...], kv)
        s = (s.reshape(H_A, qb, kc) + bias[None]).reshape(H_A * qb, kc)
        tiles = [s[:, j * 128:(j + 1) * 128] for j in range(n_lt)]
        m_cur = tiles[0]
        for t in tiles[1:]:
            m_cur = jnp.maximum(m_cur, t)
        m_old = m_ref[...]
        m_new = jnp.maximum(m_old, jnp.max(m_cur, axis=1, keepdims=True))
        ps = [jnp.exp2(t - m_new) for t in tiles]
        a = jnp.exp2(m_old - m_new)
        psum = ps[0]
        for t in ps[1:]:
            psum = psum + t
        l_ref[...] = a * l_ref[...] + psum
        p = jnp.concatenate([t.astype(BF16) for t in ps], axis=1)
        acc_ref[...] = a * acc_ref[...] + _dot(p, kv)
        m_ref[...] = m_new
        return carry

    lax.fori_loop(0, nck, att_body, 0)
    o_all = acc_ref[...] / jnp.sum(l_ref[...], axis=1, keepdims=True)
    o_lat = jnp.concatenate([o_all[h * qb:(h + 1) * qb, :] for h in range(H_A)], axis=1)
    o_ref[0] = _dot(o_lat.astype(BF16), wuvbd_ref[...]).astype(o_ref.dtype)


def _block_diag(blocks):
    n = len(blocks)
    r, c = blocks[0].shape
    out = jnp.zeros((n * r, n * c), blocks[0].dtype)
    for k, blk in enumerate(blocks):
        out = out.at[k * r:(k + 1) * r, k * c:(k + 1) * c].set(blk)
    return out


def _dsa(cq, ckv, kidx, widx, w_uq, w_uk, w_uv, w_qidx, qb=128, kc=512):
    B, S, _ = cq.shape
    kc = min(kc, S)
    topk = min(TOPK_MAX, S // 4)
    assert S % kc == 0 and kc % qb == 0 and kc >= topk
    seq_bits = max(1, (S - 1).bit_length())
    wqidx = jnp.transpose(w_qidx, (1, 0, 2)).astype(BF16)
    wuq = w_uq.reshape(R_Q, H_A * D_NOPE).astype(BF16)
    wukbd = _block_diag([w_uk[:, h, :].T for h in range(H_A)]).astype(BF16)
    wuvbd = _block_diag([w_uv[:, h, :] for h in range(H_A)]).astype(BF16)
    kern = functools.partial(_dsa_kernel, qb=qb, kc=kc, topk=topk, seq_bits=seq_bits)
    blk_q = lambda b, i: (b, i, 0)
    seq = lambda b, i: (b, 0, 0)
    fix2 = lambda b, i: (0, 0)
    fix3 = lambda b, i: (0, 0, 0)
    return pl.pallas_call(
        kern,
        grid=(B, S // qb),
        in_specs=[pl.BlockSpec((1, qb, R_Q), blk_q), pl.BlockSpec((1, S, R_KV), seq),
                  pl.BlockSpec((1, S, D_IDX), seq), pl.BlockSpec((1, qb, H_IDX), blk_q),
                  pl.BlockSpec(wqidx.shape, fix3), pl.BlockSpec(wuq.shape, fix2),
                  pl.BlockSpec(wukbd.shape, fix2), pl.BlockSpec(wuvbd.shape, fix2)],
        out_specs=pl.BlockSpec((1, qb, H_A * D_VA), blk_q),
        out_shape=jax.ShapeDtypeStruct((B, S, H_A * D_VA), BF16),
        scratch_shapes=[pltpu.VMEM((S // kc, qb, kc), I32),
                        pltpu.VMEM((H_A * qb, R_KV), BF16),
                        pltpu.VMEM((H_A * qb, R_KV), F32), pltpu.VMEM((H_A * qb, R_KV), F32),
                        pltpu.VMEM((H_A * qb, R_KV), F32)],
        compiler_params=_cparams(("parallel", "arbitrary")),
        name="dsa_attention",
    )(cq, ckv, kidx, widx, wqidx, wuq, wukbd, wuvbd)


GDN_HALO = 8
INV_PRECISION = HIGHEST


def _softplus(x):
    return jnp.maximum(x, 0.0) + jnp.log1p(jnp.exp(-jnp.abs(x)))


def _gdn_kernel(qkvz_ref, abc_ref, abr_ref, convw_ref, prm_c_ref, prm_r_ref, ng_ref, tri_ref,
                o_ref, xbuf_ref, state_ref, conv_ref, *, cb):
    C = DN_CHUNK
    W = H_B * D_B
    j = pl.program_id(1)

    @pl.when(j == 0)
    def _():
        xbuf_ref[0:GDN_HALO, :] = jnp.zeros((GDN_HALO, 3 * W), F32)
        state_ref[...] = jnp.zeros(state_ref.shape, F32)

    xbuf_ref[GDN_HALO:, :] = qkvz_ref[0, :, 0:3 * W]
    acc = xbuf_ref[GDN_HALO:, :] * convw_ref[CONV_K - 1:CONV_K, :]
    for t in range(CONV_K - 1):
        sh = CONV_K - 1 - t
        acc = acc + xbuf_ref[pl.ds(GDN_HALO - sh, cb), :] * convw_ref[t:t + 1, :]
    conv_ref[...] = _silu(acc)
    xbuf_ref[0:GDN_HALO, :] = xbuf_ref[cb:cb + GDN_HALO, :]

    abc = abc_ref[0]
    abr = abr_ref[0]
    beta_c = _sigmoid(abc)
    g_c = -jnp.exp(prm_r_ref[0:1, :]) * _softplus(abc + prm_r_ref[1:2, :])
    g_r = -jnp.exp(prm_c_ref[:, 0:1]) * _softplus(abr + prm_c_ref[:, 1:2])
    gc_c = _dot(tri_ref[...], g_c, precision=HIGHEST)
    gc_r = _dot_nt(g_r, tri_ref[...], precision=HIGHEST)

    ri = lax.broadcasted_iota(I32, (cb, cb), 0)
    ci = lax.broadcasted_iota(I32, (cb, cb), 1)
    same = (ri // C) == (ci // C)
    lower = same & (ri >= ci)
    strict = same & (ri > ci)
    eye = jnp.where(ri == ci, 1.0, 0.0)
    bf = lambda t: t.astype(BF16)

    for h in range(H_B):
        q = conv_ref[:, h * D_B:(h + 1) * D_B]
        k = conv_ref[:, W + h * D_B:W + (h + 1) * D_B]
        v = conv_ref[:, 2 * W + h * D_B:2 * W + (h + 1) * D_B]
        q = q * lax.rsqrt(jnp.sum(q * q, axis=-1, keepdims=True) + 1e-6) * (D_B ** -0.5)
        k = k * lax.rsqrt(jnp.sum(k * k, axis=-1, keepdims=True) + 1e-6)
        beta = beta_c[:, h:h + 1]
        gcol = gc_c[:, H_B + h:H_B + h + 1]
        grow = gc_r[H_B + h:H_B + h + 1, :]
        decay = jnp.exp(jnp.where(lower, gcol - grow, NEG_INF))
        kb = k * beta
        vb = v * beta
        kq = bf(k)
        a_mat = jnp.where(strict, _dot_nt(bf(kb), kq) * decay, 0.0)
        xm = -a_mat
        t_mat = eye + xm
        for _ in range(int(math.log2(C)) - 1):
            xm = _dot(xm, xm, precision=INV_PRECISION)
            t_mat = t_mat + _dot(t_mat, xm, precision=INV_PRECISION)
        egc = jnp.exp(gcol)
        tq = bf(t_mat)
        u = _dot(tq, bf(vb))
        w = bf(_dot(tq, bf(kb * egc)))
        qk = bf(jnp.where(lower, _dot_nt(bf(q), kq) * decay, 0.0))
        q_dec = bf(q * egc)
        for n in range(cb // C):
            r0 = n * C
            glast = gcol[r0 + C - 1:r0 + C, :]
            k_dec = bf(k[r0:r0 + C] * jnp.exp(glast - gcol[r0:r0 + C]))
            st = state_ref[h]
            stq = bf(st)
            v_new = u[r0:r0 + C] - _dot(w[r0:r0 + C], stq)
            vq = bf(v_new)
            o = _dot(q_dec[r0:r0 + C], stq) + _dot(qk[r0:r0 + C, r0:r0 + C], vq)
            state_ref[h] = st * jnp.exp(glast) + _dot_tn(k_dec, vq)
            o = o * lax.rsqrt(jnp.mean(o * o, axis=-1, keepdims=True) + 1e-6) * ng_ref[...]
            z = qkvz_ref[0, r0:r0 + C, 3 * W + h * D_B:3 * W + (h + 1) * D_B]
            o_ref[0, r0:r0 + C, h * D_B:(h + 1) * D_B] = (o * _silu(z)).astype(o_ref.dtype)


def _gdn(qkvz, b, a, conv_w, a_log, dt_bias, norm_g, cb=256):
    B, S, _ = qkvz.shape
    W = H_B * D_B
    cb = min(cb, S)
    abc = jnp.concatenate([b, a], axis=-1)
    abr = jnp.transpose(abc, (0, 2, 1))
    zeros = jnp.zeros((H_B,), F32)
    prm = jnp.stack([jnp.concatenate([zeros, a_log]), jnp.concatenate([zeros, dt_bias])])
    idx = jnp.arange(cb)
    tri = ((idx[:, None] >= idx[None, :]) & (idx[:, None] // DN_CHUNK == idx[None, :] // DN_CHUNK)).astype(F32)
    kern = functools.partial(_gdn_kernel, cb=cb)
    blk = lambda bi, j: (bi, j, 0)
    fix = lambda bi, j: (0, 0)
    return pl.pallas_call(
        kern,
        grid=(B, S // cb),
        in_specs=[pl.BlockSpec((1, cb, 4 * W), blk), pl.BlockSpec((1, cb, 2 * H_B), blk),
                  pl.BlockSpec((1, 2 * H_B, cb), lambda bi, j: (bi, 0, j)),
                  pl.BlockSpec((CONV_K, 3 * W), fix), pl.BlockSpec((2 * H_B, 2), fix),
                  pl.BlockSpec((2, 2 * H_B), fix), pl.BlockSpec((1, D_B), fix),
                  pl.BlockSpec((cb, cb), fix)],
        out_specs=pl.BlockSpec((1, cb, W), blk),
        out_shape=jax.ShapeDtypeStruct((B, S, W), BF16),
        scratch_shapes=[pltpu.VMEM((cb + GDN_HALO, 3 * W), F32), pltpu.VMEM((H_B, D_B, D_B), F32),
                        pltpu.VMEM((cb, 3 * W), F32)],
        compiler_params=_cparams(("parallel", "arbitrary")),
        name="gated_deltanet",
    )(qkvz, abc, abr, conv_w, prm.T, prm, norm_g.reshape(1, D_B), tri)


def _post_mixer_kernel(a1_ref, a2_ref, h_ref, w1_ref, w2_ref, g_ref, b_ref, rw_ref, rb_ref, triu_ref,
                       h1_ref, tope_ref, gate_ref, rank_ref, cnt_ref, carry_ref):
    i = pl.program_id(0)
    E, tm = rw_ref.shape[0], h_ref.shape[0]

    @pl.when(i == 0)
    def _():
        carry_ref[...] = jnp.zeros(carry_ref.shape, F32)

    mix = _dot(a1_ref[...].astype(BF16), w1_ref[...]) + _dot(a2_ref[...].astype(BF16), w2_ref[...])
    h1 = _layernorm_rows(ALPHA * h_ref[...] + mix, g_ref[...], b_ref[...])
    h1_ref[...] = h1

    logits = _dot_nt(rw_ref[...], h1, precision=HIGHEST) + rb_ref[...]
    erow = lax.broadcasted_iota(I32, (E, tm), 0)
    sel = jnp.zeros((E, tm), F32)
    onehots, tops = [], []
    for k in range(TOP_K):
        mx = jnp.max(logits, axis=0, keepdims=True)
        idx = jnp.min(jnp.where(logits == mx, erow, E), axis=0, keepdims=True)
        oh = erow == idx
        logits = jnp.where(oh, NEG_INF, logits)
        sel = sel + jnp.where(oh, 1.0, 0.0)
        onehots.append(oh)
        tops.append(mx)
        tope_ref[k:k + 1, :] = idx
    exps = [jnp.exp(t - tops[0]) for t in tops]
    den = exps[0] + exps[1] + exps[2] + exps[3]
    for k in range(TOP_K):
        gate_ref[k:k + 1, :] = exps[k] / den
    incl = _dot(sel.astype(BF16), triu_ref[...])
    excl = incl - sel + carry_ref[:, 0:1]
    for k in range(TOP_K):
        rank_ref[k:k + 1, :] = jnp.sum(jnp.where(onehots[k], excl, 0.0), axis=0, keepdims=True).astype(I32)
    carry_ref[...] = carry_ref[...] + jnp.sum(sel, axis=1, keepdims=True)
    cnt_ref[...] = carry_ref[...]


def _post_mixer(a1, a2, h2d, w_out, ln_g, ln_b, router_w, router_b, tm=512):
    T, D = h2d.shape
    E = router_w.shape[1]
    n1, n2 = a1.shape[1], a2.shape[1]
    w1 = w_out[:n1].astype(BF16)
    w2 = w_out[n1:].astype(BF16)
    idx = jnp.arange(tm)
    triu = (idx[:, None] <= idx[None, :]).astype(BF16)
    row = lambda i: (i, 0)
    col = lambda i: (0, i)
    fix = lambda i: (0, 0)
    return pl.pallas_call(
        _post_mixer_kernel,
        grid=(T // tm,),
        in_specs=[pl.BlockSpec((tm, n1), row), pl.BlockSpec((tm, n2), row), pl.BlockSpec((tm, D), row),
                  pl.BlockSpec((n1, D), fix), pl.BlockSpec((n2, D), fix),
                  pl.BlockSpec((1, D), fix), pl.BlockSpec((1, D), fix),
                  pl.BlockSpec((E, D), fix), pl.BlockSpec((E, 1), fix), pl.BlockSpec((tm, tm), fix)],
        out_specs=[pl.BlockSpec((tm, D), row),
                   pl.BlockSpec((TOP_K, tm), col), pl.BlockSpec((TOP_K, tm), col),
                   pl.BlockSpec((TOP_K, tm), col), pl.BlockSpec((E, 128), fix)],
        out_shape=[jax.ShapeDtypeStruct((T, D), F32),
                   jax.ShapeDtypeStruct((TOP_K, T), I32), jax.ShapeDtypeStruct((TOP_K, T), F32),
                   jax.ShapeDtypeStruct((TOP_K, T), I32), jax.ShapeDtypeStruct((E, 128), F32)],
        scratch_shapes=[pltpu.VMEM((E, 128), F32)],
        compiler_params=_cparams(("arbitrary",)),
        name="post_mixer_router",
    )(a1, a2, h2d, w1, w2, ln_g.reshape(1, D), ln_b.reshape(1, D), router_w.T, router_b.reshape(E, 1), triu)


def _dispatch_kernel(dest_ref, h_ref, xs_init_hbm, xs_hbm, sem, *, tt, n_tok):
    del xs_init_hbm
    base = pl.program_id(0) * tt

    def row_copy(r, k):
        d = dest_ref[k * n_tok + base + r]
        return pltpu.make_async_copy(h_ref.at[pl.ds(r, 1)], xs_hbm.at[pl.ds(d, 1)], sem)

    def start(r, c):
        for k in range(TOP_K):
            row_copy(r, k).start()
        return c

    def wait(r, c):
        for k in range(TOP_K):
            row_copy(r, k).wait()
        return c

    lax.fori_loop(0, tt, start, 0)
    lax.fori_loop(0, tt, wait, 0)


def _moe_dispatch(h1, dest_flat, n_slots, tt=512):
    T, D = h1.shape
    any_spec = pl.BlockSpec(memory_space=pl.ANY)
    return pl.pallas_call(
        functools.partial(_dispatch_kernel, tt=tt, n_tok=T),
        grid_spec=pltpu.PrefetchScalarGridSpec(
            num_scalar_prefetch=1, grid=(T // tt,),
            in_specs=[pl.BlockSpec((tt, D), lambda i, d: (i, 0)), any_spec], out_specs=any_spec,
            scratch_shapes=[pltpu.SemaphoreType.DMA(())]),
        out_shape=jax.ShapeDtypeStruct((n_slots, D), h1.dtype),
        input_output_aliases={2: 0},
        compiler_params=_cparams(("arbitrary",)),
        name="moe_dispatch",
    )(dest_flat, h1, jnp.zeros((n_slots, D), h1.dtype))


MOE_BLOCK = 256


def _moe_kernel(be_ref, nu_ref, x_ref, wg_ref, wu_ref, bg_ref, bu_ref, wd_ref, bd_ref, y_ref):
    i = pl.program_id(0)

    @pl.when(i < nu_ref[0])
    def _():
        x = x_ref[...].astype(BF16)
        g = _dot_nt(x, wg_ref[0]) + bg_ref[0]
        u = _dot_nt(x, wu_ref[0]) + bu_ref[0]
        gt = jnp.minimum(g, SWIGLU_LIMIT)
        up = jnp.clip(u, -SWIGLU_LIMIT, SWIGLU_LIMIT)
        hid = (up + 1.0) * (gt * _sigmoid(gt * SWIGLU_ALPHA))
        y_ref[...] = _dot(hid.astype(BF16), wd_ref[0]) + bd_ref[0]

    @pl.when(i >= nu_ref[0])
    def _():
        y_ref[...] = jnp.zeros(y_ref.shape, y_ref.dtype)


def _moe_experts(xs, blk_e, n_used, wgu_t, bg, bu, wd, bd):
    P, D = xs.shape
    F = wgu_t.shape[1]
    bm = MOE_BLOCK
    wsel = lambda i, be, nu: (be[i], 0, 0)
    wsel_up = lambda i, be, nu: (be[i], 0, 1)
    row = lambda i, be, nu: (i, 0)
    return pl.pallas_call(
        _moe_kernel,
        grid_spec=pltpu.PrefetchScalarGridSpec(
            num_scalar_prefetch=2,
            grid=(P // bm,),
            in_specs=[pl.BlockSpec((bm, D), row),
                      pl.BlockSpec((1, F, D), wsel), pl.BlockSpec((1, F, D), wsel_up),
                      pl.BlockSpec((1, 1, F), wsel), pl.BlockSpec((1, 1, F), wsel),
                      pl.BlockSpec((1, F, D), wsel), pl.BlockSpec((1, 1, D), wsel)],
            out_specs=pl.BlockSpec((bm, D), row)),
        out_shape=jax.ShapeDtypeStruct((P, D), F32),
        compiler_params=_cparams(("arbitrary",)),
        name="moe_experts",
    )(blk_e, n_used, xs, wgu_t, wgu_t, bg, bu, wd, bd)


def _moe(h1, top_e, rank, cnt, w_gu, b_gu, w_down, b_down):
    T, D = h1.shape
    E = w_gu.shape[0]
    bm = MOE_BLOCK
    counts = cnt[:, 0].astype(I32)
    padded = (counts + bm - 1) // bm * bm
    pad_end = jnp.cumsum(padded)
    pad_start = pad_end - padded
    e_ids = jnp.arange(E, dtype=I32)[:, None, None]
    start_of = jnp.sum(jnp.where(top_e[None] == e_ids, pad_start[:, None, None], 0), axis=0)
    dest = (start_of + rank).reshape(-1)
    nblk = (T * TOP_K) // bm + E
    blk_first = jnp.arange(nblk, dtype=I32) * bm
    blk_e = jnp.minimum(jnp.sum(pad_end[None, :] <= blk_first[:, None], axis=1), E - 1).astype(I32)
    n_used = (pad_end[-1:] // bm).astype(I32)
    xs = _moe_dispatch(h1, dest, nblk * bm)
    F = w_down.shape[1]
    wgu_t = jnp.transpose(w_gu, (0, 2, 1)).astype(BF16).reshape(E, F, 2 * D)
    ys = _moe_experts(xs, blk_e, n_used, wgu_t, b_gu[:, None, 0::2], b_gu[:, None, 1::2],
                      w_down.astype(BF16), b_down[:, None, :])
    return ys, dest


def _tail_kernel(dest_ref, h1_ref, gate_ref, p_ref, wg_ref, wp_ref, g_ref, b_ref, ys_hbm, o_ref,
                 ybuf, sem, *, tm, n_tok):
    i = pl.program_id(0)

    def row_copy(tile, slot, r, k):
        d = dest_ref[k * n_tok + tile * tm + r]
        return pltpu.make_async_copy(ys_hbm.at[pl.ds(d, 1)], ybuf.at[slot, k, pl.ds(r, 1)], sem.at[slot])

    def start_tile(tile, slot):
        def body(r, c):
            for k in range(TOP_K):
                row_copy(tile, slot, r, k).start()
            return c
        lax.fori_loop(0, tm, body, 0)

    def wait_tile(tile, slot):
        def body(r, c):
            for k in range(TOP_K):
                row_copy(tile, slot, r, k).wait()
            return c
        lax.fori_loop(0, tm, body, 0)

    @pl.when(i == 0)
    def _():
        start_tile(0, 0)

    @pl.when(i + 1 < pl.num_programs(0))
    def _():
        start_tile(i + 1, (i + 1) % 2)

    slot = i % 2
    h1 = h1_ref[...]
    ple = _sigmoid(_dot(h1.astype(BF16), wg_ref[...])) * _dot(p_ref[...].astype(BF16), wp_ref[...])
    wait_tile(i, slot)
    ffn = ybuf[slot, 0] * gate_ref[:, 0:1]
    for k in range(1, TOP_K):
        ffn = ffn + ybuf[slot, k] * gate_ref[:, k:k + 1]
    o_ref[...] = _layernorm_rows(ALPHA * h1 + ffn + ple, g_ref[...], b_ref[...])


def _layer_tail(h1, ys, dest_flat, gate_t, p2d, ple_w_gate, ple_w_proj, ln_g, ln_b, tm=256):
    T, D = h1.shape
    PD = p2d.shape[1]
    row = lambda i, d: (i, 0)
    fix = lambda i, d: (0, 0)
    return pl.pallas_call(
        functools.partial(_tail_kernel, tm=tm, n_tok=T),
        grid_spec=pltpu.PrefetchScalarGridSpec(
            num_scalar_prefetch=1, grid=(T // tm,),
            in_specs=[pl.BlockSpec((tm, D), row), pl.BlockSpec((tm, TOP_K), row),
                      pl.BlockSpec((tm, PD), row), pl.BlockSpec((D, D), fix), pl.BlockSpec((PD, D), fix),
                      pl.BlockSpec((1, D), fix), pl.BlockSpec((1, D), fix),
                      pl.BlockSpec(memory_space=pl.ANY)],
            out_specs=pl.BlockSpec((tm, D), row),
            scratch_shapes=[pltpu.VMEM((2, TOP_K, tm, D), F32), pltpu.SemaphoreType.DMA((2,))]),
        out_shape=jax.ShapeDtypeStruct((T, D), F32),
        compiler_params=_cparams(("arbitrary",)),
        name="layer_tail",
    )(dest_flat, h1, gate_t.T, p2d, ple_w_gate.astype(BF16), ple_w_proj.astype(BF16),
      ln_g.reshape(1, D), ln_b.reshape(1, D), ys)


def _cd_proj_kernel(x_ref, w_ref, q_ref, k_ref, v_ref, u_ref):
    x = x_ref[...].astype(BF16)
    n = H_C * DH_C
    q_ref[...] = (_dot(x, w_ref[:, 0:n]) * (DH_C ** -0.5 * LOG2E)).astype(q_ref.dtype)
    k_ref[...] = _dot(x, w_ref[:, n:2 * n]).astype(k_ref.dtype)
    v_ref[...] = _dot(x, w_ref[:, 2 * n:3 * n]).astype(v_ref.dtype)
    u_ref[...] = _dot(x, w_ref[:, 3 * n:])


def _cd_in_proj(h2d, w_in, tm=512):
    T, D = h2d.shape
    n = H_C * DH_C
    nu = w_in.shape[1] - 3 * n
    row = lambda i: (i, 0)
    return pl.pallas_call(
        _cd_proj_kernel,
        grid=(T // tm,),
        in_specs=[pl.BlockSpec((tm, D), row), pl.BlockSpec(w_in.shape, lambda i: (0, 0))],
        out_specs=[pl.BlockSpec((tm, n), row)] * 3 + [pl.BlockSpec((tm, nu), row)],
        out_shape=[jax.ShapeDtypeStruct((T, n), BF16)] * 3 + [jax.ShapeDtypeStruct((T, nu), F32)],
        compiler_params=_cparams(("parallel",)),
        name="cd_in_proj",
    )(h2d, w_in.astype(BF16))


def _dilated_bias_table(qb):
    import numpy as np
    max_w = max(w for w, _ in DILATED_PATTERNS)
    ndc = max_w // qb + 1
    r = np.arange(qb)[:, None]
    j = np.arange(qb)[None, :]
    tbl = np.empty((ndc, qb, qb), np.float32)
    for dc in range(ndc):
        delta = dc * qb + r - j
        mult = np.zeros((qb, qb), np.float64)
        for w, d in DILATED_PATTERNS:
            mult += (delta >= 0) & (delta <= w) & (delta % d == 0)
        with np.errstate(divide="ignore"):
            tbl[dc] = np.log2(mult)
    return jnp.asarray(tbl)


def _dilated_kernel(q_ref, k_ref, v_ref, bias_ref, o_ref, q2_ref, m_ref, l_ref, acc_ref, *, qb, ndc):
    i = pl.program_id(2)
    n_lt = qb // 128
    q = q_ref[0]
    lo = lax.broadcasted_iota(I32, q.shape, 1) < DH_C
    zero = jnp.zeros(q.shape, q.dtype)
    q2_ref[0:qb, :] = jnp.where(lo, q, zero)
    q2_ref[qb:, :] = jnp.where(lo, zero, q)
    m_ref[...] = jnp.full(m_ref.shape, -1e30, F32)
    l_ref[...] = jnp.zeros(l_ref.shape, F32)
    acc_ref[...] = jnp.zeros(acc_ref.shape, F32)

    def body(c, carry):
        r0 = pl.multiple_of(c * qb, qb)
        kk = k_ref[0, pl.ds(r0, qb), :]
        vv = v_ref[0, pl.ds(r0, qb), :]
        s = _dot_nt(q2_ref[...], kk)
        s = (s.reshape(2, qb, qb) + bias_ref[i - c][None]).reshape(2 * qb, qb)
        tiles = [s[:, j * 128:(j + 1) * 128] for j in range(n_lt)]
        m_cur = tiles[0]
        for t in tiles[1:]:
            m_cur = jnp.maximum(m_cur, t)
        m_old = m_ref[...]
        m_new = jnp.maximum(m_old, jnp.max(m_cur, axis=1, keepdims=True))
        ps = [jnp.exp2(t - m_new) for t in tiles]
        a = jnp.exp2(m_old - m_new)
        psum = ps[0]
        for t in ps[1:]:
            psum = psum + t
        l_ref[...] = a * l_ref[...] + psum
        p = jnp.concatenate([t.astype(BF16) for t in ps], axis=1)
        acc_ref[...] = a * acc_ref[...] + _dot(p, vv)
        m_ref[...] = m_new
        return carry

    lax.fori_loop(jnp.maximum(i - (ndc - 1), 0), i + 1, body, 0)
    o_all = acc_ref[...] / jnp.sum(l_ref[...], axis=1, keepdims=True)
    o_ref[0] = jnp.where(lo, o_all[0:qb, :], o_all[qb:, :]).astype(o_ref.dtype)


def _dilated_attention(q, k, v, qb=512):
    B, S, n = q.shape
    qb = min(qb, S)
    bias = _dilated_bias_table(qb)
    ndc = bias.shape[0]
    pw = 2 * DH_C
    kern = functools.partial(_dilated_kernel, qb=qb, ndc=ndc)
    blk = lambda b, pr, i: (b, i, pr)
    seq = lambda b, pr, i: (b, 0, pr)
    return pl.pallas_call(
        kern,
        grid=(B, n // pw, S // qb),
        in_specs=[pl.BlockSpec((1, qb, pw), blk), pl.BlockSpec((1, S, pw), seq),
                  pl.BlockSpec((1, S, pw), seq), pl.BlockSpec(bias.shape, lambda b, pr, i: (0, 0, 0))],
        out_specs=pl.BlockSpec((1, qb, pw), blk),
        out_shape=jax.ShapeDtypeStruct((B, S, n), BF16),
        scratch_shapes=[pltpu.VMEM((2 * qb, pw), BF16), pltpu.VMEM((2 * qb, pw), F32),
                        pltpu.VMEM((2 * qb, pw), F32), pltpu.VMEM((2 * qb, pw), F32)],
        compiler_params=_cparams(("parallel", "parallel", "arbitrary")),
        name="dilated_attention",
    )(q, k, v, bias)


POOL_HALO = 16


def _pool_kernel(u_ref, w_ref, sc_ref, o_ref, xbuf_ref, *, tm):
    j = pl.program_id(1)

    @pl.when(j == 0)
    def _():
        xbuf_ref[0:POOL_HALO, :] = jnp.zeros((POOL_HALO, xbuf_ref.shape[1]), F32)

    xbuf_ref[POOL_HALO:, :] = u_ref[0]
    x = xbuf_ref[POOL_HALO:, :]
    grp = lax.broadcasted_iota(I32, (1, x.shape[1]), 1) // POOL_GROUP
    run = x
    sel = jnp.zeros(x.shape, F32)
    win = jnp.zeros((1, x.shape[1]), F32)
    for d in range(1, max(POOL_WINDOWS)):
        run = run + xbuf_ref[pl.ds(POOL_HALO - d, tm), :]
        if d + 1 in POOL_WINDOWS:
            gi = POOL_WINDOWS.index(d + 1)
            sel = jnp.where(grp == gi, run, sel)
            win = jnp.where(grp == gi, float(d + 1), win)
    pos = j * tm + lax.broadcasted_iota(I32, (tm, 1), 0)
    mean = sel / jnp.minimum((pos + 1).astype(F32), win)
    o_ref[0] = _dot((mean - x).astype(BF16), w_ref[...]) * sc_ref[...]
    xbuf_ref[0:POOL_HALO, :] = xbuf_ref[tm:tm + POOL_HALO, :]


def _multiscale_pool(u, pool_w, pool_scale, tm=512):
    B, S, n = u.shape
    tm = min(tm, S)
    wbd = _block_diag([pool_w[g] for g in range(pool_w.shape[0])]).astype(BF16)
    blk = lambda b, j: (b, j, 0)
    fix = lambda b, j: (0, 0)
    return pl.pallas_call(
        functools.partial(_pool_kernel, tm=tm),
        grid=(B, S // tm),
        in_specs=[pl.BlockSpec((1, tm, n), blk), pl.BlockSpec((n, n), fix), pl.BlockSpec((1, n), fix)],
        out_specs=pl.BlockSpec((1, tm, n), blk),
        out_shape=jax.ShapeDtypeStruct((B, S, n), F32),
        scratch_shapes=[pltpu.VMEM((tm + POOL_HALO, n), F32)],
        compiler_params=_cparams(("parallel", "arbitrary")),
        name="multiscale_pool",
    )(u, wbd, pool_scale.reshape(1, n))


def kernel(x, p, ab_w_in, ab_q_norm_g, ab_kv_norm_g, ab_w_uq, ab_w_uk, ab_w_uv, ab_w_qidx,
           ab_kidx_norm_g, ab_kidx_norm_b, ab_conv_w, ab_a_log, ab_dt_bias, ab_out_norm_g, ab_w_out,
           cd_w_in, cd_pool_w, cd_pool_scale, cd_w_out, ln_mix_g, ln_mix_b, router_w, router_b,
           w_gu, b_gu, w_down, b_down, ple_w_proj, ple_w_gate, ln_ffn_g, ln_ffn_b):
    B, S, D = x.shape
    T = B * S
    h = x.reshape(T, D)
    for i in range(DEPTH):
        j = i // 2
        if i % 2 == 0:
            cq, ckv, kidx, small, qkvz = _ab_in_proj(h, ab_w_in[j], ab_q_norm_g[j], ab_kv_norm_g[j],
                                                     ab_kidx_norm_g[j], ab_kidx_norm_b[j])
            sh = lambda t: t.reshape(B, S, -1)
            o_w = D_IDX + H_IDX
            o_a = _dsa(sh(cq), sh(ckv), sh(kidx), sh(small[:, D_IDX:o_w]),
                       ab_w_uq[j], ab_w_uk[j], ab_w_uv[j], ab_w_qidx[j])
            o_b = _gdn(sh(qkvz), sh(small[:, o_w:o_w + H_B]), sh(small[:, o_w + H_B:o_w + 2 * H_B]),
                       ab_conv_w[j], ab_a_log[j], ab_dt_bias[j], ab_out_norm_g[j])
            a1, a2, w_out = o_a.reshape(T, -1), o_b.reshape(T, -1), ab_w_out[j]
        else:
            q, k, v, u = _cd_in_proj(h, cd_w_in[j])
            sh = lambda t: t.reshape(B, S, -1)
            o_c = _dilated_attention(sh(q), sh(k), sh(v))
            o_d = _multiscale_pool(sh(u), cd_pool_w[j], cd_pool_scale[j])
            a1, a2, w_out = o_c.reshape(T, -1), o_d.reshape(T, -1), cd_w_out[j]
        h1, top_e, gate_t, rank, cnt = _post_mixer(a1, a2, h, w_out, ln_mix_g[i], ln_mix_b[i],
                                                   router_w[i], router_b[i])
        ys, dest = _moe(h1, top_e, rank, cnt, w_gu[i], b_gu[i], w_down[i], b_down[i])
        h = _layer_tail(h1, ys, dest, gate_t, p[i].reshape(T, -1), ple_w_gate[i], ple_w_proj[i],
                        ln_ffn_g[i], ln_ffn_b[i])
    return h.reshape(B, S, D)
```

```python
import functools
import math

import jax
import jax.numpy as jnp
from jax import lax
from jax.experimental import pallas as pl
from jax.experimental.pallas import tpu as pltpu

F32 = jnp.float32
BF16 = jnp.bfloat16
I32 = jnp.int32

DEPTH = 2
H_A, D_NOPE, D_VA, R_Q, R_KV, H_IDX, D_IDX = 8, 64, 64, 256, 128, 8, 64
TOPK_MAX = 256
H_B, D_B, CONV_K, DN_CHUNK = 4, 128, 4, 64
H_C, DH_C = 12, 64
DILATED_PATTERNS = ((128, 1), (512, 4), (2048, 16))
POOL_WINDOWS = (2, 4, 8, 16)
POOL_GROUP = 64
N_EXPERTS, TOP_K = 32, 4
SWIGLU_LIMIT, SWIGLU_ALPHA = 7.0, 1.702
ALPHA = (2 * DEPTH) ** 0.25

VMEM_LIMIT_BYTES = 56 * 1024 * 1024
LANES = 128
HIGHEST = lax.Precision.HIGHEST
NEG_INF = float("-inf")
LOG2E = math.log2(math.e)
INT_MIN = -2 ** 31
KEY_NEG_INF = (0xFF800000 ^ 0x7FFFFFFF) - 2 ** 32


def _cparams(sem):
    return pltpu.CompilerParams(dimension_semantics=sem, vmem_limit_bytes=VMEM_LIMIT_BYTES)


def _dot(a, b, precision=None):
    return jnp.dot(a, b, preferred_element_type=F32, precision=precision)


def _dot_nt(a, b, precision=None):
    return lax.dot_general(a, b, (((1,), (1,)), ((), ())), preferred_element_type=F32,
                           precision=precision)


def _dot_tn(a, b, precision=None):
    return lax.dot_general(a, b, (((0,), (0,)), ((), ())), preferred_element_type=F32,
                           precision=precision)


def _sigmoid(x):
    return 1.0 / (1.0 + jnp.exp(-x))


def _silu(x):
    return x * _sigmoid(x)


def _layernorm_rows(x, g, b, eps=1e-5):
    mu = jnp.mean(x, axis=-1, keepdims=True)
    xc = x - mu
    var = jnp.mean(xc * xc, axis=-1, keepdims=True)
    return xc * lax.rsqrt(var + eps) * g + b


AB_SMALL = 128


def _ab_proj_kernel(x_ref, w_ref, gq_ref, gkv_ref, gk_ref, bk_ref,
                    cq_ref, ckv_ref, kidx_ref, small_ref, qkvz_ref):
    x = x_ref[...].astype(BF16)
    cq = _dot(x, w_ref[:, 0:R_Q])
    cq = cq * lax.rsqrt(jnp.mean(cq * cq, axis=-1, keepdims=True) + 1e-6) * gq_ref[...]
    cq_ref[...] = cq.astype(cq_ref.dtype)
    ckv = _dot(x, w_ref[:, R_Q:R_Q + R_KV])
    ckv = ckv * lax.rsqrt(jnp.mean(ckv * ckv, axis=-1, keepdims=True) + 1e-6) * gkv_ref[...]
    ckv_ref[...] = ckv.astype(ckv_ref.dtype)
    off = R_Q + R_KV
    sm = _dot(x, w_ref[:, off:off + AB_SMALL])
    small_ref[...] = sm
    lane = lax.broadcasted_iota(I32, sm.shape, 1)
    is_k = lane < D_IDX
    mu = jnp.sum(jnp.where(is_k, sm, 0.0), axis=-1, keepdims=True) * (1.0 / D_IDX)
    xc = jnp.where(is_k, sm - mu, 0.0)
    var = jnp.sum(xc * xc, axis=-1, keepdims=True) * (1.0 / D_IDX)
    kn = xc * lax.rsqrt(var + 1e-5) * gk_ref[...] + bk_ref[...]
    kidx_ref[...] = kn[:, :D_IDX].astype(kidx_ref.dtype)
    off += AB_SMALL
    qkvz_ref[...] = _dot(x, w_ref[:, off:])


def _ab_in_proj(h2d, w_in, q_norm_g, kv_norm_g, kidx_g, kidx_b, tm=512):
    T, D = h2d.shape
    W = H_B * D_B
    o = [0, R_Q, R_Q + R_KV, R_Q + R_KV + D_IDX, R_Q + R_KV + D_IDX + H_IDX]
    o_q = o[4]
    o_b = o_q + 4 * W
    pad = AB_SMALL - (D_IDX + H_IDX + 2 * H_B)
    w_perm = jnp.concatenate([
        w_in[:, o[0]:o[2]],
        w_in[:, o[2]:o[4]], w_in[:, o_b:o_b + 2 * H_B],
        jnp.zeros((D, pad), w_in.dtype),
        w_in[:, o_q:o_b],
    ], axis=1).astype(BF16)
    n_all = w_perm.shape[1]
    gk = jnp.zeros((1, AB_SMALL), F32).at[0, :D_IDX].set(kidx_g)
    bk = jnp.zeros((1, AB_SMALL), F32).at[0, :D_IDX].set(kidx_b)
    row = lambda i: (i, 0)
    fixed = lambda i: (0, 0)
    return pl.pallas_call(
        _ab_proj_kernel,
        grid=(T // tm,),
        in_specs=[pl.BlockSpec((tm, D), row), pl.BlockSpec((D, n_all), fixed),
                  pl.BlockSpec((1, R_Q), fixed), pl.BlockSpec((1, R_KV), fixed),
                  pl.BlockSpec((1, AB_SMALL), fixed), pl.BlockSpec((1, AB_SMALL), fixed)],
        out_specs=[pl.BlockSpec((tm, R_Q), row), pl.BlockSpec((tm, R_KV), row),
                   pl.BlockSpec((tm, D_IDX), row), pl.BlockSpec((tm, AB_SMALL), row),
                   pl.BlockSpec((tm, 4 * W), row)],
        out_shape=[jax.ShapeDtypeStruct((T, R_Q), BF16), jax.ShapeDtypeStruct((T, R_KV), BF16),
                   jax.ShapeDtypeStruct((T, D_IDX), BF16), jax.ShapeDtypeStruct((T, AB_SMALL), F32),
                   jax.ShapeDtypeStruct((T, 4 * W), F32)],
        compiler_params=_cparams(("parallel",)),
        name="ab_in_proj",
    )(h2d, w_perm, q_norm_g.reshape(1, -1), kv_norm_g.reshape(1, -1), gk, bk)


def _sort_key(x):
    bits = pltpu.bitcast(x + 0.0, I32)
    return jnp.where(bits < 0, bits ^ 0x7FFFFFFF, bits)


def _dsa_kernel(cq_ref, ckv_ref, kidx_ref, widx_ref, wqidx_ref, wuq_ref, wukbd_ref, wuvbd_ref,
                o_ref, key_ref, qlat_ref, m_ref, l_ref, acc_ref, *, qb, kc, topk, seq_bits):
    i = pl.program_id(1)
    nck = ((i + 1) * qb + kc - 1) // kc
    n_lt = kc // 128
    cq = cq_ref[0]
    widx = widx_ref[0] * (H_IDX ** -0.5 * D_IDX ** -0.5)
    q_pos = i * qb + lax.broadcasted_iota(I32, (qb, 1), 0)

    qidx = [_dot(cq, wqidx_ref[h]).astype(BF16) for h in range(H_IDX)]

    def idx_body(c, carry):
        kblk = kidx_ref[0, pl.ds(pl.multiple_of(c * kc, kc), kc), :]
        isc = jnp.zeros((qb, kc), F32)
        for h in range(H_IDX):
            sc = _dot_nt(qidx[h], kblk)
            isc = isc + jnp.maximum(sc, 0.0) * widx[:, h:h + 1]
        k_pos = c * kc + lax.broadcasted_iota(I32, (1, kc), 1)
        key_ref[c] = jnp.where(k_pos <= q_pos, _sort_key(isc), KEY_NEG_INF)
        return carry

    lax.fori_loop(0, nck, idx_body, 0)

    def count_where(pred):
        def body(c, acc):
            k = key_ref[c]
            for j in range(n_lt):
                acc = acc + jnp.where(pred(k[:, j * 128:(j + 1) * 128], c, j), 1, 0)
            return acc
        acc = lax.fori_loop(0, nck, body, jnp.zeros((qb, 128), I32))
        return jnp.sum(acc, axis=1, keepdims=True)

    def bit_cond(carry):
        b, _, done = carry
        return (b < 32) & (jnp.min(done) == 0)

    def bit_body(carry):
        b, t, done = carry
        cand = t + (jnp.int32(1) << (31 - b))
        cnt = count_where(lambda k, c, j: k >= cand)
        t = jnp.where((cnt >= topk) & (done == 0), cand, t)
        done = jnp.where(cnt == topk, 1, done)
        return b + 1, t, done

    settled = jnp.where(q_pos + 1 <= topk, 1, 0)
    _, thr, done = lax.while_loop(bit_cond, bit_body,
                                  (jnp.int32(0), jnp.full((qb, 1), INT_MIN, I32), settled))
    thr = jnp.maximum(thr, KEY_NEG_INF + 1)

    def break_ties(need):
        def idx_of(c, j):
            return c * kc + j * 128 + lax.broadcasted_iota(I32, (1, 128), 1)

        def pos_body(b, m):
            cand = m + (jnp.int32(1) << (seq_bits - 1 - b))
            cnt = count_where(lambda k, c, j: (k == thr) & (idx_of(c, j) < cand))
            return jnp.where(cnt < need, cand, m)

        m = lax.fori_loop(0, seq_bits, pos_body, jnp.zeros((qb, 1), I32))

        def demote(c, carry):
            k = key_ref[c]
            idx = c * kc + lax.broadcasted_iota(I32, (1, kc), 1)
            key_ref[c] = jnp.where((k == thr) & (idx > m), thr - 1, k)
            return carry

        lax.fori_loop(0, nck, demote, 0)

    @pl.when(jnp.min(done) == 0)
    def _():
        c_gt = count_where(lambda k, c, j: k > thr)
        c_ge = count_where(lambda k, c, j: k >= thr)
        need = topk - c_gt

        @pl.when(jnp.max(c_ge - c_gt - need) > 0)
        def _():
            break_ties(need)

    q = _dot(cq, wuq_ref[...]).astype(BF16)
    qlat = _dot(q, wukbd_ref[...]) * (D_NOPE ** -0.5 * LOG2E)
    for h in range(H_A):
        qlat_ref[h * qb:(h + 1) * qb, :] = qlat[:, h * R_KV:(h + 1) * R_KV].astype(BF16)
    m_ref[...] = jnp.full(m_ref.shape, -1e30, F32)
    l_ref[...] = jnp.zeros(l_ref.shape, F32)
    acc_ref[...] = jnp.zeros(acc_ref.shape, F32)

    def att_body(c, carry):
        kv = ckv_ref[0, pl.ds(pl.multiple_of(c * kc, kc), kc), :]
        bias = jnp.where(key_ref[c] >= thr, 0.0, NEG_INF)
        s = _dot_nt(qlat_ref[...], kv)
        s = (s.reshape(H_A, qb, kc) + bias[None]).reshape(H_A * qb, kc)
        tiles = [s[:, j * 128:(j + 1) * 128] for j in range(n_lt)]
        m_cur = tiles[0]
        for t in tiles[1:]:
            m_cur = jnp.maximum(m_cur, t)
        m_old = m_ref[...]
        m_new = jnp.maximum(m_old, jnp.max(m_cur, axis=1, keepdims=True))
        ps = [jnp.exp2(t - m_new) for t in tiles]
        a = jnp.exp2(m_old - m_new)
        psum = ps[0]
        for t in ps[1:]:
            psum = psum + t
        l_ref[...] = a * l_ref[...] + psum
        p = jnp.concatenate([t.astype(BF16) for t in ps], axis=1)
        acc_ref[...] = a * acc_ref[...] + _dot(p, kv)
        m_ref[...] = m_new
        return carry

    lax.fori_loop(0, nck, att_body, 0)
    o_all = acc_ref[...] / jnp.sum(l_ref[...], axis=1, keepdims=True)
    o_lat = jnp.concatenate([o_all[h * qb:(h + 1) * qb, :] for h in range(H_A)], axis=1)
    o_ref[0] = _dot(o_lat.astype(BF16), wuvbd_ref[...]).astype(o_ref.dtype)


def _block_diag(blocks):
    n = len(blocks)
    r, c = blocks[0].shape
    out = jnp.zeros((n * r, n * c), blocks[0].dtype)
    for k, blk in enumerate(blocks):
        out = out.at[k * r:(k + 1) * r, k * c:(k + 1) * c].set(blk)
    return out


def _dsa(cq, ckv, kidx, widx, w_uq, w_uk, w_uv, w_qidx, qb=128, kc=512):
    B, S, _ = cq.shape
    kc = min(kc, S)
    topk = min(TOPK_MAX, S // 4)
    assert S % kc == 0 and kc % qb == 0 and kc >= topk
    seq_bits = max(1, (S - 1).bit_length())
    wqidx = jnp.transpose(w_qidx, (1, 0, 2)).astype(BF16)
    wuq = w_uq.reshape(R_Q, H_A * D_NOPE).astype(BF16)
    wukbd = _block_diag([w_uk[:, h, :].T for h in range(H_A)]).astype(BF16)
    wuvbd = _block_diag([w_uv[:, h, :] for h in range(H_A)]).astype(BF16)
    kern = functools.partial(_dsa_kernel, qb=qb, kc=kc, topk=topk, seq_bits=seq_bits)
    blk_q = lambda b, i: (b, i, 0)
    seq = lambda b, i: (b, 0, 0)
    fix2 = lambda b, i: (0, 0)
    fix3 = lambda b, i: (0, 0, 0)
    return pl.pallas_call(
        kern,
        grid=(B, S // qb),
        in_specs=[pl.BlockSpec((1, qb, R_Q), blk_q), pl.BlockSpec((1, S, R_KV), seq),
                  pl.BlockSpec((1, S, D_IDX), seq), pl.BlockSpec((1, qb, H_IDX), blk_q),
                  pl.BlockSpec(wqidx.shape, fix3), pl.BlockSpec(wuq.shape, fix2),
                  pl.BlockSpec(wukbd.shape, fix2), pl.BlockSpec(wuvbd.shape, fix2)],
        out_specs=pl.BlockSpec((1, qb, H_A * D_VA), blk_q),
        out_shape=jax.ShapeDtypeStruct((B, S, H_A * D_VA), BF16),
        scratch_shapes=[pltpu.VMEM((S // kc, qb, kc), I32),
                        pltpu.VMEM((H_A * qb, R_KV), BF16),
                        pltpu.VMEM((H_A * qb, R_KV), F32), pltpu.VMEM((H_A * qb, R_KV), F32),
                        pltpu.VMEM((H_A * qb, R_KV), F32)],
        compiler_params=_cparams(("parallel", "arbitrary")),
        name="dsa_attention",
    )(cq, ckv, kidx, widx, wqidx, wuq, wukbd, wuvbd)


GDN_HALO = 8


def _softplus(x):
    return jnp.maximum(x, 0.0) + jnp.log1p(jnp.exp(-jnp.abs(x)))


def _gdn_kernel(qkvz_ref, abc_ref, abr_ref, convw_ref, prm_c_ref, prm_r_ref, ng_ref, tri_ref,
                o_ref, xbuf_ref, state_ref, conv_ref, *, cb):
    C = DN_CHUNK
    W = H_B * D_B
    j = pl.program_id(1)

    @pl.when(j == 0)
    def _():
        xbuf_ref[0:GDN_HALO, :] = jnp.zeros((GDN_HALO, 3 * W), F32)
        state_ref[...] = jnp.zeros(state_ref.shape, F32)

    xbuf_ref[GDN_HALO:, :] = qkvz_ref[0, :, 0:3 * W]
    acc = xbuf_ref[GDN_HALO:, :] * convw_ref[CONV_K - 1:CONV_K, :]
    for t in range(CONV_K - 1):
        sh = CONV_K - 1 - t
        acc = acc + xbuf_ref[pl.ds(GDN_HALO - sh, cb), :] * convw_ref[t:t + 1, :]
    conv_ref[...] = _silu(acc)
    xbuf_ref[0:GDN_HALO, :] = xbuf_ref[cb:cb + GDN_HALO, :]

    abc = abc_ref[0]
    abr = abr_ref[0]
    beta_c = _sigmoid(abc)
    g_c = -jnp.exp(prm_r_ref[0:1, :]) * _softplus(abc + prm_r_ref[1:2, :])
    g_r = -jnp.exp(prm_c_ref[:, 0:1]) * _softplus(abr + prm_c_ref[:, 1:2])
    gc_c = _dot(tri_ref[...], g_c, precision=HIGHEST)
    gc_r = _dot_nt(g_r, tri_ref[...], precision=HIGHEST)

    ri = lax.broadcasted_iota(I32, (cb, cb), 0)
    ci = lax.broadcasted_iota(I32, (cb, cb), 1)
    same = (ri // C) == (ci // C)
    lower = same & (ri >= ci)
    strict = same & (ri > ci)
    eye = jnp.where(ri == ci, 1.0, 0.0)
    bf = lambda t: t.astype(BF16)

    for h in range(H_B):
        q = conv_ref[:, h * D_B:(h + 1) * D_B]
        k = conv_ref[:, W + h * D_B:W + (h + 1) * D_B]
        v = conv_ref[:, 2 * W + h * D_B:2 * W + (h + 1) * D_B]
        q = q * lax.rsqrt(jnp.sum(q * q, axis=-1, keepdims=True) + 1e-6) * (D_B ** -0.5)
        k = k * lax.rsqrt(jnp.sum(k * k, axis=-1, keepdims=True) + 1e-6)
        beta = beta_c[:, h:h + 1]
        gcol = gc_c[:, H_B + h:H_B + h + 1]
        grow = gc_r[H_B + h:H_B + h + 1, :]
        decay = jnp.exp(jnp.where(lower, gcol - grow, NEG_INF))
        kb = k * beta
        vb = v * beta
        kq = bf(k)
        a_mat = jnp.where(strict, _dot_nt(bf(kb), kq) * decay, 0.0)
        xm = -a_mat
        t_mat = eye + xm
        for _ in range(int(math.log2(C)) - 1):
            xq = bf(xm)
            xm = _dot(xq, xq)
            t_mat = t_mat + _dot(bf(t_mat), bf(xm))
        egc = jnp.exp(gcol)
        tq = bf(t_mat)
        u = _dot(tq, bf(vb))
        w = bf(_dot(tq, bf(kb * egc)))
        qk = bf(jnp.where(lower, _dot_nt(bf(q), kq) * decay, 0.0))
        q_dec = bf(q * egc)
        for n in range(cb // C):
            r0 = n * C
            glast = gcol[r0 + C - 1:r0 + C, :]
            k_dec = bf(k[r0:r0 + C] * jnp.exp(glast - gcol[r0:r0 + C]))
            st = state_ref[h]
            stq = bf(st)
            v_new = u[r0:r0 + C] - _dot(w[r0:r0 + C], stq)
            vq = bf(v_new)
            o = _dot(q_dec[r0:r0 + C], stq) + _dot(qk[r0:r0 + C, r0:r0 + C], vq)
            state_ref[h] = st * jnp.exp(glast) + _dot_tn(k_dec, vq)
            o = o * lax.rsqrt(jnp.mean(o * o, axis=-1, keepdims=True) + 1e-6) * ng_ref[...]
            z = qkvz_ref[0, r0:r0 + C, 3 * W + h * D_B:3 * W + (h + 1) * D_B]
            o_ref[0, r0:r0 + C, h * D_B:(h + 1) * D_B] = (o * _silu(z)).astype(o_ref.dtype)


def _gdn(qkvz, b, a, conv_w, a_log, dt_bias, norm_g, cb=256):
    B, S, _ = qkvz.shape
    W = H_B * D_B
    cb = min(cb, S)
    abc = jnp.concatenate([b, a], axis=-1)
    abr = jnp.transpose(abc, (0, 2, 1))
    zeros = jnp.zeros((H_B,), F32)
    prm = jnp.stack([jnp.concatenate([zeros, a_log]), jnp.concatenate([zeros, dt_bias])])
    idx = jnp.arange(cb)
    tri = ((idx[:, None] >= idx[None, :]) & (idx[:, None] // DN_CHUNK == idx[None, :] // DN_CHUNK)).astype(F32)
    kern = functools.partial(_gdn_kernel, cb=cb)
    blk = lambda bi, j: (bi, j, 0)
    fix = lambda bi, j: (0, 0)
    return pl.pallas_call(
        kern,
        grid=(B, S // cb),
        in_specs=[pl.BlockSpec((1, cb, 4 * W), blk), pl.BlockSpec((1, cb, 2 * H_B), blk),
                  pl.BlockSpec((1, 2 * H_B, cb), lambda bi, j: (bi, 0, j)),
                  pl.BlockSpec((CONV_K, 3 * W), fix), pl.BlockSpec((2 * H_B, 2), fix),
                  pl.BlockSpec((2, 2 * H_B), fix), pl.BlockSpec((1, D_B), fix),
                  pl.BlockSpec((cb, cb), fix)],
        out_specs=pl.BlockSpec((1, cb, W), blk),
        out_shape=jax.ShapeDtypeStruct((B, S, W), BF16),
        scratch_shapes=[pltpu.VMEM((cb + GDN_HALO, 3 * W), F32), pltpu.VMEM((H_B, D_B, D_B), F32),
                        pltpu.VMEM((cb, 3 * W), F32)],
        compiler_params=_cparams(("parallel", "arbitrary")),
        name="gated_deltanet",
    )(qkvz, abc, abr, conv_w, prm.T, prm, norm_g.reshape(1, D_B), tri)


def _post_mixer_kernel(a1_ref, a2_ref, h_ref, w1_ref, w2_ref, g_ref, b_ref, rw_ref, rb_ref, triu_ref,
                       h1_ref, tope_ref, gate_ref, rank_ref, cnt_ref, carry_ref):
    i = pl.program_id(0)
    E, tm = rw_ref.shape[0], h_ref.shape[0]

    @pl.when(i == 0)
    def _():
        carry_ref[...] = jnp.zeros(carry_ref.shape, F32)

    mix = _dot(a1_ref[...].astype(BF16), w1_ref[...]) + _dot(a2_ref[...].astype(BF16), w2_ref[...])
    h1 = _layernorm_rows(ALPHA * h_ref[...] + mix, g_ref[...], b_ref[...])
    h1_ref[...] = h1

    logits = _dot_nt(rw_ref[...], h1, precision=HIGHEST) + rb_ref[...]
    erow = lax.broadcasted_iota(I32, (E, tm), 0)
    sel = jnp.zeros((E, tm), F32)
    onehots, tops = [], []
    for k in range(TOP_K):
        mx = jnp.max(logits, axis=0, keepdims=True)
        idx = jnp.min(jnp.where(logits == mx, erow, E), axis=0, keepdims=True)
        oh = erow == idx
        logits = jnp.where(oh, NEG_INF, logits)
        sel = sel + jnp.where(oh, 1.0, 0.0)
        onehots.append(oh)
        tops.append(mx)
        tope_ref[k:k + 1, :] = idx
    exps = [jnp.exp(t - tops[0]) for t in tops]
    den = exps[0] + exps[1] + exps[2] + exps[3]
    for k in range(TOP_K):
        gate_ref[k:k + 1, :] = exps[k] / den
    incl = _dot(sel.astype(BF16), triu_ref[...])
    excl = incl - sel + carry_ref[:, 0:1]
    for k in range(TOP_K):
        rank_ref[k:k + 1, :] = jnp.sum(jnp.where(onehots[k], excl, 0.0), axis=0, keepdims=True).astype(I32)
    carry_ref[...] = carry_ref[...] + jnp.sum(sel, axis=1, keepdims=True)
    cnt_ref[...] = carry_ref[...]


def _post_mixer(a1, a2, h2d, w_out, ln_g, ln_b, router_w, router_b, tm=512):
    T, D = h2d.shape
    E = router_w.shape[1]
    n1, n2 = a1.shape[1], a2.shape[1]
    w1 = w_out[:n1].astype(BF16)
    w2 = w_out[n1:].astype(BF16)
    idx = jnp.arange(tm)
    triu = (idx[:, None] <= idx[None, :]).astype(BF16)
    row = lambda i: (i, 0)
    col = lambda i: (0, i)
    fix = lambda i: (0, 0)
    return pl.pallas_call(
        _post_mixer_kernel,
        grid=(T // tm,),
        in_specs=[pl.BlockSpec((tm, n1), row), pl.BlockSpec((tm, n2), row), pl.BlockSpec((tm, D), row),
                  pl.BlockSpec((n1, D), fix), pl.BlockSpec((n2, D), fix),
                  pl.BlockSpec((1, D), fix), pl.BlockSpec((1, D), fix),
                  pl.BlockSpec((E, D), fix), pl.BlockSpec((E, 1), fix), pl.BlockSpec((tm, tm), fix)],
        out_specs=[pl.BlockSpec((tm, D), row),
                   pl.BlockSpec((TOP_K, tm), col), pl.BlockSpec((TOP_K, tm), col),
                   pl.BlockSpec((TOP_K, tm), col), pl.BlockSpec((E, 128), fix)],
        out_shape=[jax.ShapeDtypeStruct((T, D), F32),
                   jax.ShapeDtypeStruct((TOP_K, T), I32), jax.ShapeDtypeStruct((TOP_K, T), F32),
                   jax.ShapeDtypeStruct((TOP_K, T), I32), jax.ShapeDtypeStruct((E, 128), F32)],
        scratch_shapes=[pltpu.VMEM((E, 128), F32)],
        compiler_params=_cparams(("arbitrary",)),
        name="post_mixer_router",
    )(a1, a2, h2d, w1, w2, ln_g.reshape(1, D), ln_b.reshape(1, D), router_w.T, router_b.reshape(E, 1), triu)


def _dispatch_kernel(dest_ref, h_ref, xs_init_hbm, xs_hbm, sem, *, tt, n_tok):
    del xs_init_hbm
    base = pl.program_id(0) * tt

    def row_copy(r, k):
        d = dest_ref[k * n_tok + base + r]
        return pltpu.make_async_copy(h_ref.at[r], xs_hbm.at[d], sem)

    def start(r, c):
        for k in range(TOP_K):
            row_copy(r, k).start(priority=k % 2)
        return c

    def wait(r, c):
        for k in range(TOP_K):
            row_copy(r, k).wait()
        return c

    lax.fori_loop(0, tt, start, 0)
    lax.fori_loop(0, tt, wait, 0)


def _as_row_tiles(a):
    return a.reshape(a.shape[0], a.shape[1] // LANES, LANES)


def _moe_dispatch(h1, dest_flat, n_slots, tt=512):
    T, D = h1.shape
    sub = D // LANES
    any_spec = pl.BlockSpec(memory_space=pl.ANY)
    xs = pl.pallas_call(
        functools.partial(_dispatch_kernel, tt=tt, n_tok=T),
        grid_spec=pltpu.PrefetchScalarGridSpec(
            num_scalar_prefetch=1, grid=(T // tt,),
            in_specs=[pl.BlockSpec((tt, sub, LANES), lambda i, d: (i, 0, 0)), any_spec], out_specs=any_spec,
            scratch_shapes=[pltpu.SemaphoreType.DMA(())]),
        out_shape=jax.ShapeDtypeStruct((n_slots, sub, LANES), h1.dtype),
        input_output_aliases={2: 0},
        compiler_params=_cparams(("arbitrary",)),
        name="moe_dispatch",
    )(dest_flat, _as_row_tiles(h1), jnp.zeros((n_slots, sub, LANES), h1.dtype))
    return xs.reshape(n_slots, D)


MOE_BLOCK = 256


def _moe_kernel(be_ref, nu_ref, x_ref, wg_ref, wu_ref, bg_ref, bu_ref, wd_ref, bd_ref, y_ref):
    i = pl.program_id(0)

    @pl.when(i < nu_ref[0])
    def _():
        x = x_ref[...].astype(BF16)
        g = _dot_nt(x, wg_ref[0]) + bg_ref[0]
        u = _dot_nt(x, wu_ref[0]) + bu_ref[0]
        gt = jnp.minimum(g, SWIGLU_LIMIT)
        up = jnp.clip(u, -SWIGLU_LIMIT, SWIGLU_LIMIT)
        hid = (up + 1.0) * (gt * _sigmoid(gt * SWIGLU_ALPHA))
        y_ref[...] = _dot(hid.astype(BF16), wd_ref[0]) + bd_ref[0]

    @pl.when(i >= nu_ref[0])
    def _():
        y_ref[...] = jnp.zeros(y_ref.shape, y_ref.dtype)


def _moe_experts(xs, blk_e, n_used, wgu_t, bg, bu, wd, bd):
    P, D = xs.shape
    F = wgu_t.shape[1]
    bm = MOE_BLOCK
    wsel = lambda i, be, nu: (be[i], 0, 0)
    wsel_up = lambda i, be, nu: (be[i], 0, 1)
    row = lambda i, be, nu: (i, 0)
    return pl.pallas_call(
        _moe_kernel,
        grid_spec=pltpu.PrefetchScalarGridSpec(
            num_scalar_prefetch=2,
            grid=(P // bm,),
            in_specs=[pl.BlockSpec((bm, D), row),
                      pl.BlockSpec((1, F, D), wsel), pl.BlockSpec((1, F, D), wsel_up),
                      pl.BlockSpec((1, 1, F), wsel), pl.BlockSpec((1, 1, F), wsel),
                      pl.BlockSpec((1, F, D), wsel), pl.BlockSpec((1, 1, D), wsel)],
            out_specs=pl.BlockSpec((bm, D), row)),
        out_shape=jax.ShapeDtypeStruct((P, D), F32),
        compiler_params=_cparams(("arbitrary",)),
        name="moe_experts",
    )(blk_e, n_used, xs, wgu_t, wgu_t, bg, bu, wd, bd)


def _moe(h1, top_e, rank, cnt, w_gu, b_gu, w_down, b_down):
    T, D = h1.shape
    E = w_gu.shape[0]
    bm = MOE_BLOCK
    counts = cnt[:, 0].astype(I32)
    padded = (counts + bm - 1) // bm * bm
    pad_end = jnp.cumsum(padded)
    pad_start = pad_end - padded
    e_ids = jnp.arange(E, dtype=I32)[:, None, None]
    start_of = jnp.sum(jnp.where(top_e[None] == e_ids, pad_start[:, None, None], 0), axis=0)
    dest = (start_of + rank).reshape(-1)
    nblk = (T * TOP_K) // bm + E
    blk_first = jnp.arange(nblk, dtype=I32) * bm
    blk_e = jnp.minimum(jnp.sum(pad_end[None, :] <= blk_first[:, None], axis=1), E - 1).astype(I32)
    n_used = (pad_end[-1:] // bm).astype(I32)
    xs = _moe_dispatch(h1, dest, nblk * bm)
    F = w_down.shape[1]
    wgu_t = jnp.transpose(w_gu, (0, 2, 1)).astype(BF16).reshape(E, F, 2 * D)
    ys = _moe_experts(xs, blk_e, n_used, wgu_t, b_gu[:, None, 0::2], b_gu[:, None, 1::2],
                      w_down.astype(BF16), b_down[:, None, :])
    return ys, dest


def _tail_kernel(dest_ref, h1_ref, gate_ref, p_ref, wg_ref, wp_ref, g_ref, b_ref, ys_hbm, o_ref,
                 ybuf, sem, *, tm, n_tok, sub):
    i = pl.program_id(0)

    def row_copy(tile, slot, r, k):
        d = dest_ref[k * n_tok + tile * tm + r]
        return pltpu.make_async_copy(ys_hbm.at[d], ybuf.at[slot, k, pl.ds(r * sub, sub)], sem.at[slot])

    def start_tile(tile, slot):
        def body(r, c):
            for k in range(TOP_K):
                row_copy(tile, slot, r, k).start(priority=k % 2)
            return c
        lax.fori_loop(0, tm, body, 0)

    def wait_tile(tile, slot):
        def body(r, c):
            for k in range(TOP_K):
                row_copy(tile, slot, r, k).wait()
            return c
        lax.fori_loop(0, tm, body, 0)

    @pl.when(i == 0)
    def _():
        start_tile(0, 0)

    h1 = h1_ref[...]
    ple = _sigmoid(_dot(h1.astype(BF16), wg_ref[...])) * _dot(p_ref[...].astype(BF16), wp_ref[...])
    x = ALPHA * h1 + ple

    for slot in range(2):
        @pl.when(i % 2 == slot)
        def _(slot=slot):
            @pl.when(i + 1 < pl.num_programs(0))
            def _():
                start_tile(i + 1, 1 - slot)

            wait_tile(i, slot)
            cols = []
            for j in range(sub):
                col = ybuf[slot, 0, pl.ds(j, tm, stride=sub), :] * gate_ref[:, 0:1]
                for k in range(1, TOP_K):
                    col = col + ybuf[slot, k, pl.ds(j, tm, stride=sub), :] * gate_ref[:, k:k + 1]
                cols.append(col)
            ffn = jnp.concatenate(cols, axis=1)
            o_ref[...] = _layernorm_rows(x + ffn, g_ref[...], b_ref[...])


def _layer_tail(h1, ys, dest_flat, gate_t, p2d, ple_w_gate, ple_w_proj, ln_g, ln_b, tm=256):
    T, D = h1.shape
    PD = p2d.shape[1]
    sub = D // LANES
    row = lambda i, d: (i, 0)
    fix = lambda i, d: (0, 0)
    return pl.pallas_call(
        functools.partial(_tail_kernel, tm=tm, n_tok=T, sub=sub),
        grid_spec=pltpu.PrefetchScalarGridSpec(
            num_scalar_prefetch=1, grid=(T // tm,),
            in_specs=[pl.BlockSpec((tm, D), row), pl.BlockSpec((tm, TOP_K), row),
                      pl.BlockSpec((tm, PD), row), pl.BlockSpec((D, D), fix), pl.BlockSpec((PD, D), fix),
                      pl.BlockSpec((1, D), fix), pl.BlockSpec((1, D), fix),
                      pl.BlockSpec(memory_space=pl.ANY)],
            out_specs=pl.BlockSpec((tm, D), row),
            scratch_shapes=[pltpu.VMEM((2, TOP_K, tm * sub, LANES), F32), pltpu.SemaphoreType.DMA((2,))]),
        out_shape=jax.ShapeDtypeStruct((T, D), F32),
        compiler_params=_cparams(("arbitrary",)),
        name="layer_tail",
    )(dest_flat, h1, gate_t.T, p2d, ple_w_gate.astype(BF16), ple_w_proj.astype(BF16),
      ln_g.reshape(1, D), ln_b.reshape(1, D), _as_row_tiles(ys))


def _cd_proj_kernel(x_ref, w_ref, q_ref, k_ref, v_ref, u_ref):
    x = x_ref[...].astype(BF16)
    n = H_C * DH_C
    q_ref[...] = (_dot(x, w_ref[:, 0:n]) * (DH_C ** -0.5 * LOG2E)).astype(q_ref.dtype)
    k_ref[...] = _dot(x, w_ref[:, n:2 * n]).astype(k_ref.dtype)
    v_ref[...] = _dot(x, w_ref[:, 2 * n:3 * n]).astype(v_ref.dtype)
    u_ref[...] = _dot(x, w_ref[:, 3 * n:])


def _cd_in_proj(h2d, w_in, tm=512):
    T, D = h2d.shape
    n = H_C * DH_C
    nu = w_in.shape[1] - 3 * n
    row = lambda i: (i, 0)
    return pl.pallas_call(
        _cd_proj_kernel,
        grid=(T // tm,),
        in_specs=[pl.BlockSpec((tm, D), row), pl.BlockSpec(w_in.shape, lambda i: (0, 0))],
        out_specs=[pl.BlockSpec((tm, n), row)] * 3 + [pl.BlockSpec((tm, nu), row)],
        out_shape=[jax.ShapeDtypeStruct((T, n), BF16)] * 3 + [jax.ShapeDtypeStruct((T, nu), F32)],
        compiler_params=_cparams(("parallel",)),
        name="cd_in_proj",
    )(h2d, w_in.astype(BF16))


def _dilated_bias_table(qb):
    import numpy as np
    max_w = max(w for w, _ in DILATED_PATTERNS)
    ndc = max_w // qb + 1
    r = np.arange(qb)[:, None]
    j = np.arange(qb)[None, :]
    tbl = np.empty((ndc, qb, qb), np.float32)
    for dc in range(ndc):
        delta = dc * qb + r - j
        mult = np.zeros((qb, qb), np.float64)
        for w, d in DILATED_PATTERNS:
            mult += (delta >= 0) & (delta <= w) & (delta % d == 0)
        with np.errstate(divide="ignore"):
            tbl[dc] = np.log2(mult)
    return jnp.asarray(tbl)


def _dilated_kernel(q_ref, k_ref, v_ref, bias_ref, o_ref, q2_ref, m_ref, l_ref, acc_ref, *, qb, ndc):
    i = pl.program_id(2)
    n_lt = qb // 128
    q = q_ref[0]
    lo = lax.broadcasted_iota(I32, q.shape, 1) < DH_C
    zero = jnp.zeros(q.shape, q.dtype)
    q2_ref[0:qb, :] = jnp.where(lo, q, zero)
    q2_ref[qb:, :] = jnp.where(lo, zero, q)
    m_ref[...] = jnp.full(m_ref.shape, -1e30, F32)
    l_ref[...] = jnp.zeros(l_ref.shape, F32)
    acc_ref[...] = jnp.zeros(acc_ref.shape, F32)

    def body(c, carry):
        r0 = pl.multiple_of(c * qb, qb)
        kk = k_ref[0, pl.ds(r0, qb), :]
        vv = v_ref[0, pl.ds(r0, qb), :]
        s = _dot_nt(q2_ref[...], kk)
        s = (s.reshape(2, qb, qb) + bias_ref[i - c][None]).reshape(2 * qb, qb)
        tiles = [s[:, j * 128:(j + 1) * 128] for j in range(n_lt)]
        m_cur = tiles[0]
        for t in tiles[1:]:
            m_cur = jnp.maximum(m_cur, t)
        m_old = m_ref[...]
        m_new = jnp.maximum(m_old, jnp.max(m_cur, axis=1, keepdims=True))
        ps = [jnp.exp2(t - m_new) for t in tiles]
        a = jnp.exp2(m_old - m_new)
        psum = ps[0]
        for t in ps[1:]:
            psum = psum + t
        l_ref[...] = a * l_ref[...] + psum
        p = jnp.concatenate([t.astype(BF16) for t in ps], axis=1)
        acc_ref[...] = a * acc_ref[...] + _dot(p, vv)
        m_ref[...] = m_new
        return carry

    lax.fori_loop(jnp.maximum(i - (ndc - 1), 0), i + 1, body, 0)
    o_all = acc_ref[...] / jnp.sum(l_ref[...], axis=1, keepdims=True)
    o_ref[0] = jnp.where(lo, o_all[0:qb, :], o_all[qb:, :]).astype(o_ref.dtype)


def _dilated_attention(q, k, v, qb=512):
    B, S, n = q.shape
    qb = min(qb, S)
    bias = _dilated_bias_table(qb)
    ndc = bias.shape[0]
    pw = 2 * DH_C
    kern = functools.partial(_dilated_kernel, qb=qb, ndc=ndc)
    blk = lambda b, pr, i: (b, i, pr)
    seq = lambda b, pr, i: (b, 0, pr)
    return pl.pallas_call(
        kern,
        grid=(B, n // pw, S // qb),
        in_specs=[pl.BlockSpec((1, qb, pw), blk), pl.BlockSpec((1, S, pw), seq),
                  pl.BlockSpec((1, S, pw), seq), pl.BlockSpec(bias.shape, lambda b, pr, i: (0, 0, 0))],
        out_specs=pl.BlockSpec((1, qb, pw), blk),
        out_shape=jax.ShapeDtypeStruct((B, S, n), BF16),
        scratch_shapes=[pltpu.VMEM((2 * qb, pw), BF16), pltpu.VMEM((2 * qb, pw), F32),
                        pltpu.VMEM((2 * qb, pw), F32), pltpu.VMEM((2 * qb, pw), F32)],
        compiler_params=_cparams(("parallel", "parallel", "arbitrary")),
        name="dilated_attention",
    )(q, k, v, bias)


POOL_HALO = 16


def _pool_kernel(u_ref, w_ref, sc_ref, o_ref, xbuf_ref, *, tm):
    j = pl.program_id(1)

    @pl.when(j == 0)
    def _():
        xbuf_ref[0:POOL_HALO, :] = jnp.zeros((POOL_HALO, xbuf_ref.shape[1]), F32)

    xbuf_ref[POOL_HALO:, :] = u_ref[0]
    x = xbuf_ref[POOL_HALO:, :]
    grp = lax.broadcasted_iota(I32, (1, x.shape[1]), 1) // POOL_GROUP
    run = x
    sel = jnp.zeros(x.shape, F32)
    win = jnp.zeros((1, x.shape[1]), F32)
    for d in range(1, max(POOL_WINDOWS)):
        run = run + xbuf_ref[pl.ds(POOL_HALO - d, tm), :]
        if d + 1 in POOL_WINDOWS:
            gi = POOL_WINDOWS.index(d + 1)
            sel = jnp.where(grp == gi, run, sel)
            win = jnp.where(grp == gi, float(d + 1), win)
    pos = j * tm + lax.broadcasted_iota(I32, (tm, 1), 0)
    mean = sel / jnp.minimum((pos + 1).astype(F32), win)
    o_ref[0] = _dot((mean - x).astype(BF16), w_ref[...]) * sc_ref[...]
    xbuf_ref[0:POOL_HALO, :] = xbuf_ref[tm:tm + POOL_HALO, :]


def _multiscale_pool(u, pool_w, pool_scale, tm=512):
    B, S, n = u.shape
    tm = min(tm, S)
    wbd = _block_diag([pool_w[g] for g in range(pool_w.shape[0])]).astype(BF16)
    blk = lambda b, j: (b, j, 0)
    fix = lambda b, j: (0, 0)
    return pl.pallas_call(
        functools.partial(_pool_kernel, tm=tm),
        grid=(B, S // tm),
        in_specs=[pl.BlockSpec((1, tm, n), blk), pl.BlockSpec((n, n), fix), pl.BlockSpec((1, n), fix)],
        out_specs=pl.BlockSpec((1, tm, n), blk),
        out_shape=jax.ShapeDtypeStruct((B, S, n), F32),
        scratch_shapes=[pltpu.VMEM((tm + POOL_HALO, n), F32)],
        compiler_params=_cparams(("parallel", "arbitrary")),
        name="multiscale_pool",
    )(u, wbd, pool_scale.reshape(1, n))


def kernel(x, p, ab_w_in, ab_q_norm_g, ab_kv_norm_g, ab_w_uq, ab_w_uk, ab_w_uv, ab_w_qidx,
           ab_kidx_norm_g, ab_kidx_norm_b, ab_conv_w, ab_a_log, ab_dt_bias, ab_out_norm_g, ab_w_out,
           cd_w_in, cd_pool_w, cd_pool_scale, cd_w_out, ln_mix_g, ln_mix_b, router_w, router_b,
           w_gu, b_gu, w_down, b_down, ple_w_proj, ple_w_gate, ln_ffn_g, ln_ffn_b):
    B, S, D = x.shape
    T = B * S
    h = x.reshape(T, D)
    for i in range(DEPTH):
        j = i // 2
        if i % 2 == 0:
            cq, ckv, kidx, small, qkvz = _ab_in_proj(h, ab_w_in[j], ab_q_norm_g[j], ab_kv_norm_g[j],
                                                     ab_kidx_norm_g[j], ab_kidx_norm_b[j])
            sh = lambda t: t.reshape(B, S, -1)
            o_w = D_IDX + H_IDX
            o_a = _dsa(sh(cq), sh(ckv), sh(kidx), sh(small[:, D_IDX:o_w]),
                       ab_w_uq[j], ab_w_uk[j], ab_w_uv[j], ab_w_qidx[j])
            o_b = _gdn(sh(qkvz), sh(small[:, o_w:o_w + H_B]), sh(small[:, o_w + H_B:o_w + 2 * H_B]),
                       ab_conv_w[j], ab_a_log[j], ab_dt_bias[j], ab_out_norm_g[j])
            a1, a2, w_out = o_a.reshape(T, -1), o_b.reshape(T, -1), ab_w_out[j]
        else:
            q, k, v, u = _cd_in_proj(h, cd_w_in[j])
            sh = lambda t: t.reshape(B, S, -1)
            o_c = _dilated_attention(sh(q), sh(k), sh(v))
            o_d = _multiscale_pool(sh(u), cd_pool_w[j], cd_pool_scale[j])
            a1, a2, w_out = o_c.reshape(T, -1), o_d.reshape(T, -1), cd_w_out[j]
        h1, top_e, gate_t, rank, cnt = _post_mixer(a1, a2, h, w_out, ln_mix_g[i], ln_mix_b[i],
                                                   router_w[i], router_b[i])
        ys, dest = _moe(h1, top_e, rank, cnt, w_gu[i], b_gu[i], w_down[i], b_down[i])
        h = _layer_tail(h1, ys, dest, gate_t, p[i].reshape(T, -1), ple_w_gate[i], ple_w_proj[i],
                        ln_ffn_g[i], ln_ffn_b[i])
    return h.reshape(B, S, D)
```

```python
import functools
import math

import jax
import jax.numpy as jnp
from jax import lax
from jax.experimental import pallas as pl
from jax.experimental.pallas import tpu as pltpu

F32 = jnp.float32
BF16 = jnp.bfloat16
I32 = jnp.int32

DEPTH = 2
H_A, D_NOPE, D_VA, R_Q, R_KV, H_IDX, D_IDX = 8, 64, 64, 256, 128, 8, 64
TOPK_MAX = 256
H_B, D_B, CONV_K, DN_CHUNK = 4, 128, 4, 64
H_C, DH_C = 12, 64
DILATED_PATTERNS = ((128, 1), (512, 4), (2048, 16))
POOL_WINDOWS = (2, 4, 8, 16)
POOL_GROUP = 64
N_EXPERTS, TOP_K = 32, 4
SWIGLU_LIMIT, SWIGLU_ALPHA = 7.0, 1.702
ALPHA = (2 * DEPTH) ** 0.25

VMEM_LIMIT_BYTES = 56 * 1024 * 1024
LANES = 128
HIGHEST = lax.Precision.HIGHEST
NEG_INF = float("-inf")
LOG2E = math.log2(math.e)
INT_MIN = -2 ** 31
KEY_NEG_INF = (0xFF800000 ^ 0x7FFFFFFF) - 2 ** 32


def _cparams(sem):
    return pltpu.CompilerParams(dimension_semantics=sem, vmem_limit_bytes=VMEM_LIMIT_BYTES)


def _dot(a, b, precision=None):
    return jnp.dot(a, b, preferred_element_type=F32, precision=precision)


def _dot_nt(a, b, precision=None):
    return lax.dot_general(a, b, (((1,), (1,)), ((), ())), preferred_element_type=F32,
                           precision=precision)


def _dot_tn(a, b, precision=None):
    return lax.dot_general(a, b, (((0,), (0,)), ((), ())), preferred_element_type=F32,
                           precision=precision)


def _sigmoid(x):
    return 1.0 / (1.0 + jnp.exp(-x))


def _silu(x):
    return x * _sigmoid(x)


def _layernorm_rows(x, g, b, eps=1e-5):
    mu = jnp.mean(x, axis=-1, keepdims=True)
    xc = x - mu
    var = jnp.mean(xc * xc, axis=-1, keepdims=True)
    return xc * lax.rsqrt(var + eps) * g + b


AB_SMALL = 128


def _ab_proj_kernel(x_ref, w_ref, gq_ref, gkv_ref, gk_ref, bk_ref,
                    cq_ref, ckv_ref, kidx_ref, small_ref, qkvz_ref):
    x = x_ref[...].astype(BF16)
    cq = _dot(x, w_ref[:, 0:R_Q])
    cq = cq * lax.rsqrt(jnp.mean(cq * cq, axis=-1, keepdims=True) + 1e-6) * gq_ref[...]
    cq_ref[...] = cq.astype(cq_ref.dtype)
    ckv = _dot(x, w_ref[:, R_Q:R_Q + R_KV])
    ckv = ckv * lax.rsqrt(jnp.mean(ckv * ckv, axis=-1, keepdims=True) + 1e-6) * gkv_ref[...]
    ckv_ref[...] = ckv.astype(ckv_ref.dtype)
    off = R_Q + R_KV
    sm = _dot(x, w_ref[:, off:off + AB_SMALL])
    small_ref[...] = sm
    lane = lax.broadcasted_iota(I32, sm.shape, 1)
    is_k = lane < D_IDX
    mu = jnp.sum(jnp.where(is_k, sm, 0.0), axis=-1, keepdims=True) * (1.0 / D_IDX)
    xc = jnp.where(is_k, sm - mu, 0.0)
    var = jnp.sum(xc * xc, axis=-1, keepdims=True) * (1.0 / D_IDX)
    kn = xc * lax.rsqrt(var + 1e-5) * gk_ref[...] + bk_ref[...]
    kidx_ref[...] = kn[:, :D_IDX].astype(kidx_ref.dtype)
    off += AB_SMALL
    qkvz_ref[...] = _dot(x, w_ref[:, off:])


def _ab_in_proj(h2d, w_in, q_norm_g, kv_norm_g, kidx_g, kidx_b, tm=512):
    T, D = h2d.shape
    W = H_B * D_B
    o = [0, R_Q, R_Q + R_KV, R_Q + R_KV + D_IDX, R_Q + R_KV + D_IDX + H_IDX]
    o_q = o[4]
    o_b = o_q + 4 * W
    pad = AB_SMALL - (D_IDX + H_IDX + 2 * H_B)
    w_perm = jnp.concatenate([
        w_in[:, o[0]:o[2]],
        w_in[:, o[2]:o[4]], w_in[:, o_b:o_b + 2 * H_B],
        jnp.zeros((D, pad), w_in.dtype),
        w_in[:, o_q:o_b],
    ], axis=1).astype(BF16)
    n_all = w_perm.shape[1]
    gk = jnp.zeros((1, AB_SMALL), F32).at[0, :D_IDX].set(kidx_g)
    bk = jnp.zeros((1, AB_SMALL), F32).at[0, :D_IDX].set(kidx_b)
    row = lambda i: (i, 0)
    fixed = lambda i: (0, 0)
    return pl.pallas_call(
        _ab_proj_kernel,
        grid=(T // tm,),
        in_specs=[pl.BlockSpec((tm, D), row), pl.BlockSpec((D, n_all), fixed),
                  pl.BlockSpec((1, R_Q), fixed), pl.BlockSpec((1, R_KV), fixed),
                  pl.BlockSpec((1, AB_SMALL), fixed), pl.BlockSpec((1, AB_SMALL), fixed)],
        out_specs=[pl.BlockSpec((tm, R_Q), row), pl.BlockSpec((tm, R_KV), row),
                   pl.BlockSpec((tm, D_IDX), row), pl.BlockSpec((tm, AB_SMALL), row),
                   pl.BlockSpec((tm, 4 * W), row)],
        out_shape=[jax.ShapeDtypeStruct((T, R_Q), BF16), jax.ShapeDtypeStruct((T, R_KV), BF16),
                   jax.ShapeDtypeStruct((T, D_IDX), BF16), jax.ShapeDtypeStruct((T, AB_SMALL), F32),
                   jax.ShapeDtypeStruct((T, 4 * W), F32)],
        compiler_params=_cparams(("parallel",)),
        name="ab_in_proj",
    )(h2d, w_perm, q_norm_g.reshape(1, -1), kv_norm_g.reshape(1, -1), gk, bk)


def _sort_key(x):
    bits = pltpu.bitcast(x + 0.0, I32)
    return jnp.where(bits < 0, bits ^ 0x7FFFFFFF, bits)


def _dsa_kernel(cq_ref, ckv_ref, kidx_ref, widx_ref, wqidx_ref, wuq_ref, wukbd_ref, wuvbd_ref,
                o_ref, key_ref, qlat_ref, m_ref, l_ref, acc_ref, *, qb, kc, topk, seq_bits):
    i = pl.program_id(1)
    nck = ((i + 1) * qb + kc - 1) // kc
    n_lt = kc // 128
    cq = cq_ref[0]
    widx = widx_ref[0] * (H_IDX ** -0.5 * D_IDX ** -0.5)
    q_pos = i * qb + lax.broadcasted_iota(I32, (qb, 1), 0)

    qidx = [_dot(cq, wqidx_ref[h]).astype(BF16) for h in range(H_IDX)]

    def idx_body(c, carry):
        kblk = kidx_ref[0, pl.ds(pl.multiple_of(c * kc, kc), kc), :]
        isc = jnp.zeros((qb, kc), F32)
        for h in range(H_IDX):
            sc = _dot_nt(qidx[h], kblk)
            isc = isc + jnp.maximum(sc, 0.0) * widx[:, h:h + 1]
        k_pos = c * kc + lax.broadcasted_iota(I32, (1, kc), 1)
        key_ref[c] = jnp.where(k_pos <= q_pos, _sort_key(isc), KEY_NEG_INF)
        return carry

    lax.fori_loop(0, nck, idx_body, 0)

    def count_where(pred):
        def body(c, acc):
            k = key_ref[c]
            for j in range(n_lt):
                acc = acc + jnp.where(pred(k[:, j * 128:(j + 1) * 128], c, j), 1, 0)
            return acc
        acc = lax.fori_loop(0, nck, body, jnp.zeros((qb, 128), I32))
        return jnp.sum(acc, axis=1, keepdims=True)

    def bit_cond(carry):
        b, _, done = carry
        return (b < 32) & (jnp.min(done) == 0)

    def bit_body(carry):
        b, t, done = carry
        cand = t + (jnp.int32(1) << (31 - b))
        cnt = count_where(lambda k, c, j: k >= cand)
        t = jnp.where((cnt >= topk) & (done == 0), cand, t)
        done = jnp.where(cnt == topk, 1, done)
        return b + 1, t, done

    settled = jnp.where(q_pos + 1 <= topk, 1, 0)
    _, thr, done = lax.while_loop(bit_cond, bit_body,
                                  (jnp.int32(0), jnp.full((qb, 1), INT_MIN, I32), settled))
    thr = jnp.maximum(thr, KEY_NEG_INF + 1)

    def break_ties(need):
        def idx_of(c, j):
            return c * kc + j * 128 + lax.broadcasted_iota(I32, (1, 128), 1)

        def pos_body(b, m):
            cand = m + (jnp.int32(1) << (seq_bits - 1 - b))
            cnt = count_where(lambda k, c, j: (k == thr) & (idx_of(c, j) < cand))
            return jnp.where(cnt < need, cand, m)

        m = lax.fori_loop(0, seq_bits, pos_body, jnp.zeros((qb, 1), I32))

        def demote(c, carry):
            k = key_ref[c]
            idx = c * kc + lax.broadcasted_iota(I32, (1, kc), 1)
            key_ref[c] = jnp.where((k == thr) & (idx > m), thr - 1, k)
            return carry

        lax.fori_loop(0, nck, demote, 0)

    @pl.when(jnp.min(done) == 0)
    def _():
        c_gt = count_where(lambda k, c, j: k > thr)
        c_ge = count_where(lambda k, c, j: k >= thr)
        need = topk - c_gt

        @pl.when(jnp.max(c_ge - c_gt - need) > 0)
        def _():
            break_ties(need)

    q = _dot(cq, wuq_ref[...]).astype(BF16)
    qlat = _dot(q, wukbd_ref[...]) * (D_NOPE ** -0.5 * LOG2E)
    for h in range(H_A):
        qlat_ref[h * qb:(h + 1) * qb, :] = qlat[:, h * R_KV:(h + 1) * R_KV].astype(BF16)
    m_ref[...] = jnp.full(m_ref.shape, -1e30, F32)
    l_ref[...] = jnp.zeros(l_ref.shape, F32)
    acc_ref[...] = jnp.zeros(acc_ref.shape, F32)

    def att_body(c, carry):
        kv = ckv_ref[0, pl.ds(pl.multiple_of(c * kc, kc), kc), :]
        bias = jnp.where(key_ref[c] >= thr, 0.0, NEG_INF)
        s = _dot_nt(qlat_ref[...], kv)
        s = (s.reshape(H_A, qb, kc) + bias[None]).reshape(H_A * qb, kc)
        tiles = [s[:, j * 128:(j + 1) * 128] for j in range(n_lt)]
        m_cur = tiles[0]
        for t in tiles[1:]:
            m_cur = jnp.maximum(m_cur, t)
        m_old = m_ref[...]
        m_new = jnp.maximum(m_old, jnp.max(m_cur, axis=1, keepdims=True))
        ps = [jnp.exp2(t - m_new) for t in tiles]
        a = jnp.exp2(m_old - m_new)
        psum = ps[0]
        for t in ps[1:]:
            psum = psum + t
        l_ref[...] = a * l_ref[...] + psum
        p = jnp.concatenate([t.astype(BF16) for t in ps], axis=1)
        acc_ref[...] = a * acc_ref[...] + _dot(p, kv)
        m_ref[...] = m_new
        return carry

    lax.fori_loop(0, nck, att_body, 0)
    o_all = acc_ref[...] / jnp.sum(l_ref[...], axis=1, keepdims=True)
    o_lat = jnp.concatenate([o_all[h * qb:(h + 1) * qb, :] for h in range(H_A)], axis=1)
    o_ref[0] = _dot(o_lat.astype(BF16), wuvbd_ref[...]).astype(o_ref.dtype)


def _block_diag(blocks):
    n = len(blocks)
    r, c = blocks[0].shape
    out = jnp.zeros((n * r, n * c), blocks[0].dtype)
    for k, blk in enumerate(blocks):
        out = out.at[k * r:(k + 1) * r, k * c:(k + 1) * c].set(blk)
    return out


def _dsa(cq, ckv, kidx, widx, w_uq, w_uk, w_uv, w_qidx, qb=128, kc=512):
    B, S, _ = cq.shape
    kc = min(kc, S)
    topk = min(TOPK_MAX, S // 4)
    assert S % kc == 0 and kc % qb == 0 and kc >= topk
    seq_bits = max(1, (S - 1).bit_length())
    wqidx = jnp.transpose(w_qidx, (1, 0, 2)).astype(BF16)
    wuq = w_uq.reshape(R_Q, H_A * D_NOPE).astype(BF16)
    wukbd = _block_diag([w_uk[:, h, :].T for h in range(H_A)]).astype(BF16)
    wuvbd = _block_diag([w_uv[:, h, :] for h in range(H_A)]).astype(BF16)
    kern = functools.partial(_dsa_kernel, qb=qb, kc=kc, topk=topk, seq_bits=seq_bits)
    blk_q = lambda b, i: (b, i, 0)
    seq = lambda b, i: (b, 0, 0)
    fix2 = lambda b, i: (0, 0)
    fix3 = lambda b, i: (0, 0, 0)
    return pl.pallas_call(
        kern,
        grid=(B, S // qb),
        in_specs=[pl.BlockSpec((1, qb, R_Q), blk_q), pl.BlockSpec((1, S, R_KV), seq),
                  pl.BlockSpec((1, S, D_IDX), seq), pl.BlockSpec((1, qb, H_IDX), blk_q),
                  pl.BlockSpec(wqidx.shape, fix3), pl.BlockSpec(wuq.shape, fix2),
                  pl.BlockSpec(wukbd.shape, fix2), pl.BlockSpec(wuvbd.shape, fix2)],
        out_specs=pl.BlockSpec((1, qb, H_A * D_VA), blk_q),
        out_shape=jax.ShapeDtypeStruct((B, S, H_A * D_VA), BF16),
        scratch_shapes=[pltpu.VMEM((S // kc, qb, kc), I32),
                        pltpu.VMEM((H_A * qb, R_KV), BF16),
                        pltpu.VMEM((H_A * qb, R_KV), F32), pltpu.VMEM((H_A * qb, R_KV), F32),
                        pltpu.VMEM((H_A * qb, R_KV), F32)],
        compiler_params=_cparams(("parallel", "arbitrary")),
        name="dsa_attention",
    )(cq, ckv, kidx, widx, wqidx, wuq, wukbd, wuvbd)


GDN_HALO = 8


def _softplus(x):
    return jnp.maximum(x, 0.0) + jnp.log1p(jnp.exp(-jnp.abs(x)))


def _gdn_kernel(qkvz_ref, abc_ref, abr_ref, convw_ref, prm_c_ref, prm_r_ref, ng_ref, tri_ref,
                o_ref, xbuf_ref, state_ref, conv_ref, *, cb):
    C = DN_CHUNK
    W = H_B * D_B
    j = pl.program_id(1)

    @pl.when(j == 0)
    def _():
        xbuf_ref[0:GDN_HALO, :] = jnp.zeros((GDN_HALO, 3 * W), F32)
        state_ref[...] = jnp.zeros(state_ref.shape, F32)

    xbuf_ref[GDN_HALO:, :] = qkvz_ref[0, :, 0:3 * W]
    acc = xbuf_ref[GDN_HALO:, :] * convw_ref[CONV_K - 1:CONV_K, :]
    for t in range(CONV_K - 1):
        sh = CONV_K - 1 - t
        acc = acc + xbuf_ref[pl.ds(GDN_HALO - sh, cb), :] * convw_ref[t:t + 1, :]
    conv_ref[...] = _silu(acc)
    xbuf_ref[0:GDN_HALO, :] = xbuf_ref[cb:cb + GDN_HALO, :]

    abc = abc_ref[0]
    abr = abr_ref[0]
    beta_c = _sigmoid(abc)
    g_c = -jnp.exp(prm_r_ref[0:1, :]) * _softplus(abc + prm_r_ref[1:2, :])
    g_r = -jnp.exp(prm_c_ref[:, 0:1]) * _softplus(abr + prm_c_ref[:, 1:2])
    gc_c = _dot(tri_ref[...], g_c, precision=HIGHEST)
    gc_r = _dot_nt(g_r, tri_ref[...], precision=HIGHEST)

    ri = lax.broadcasted_iota(I32, (cb, cb), 0)
    ci = lax.broadcasted_iota(I32, (cb, cb), 1)
    same = (ri // C) == (ci // C)
    lower = same & (ri >= ci)
    strict = same & (ri > ci)
    eye = jnp.where(ri == ci, 1.0, 0.0)
    bf = lambda t: t.astype(BF16)

    for h in range(H_B):
        q = conv_ref[:, h * D_B:(h + 1) * D_B]
        k = conv_ref[:, W + h * D_B:W + (h + 1) * D_B]
        v = conv_ref[:, 2 * W + h * D_B:2 * W + (h + 1) * D_B]
        q = q * lax.rsqrt(jnp.sum(q * q, axis=-1, keepdims=True) + 1e-6) * (D_B ** -0.5)
        k = k * lax.rsqrt(jnp.sum(k * k, axis=-1, keepdims=True) + 1e-6)
        beta = beta_c[:, h:h + 1]
        gcol = gc_c[:, H_B + h:H_B + h + 1]
        grow = gc_r[H_B + h:H_B + h + 1, :]
        decay = jnp.exp(jnp.where(lower, gcol - grow, NEG_INF))
        kb = k * beta
        vb = v * beta
        kq = bf(k)
        a_mat = jnp.where(strict, _dot_nt(bf(kb), kq) * decay, 0.0)
        xm = -a_mat
        t_mat = eye + xm
        for _ in range(int(math.log2(C)) - 1):
            xq = bf(xm)
            xm = _dot(xq, xq)
            t_mat = t_mat + _dot(bf(t_mat), bf(xm))
        egc = jnp.exp(gcol)
        tq = bf(t_mat)
        u = _dot(tq, bf(vb))
        w = bf(_dot(tq, bf(kb * egc)))
        qk = bf(jnp.where(lower, _dot_nt(bf(q), kq) * decay, 0.0))
        q_dec = bf(q * egc)
        for n in range(cb // C):
            r0 = n * C
            glast = gcol[r0 + C - 1:r0 + C, :]
            k_dec = bf(k[r0:r0 + C] * jnp.exp(glast - gcol[r0:r0 + C]))
            st = state_ref[h]
            stq = bf(st)
            v_new = u[r0:r0 + C] - _dot(w[r0:r0 + C], stq)
            vq = bf(v_new)
            o = _dot(q_dec[r0:r0 + C], stq) + _dot(qk[r0:r0 + C, r0:r0 + C], vq)
            state_ref[h] = st * jnp.exp(glast) + _dot_tn(k_dec, vq)
            o = o * lax.rsqrt(jnp.mean(o * o, axis=-1, keepdims=True) + 1e-6) * ng_ref[...]
            z = qkvz_ref[0, r0:r0 + C, 3 * W + h * D_B:3 * W + (h + 1) * D_B]
            o_ref[0, r0:r0 + C, h * D_B:(h + 1) * D_B] = (o * _silu(z)).astype(o_ref.dtype)


def _gdn(qkvz, b, a, conv_w, a_log, dt_bias, norm_g, cb=256):
    B, S, _ = qkvz.shape
    W = H_B * D_B
    cb = min(cb, S)
    abc = jnp.concatenate([b, a], axis=-1)
    abr = jnp.transpose(abc, (0, 2, 1))
    zeros = jnp.zeros((H_B,), F32)
    prm = jnp.stack([jnp.concatenate([zeros, a_log]), jnp.concatenate([zeros, dt_bias])])
    idx = jnp.arange(cb)
    tri = ((idx[:, None] >= idx[None, :]) & (idx[:, None] // DN_CHUNK == idx[None, :] // DN_CHUNK)).astype(F32)
    kern = functools.partial(_gdn_kernel, cb=cb)
    blk = lambda bi, j: (bi, j, 0)
    fix = lambda bi, j: (0, 0)
    return pl.pallas_call(
        kern,
        grid=(B, S // cb),
        in_specs=[pl.BlockSpec((1, cb, 4 * W), blk), pl.BlockSpec((1, cb, 2 * H_B), blk),
                  pl.BlockSpec((1, 2 * H_B, cb), lambda bi, j: (bi, 0, j)),
                  pl.BlockSpec((CONV_K, 3 * W), fix), pl.BlockSpec((2 * H_B, 2), fix),
                  pl.BlockSpec((2, 2 * H_B), fix), pl.BlockSpec((1, D_B), fix),
                  pl.BlockSpec((cb, cb), fix)],
        out_specs=pl.BlockSpec((1, cb, W), blk),
        out_shape=jax.ShapeDtypeStruct((B, S, W), BF16),
        scratch_shapes=[pltpu.VMEM((cb + GDN_HALO, 3 * W), F32), pltpu.VMEM((H_B, D_B, D_B), F32),
                        pltpu.VMEM((cb, 3 * W), F32)],
        compiler_params=_cparams(("parallel", "arbitrary")),
        name="gated_deltanet",
    )(qkvz, abc, abr, conv_w, prm.T, prm, norm_g.reshape(1, D_B), tri)


def _store_row_tiles(ref, val):
    n, d = val.shape
    sub = d // LANES
    for j in range(sub):
        ref[pl.ds(j, n, stride=sub), :] = val[:, j * LANES:(j + 1) * LANES]


def _load_row_tiles(ref, n, sub):
    return jnp.concatenate([ref[pl.ds(j, n, stride=sub), :] for j in range(sub)], axis=1)


def _post_mixer_kernel(a1_ref, a2_ref, h_ref, w1_ref, w2_ref, g_ref, b_ref, rw_ref, rb_ref, triu_ref,
                       h1_ref, h1t_ref, tope_ref, gate_ref, rank_ref, cnt_ref, carry_ref):
    i = pl.program_id(0)
    E, tm = rw_ref.shape[0], h_ref.shape[0]

    @pl.when(i == 0)
    def _():
        carry_ref[...] = jnp.zeros(carry_ref.shape, F32)

    mix = _dot(a1_ref[...].astype(BF16), w1_ref[...]) + _dot(a2_ref[...].astype(BF16), w2_ref[...])
    h1 = _layernorm_rows(ALPHA * h_ref[...] + mix, g_ref[...], b_ref[...])
    h1_ref[...] = h1
    _store_row_tiles(h1t_ref, h1)

    logits = _dot_nt(rw_ref[...], h1, precision=HIGHEST) + rb_ref[...]
    erow = lax.broadcasted_iota(I32, (E, tm), 0)
    sel = jnp.zeros((E, tm), F32)
    onehots, tops = [], []
    for k in range(TOP_K):
        mx = jnp.max(logits, axis=0, keepdims=True)
        idx = jnp.min(jnp.where(logits == mx, erow, E), axis=0, keepdims=True)
        oh = erow == idx
        logits = jnp.where(oh, NEG_INF, logits)
        sel = sel + jnp.where(oh, 1.0, 0.0)
        onehots.append(oh)
        tops.append(mx)
        tope_ref[k:k + 1, :] = idx
    exps = [jnp.exp(t - tops[0]) for t in tops]
    den = exps[0] + exps[1] + exps[2] + exps[3]
    for k in range(TOP_K):
        gate_ref[k:k + 1, :] = exps[k] / den
    incl = _dot(sel.astype(BF16), triu_ref[...])
    excl = incl - sel + carry_ref[:, 0:1]
    for k in range(TOP_K):
        rank_ref[k:k + 1, :] = jnp.sum(jnp.where(onehots[k], excl, 0.0), axis=0, keepdims=True).astype(I32)
    carry_ref[...] = carry_ref[...] + jnp.sum(sel, axis=1, keepdims=True)
    cnt_ref[...] = carry_ref[...]


def _post_mixer(a1, a2, h2d, w_out, ln_g, ln_b, router_w, router_b, tm=512):
    T, D = h2d.shape
    E = router_w.shape[1]
    n1, n2 = a1.shape[1], a2.shape[1]
    w1 = w_out[:n1].astype(BF16)
    w2 = w_out[n1:].astype(BF16)
    idx = jnp.arange(tm)
    triu = (idx[:, None] <= idx[None, :]).astype(BF16)
    row = lambda i: (i, 0)
    col = lambda i: (0, i)
    fix = lambda i: (0, 0)
    return pl.pallas_call(
        _post_mixer_kernel,
        grid=(T // tm,),
        in_specs=[pl.BlockSpec((tm, n1), row), pl.BlockSpec((tm, n2), row), pl.BlockSpec((tm, D), row),
                  pl.BlockSpec((n1, D), fix), pl.BlockSpec((n2, D), fix),
                  pl.BlockSpec((1, D), fix), pl.BlockSpec((1, D), fix),
                  pl.BlockSpec((E, D), fix), pl.BlockSpec((E, 1), fix), pl.BlockSpec((tm, tm), fix)],
        out_specs=[pl.BlockSpec((tm, D), row), pl.BlockSpec((tm * (D // LANES), LANES), row),
                   pl.BlockSpec((TOP_K, tm), col), pl.BlockSpec((TOP_K, tm), col),
                   pl.BlockSpec((TOP_K, tm), col), pl.BlockSpec((E, 128), fix)],
        out_shape=[jax.ShapeDtypeStruct((T, D), F32), jax.ShapeDtypeStruct((T * (D // LANES), LANES), F32),
                   jax.ShapeDtypeStruct((TOP_K, T), I32), jax.ShapeDtypeStruct((TOP_K, T), F32),
                   jax.ShapeDtypeStruct((TOP_K, T), I32), jax.ShapeDtypeStruct((E, 128), F32)],
        scratch_shapes=[pltpu.VMEM((E, 128), F32)],
        compiler_params=_cparams(("arbitrary",)),
        name="post_mixer_router",
    )(a1, a2, h2d, w1, w2, ln_g.reshape(1, D), ln_b.reshape(1, D), router_w.T, router_b.reshape(E, 1), triu)


def _dispatch_kernel(dest_ref, h_ref, xs_init_hbm, xs_hbm, sem, *, tt, n_tok, sub):
    del xs_init_hbm
    base = pl.program_id(0) * tt

    def row_copy(r, k):
        d = dest_ref[k * n_tok + base + r]
        return pltpu.make_async_copy(h_ref.at[pl.ds(pl.multiple_of(r * sub, sub), sub)],
                                     xs_hbm.at[pl.ds(pl.multiple_of(d * sub, sub), sub)], sem)

    def start(r, c):
        for k in range(TOP_K):
            row_copy(r, k).start(priority=k % 2)
        return c

    def wait(r, c):
        for k in range(TOP_K):
            row_copy(r, k).wait()
        return c

    lax.fori_loop(0, tt, start, 0)
    lax.fori_loop(0, tt, wait, 0)


def _moe_dispatch(h1t, dest_flat, n_slots, sub, tt=512):
    T = h1t.shape[0] // sub
    any_spec = pl.BlockSpec(memory_space=pl.ANY)
    return pl.pallas_call(
        functools.partial(_dispatch_kernel, tt=tt, n_tok=T, sub=sub),
        grid_spec=pltpu.PrefetchScalarGridSpec(
            num_scalar_prefetch=1, grid=(T // tt,),
            in_specs=[pl.BlockSpec((tt * sub, LANES), lambda i, d: (i, 0)), any_spec], out_specs=any_spec,
            scratch_shapes=[pltpu.SemaphoreType.DMA(())]),
        out_shape=jax.ShapeDtypeStruct((n_slots * sub, LANES), h1t.dtype),
        input_output_aliases={2: 0},
        compiler_params=_cparams(("arbitrary",)),
        name="moe_dispatch",
    )(dest_flat, h1t, jnp.zeros((n_slots * sub, LANES), h1t.dtype))


MOE_BLOCK = 256


def _moe_kernel(be_ref, nu_ref, x_ref, wgu_ref, bg_ref, bu_ref, wd_ref, bd_ref, y_ref,
                wt_ref, wg_ref, wu_ref, wdb_ref, *, bm, sub):
    i = pl.program_id(0)
    e = be_ref[i]
    F = wd_ref.shape[1]

    @pl.when((i < nu_ref[0]) & ((i == 0) | (e != be_ref[jnp.maximum(i - 1, 0)])))
    def _():
        ck = 512
        for c in range(0, 2 * F, ck):
            wt = wgu_ref[0, :, c:c + ck].T
            for j in range(sub):
                wt_ref[pl.ds(c * sub + j, ck, stride=sub), :] = wt[:, j * LANES:(j + 1) * LANES]
        for j in range(sub):
            cols = slice(j * LANES, (j + 1) * LANES)
            wg_ref[:, cols] = wt_ref[pl.ds(j, F, stride=2 * sub), :].astype(BF16)
            wu_ref[:, cols] = wt_ref[pl.ds(sub + j, F, stride=2 * sub), :].astype(BF16)
        wdb_ref[...] = wd_ref[0].astype(BF16)

    @pl.when(i < nu_ref[0])
    def _():
        x = _load_row_tiles(x_ref, bm, sub).astype(BF16)
        g = _dot_nt(x, wg_ref[...]) + bg_ref[0]
        u = _dot_nt(x, wu_ref[...]) + bu_ref[0]
        gt = jnp.minimum(g, SWIGLU_LIMIT)
        up = jnp.clip(u, -SWIGLU_LIMIT, SWIGLU_LIMIT)
        hid = (up + 1.0) * (gt * _sigmoid(gt * SWIGLU_ALPHA))
        _store_row_tiles(y_ref, _dot(hid.astype(BF16), wdb_ref[...]) + bd_ref[0])

    @pl.when(i >= nu_ref[0])
    def _():
        y_ref[...] = jnp.zeros(y_ref.shape, y_ref.dtype)


def _moe_experts(xs, blk_e, n_used, w_gu, bg, bu, w_down, bd, sub):
    E, D, F2 = w_gu.shape
    F = F2 // 2
    bm = MOE_BLOCK
    P = xs.shape[0] // sub
    wsel = lambda i, be, nu: (be[i], 0, 0)
    row = lambda i, be, nu: (i, 0)
    return pl.pallas_call(
        functools.partial(_moe_kernel, bm=bm, sub=sub),
        grid_spec=pltpu.PrefetchScalarGridSpec(
            num_scalar_prefetch=2,
            grid=(P // bm,),
            in_specs=[pl.BlockSpec((bm * sub, LANES), row),
                      pl.BlockSpec((1, D, F2), wsel),
                      pl.BlockSpec((1, 1, F), wsel), pl.BlockSpec((1, 1, F), wsel),
                      pl.BlockSpec((1, F, D), wsel), pl.BlockSpec((1, 1, D), wsel)],
            out_specs=pl.BlockSpec((bm * sub, LANES), row),
            scratch_shapes=[pltpu.VMEM((F2 * sub, LANES), F32), pltpu.VMEM((F, D), BF16), pltpu.VMEM((F, D), BF16),
                            pltpu.VMEM((F, D), BF16)]),
        out_shape=jax.ShapeDtypeStruct((P * sub, LANES), F32),
        compiler_params=_cparams(("arbitrary",)),
        name="moe_experts",
    )(blk_e, n_used, xs, w_gu, bg, bu, w_down, bd)


def _moe(h1t, top_e, rank, cnt, w_gu, b_gu, w_down, b_down):
    E, D = w_gu.shape[0], w_gu.shape[1]
    sub = D // LANES
    T = h1t.shape[0] // sub
    bm = MOE_BLOCK
    counts = cnt[:, 0].astype(I32)
    padded = (counts + bm - 1) // bm * bm
    pad_end = jnp.cumsum(padded)
    pad_start = pad_end - padded
    e_ids = jnp.arange(E, dtype=I32)[:, None, None]
    start_of = jnp.sum(jnp.where(top_e[None] == e_ids, pad_start[:, None, None], 0), axis=0)
    dest = (start_of + rank).reshape(-1)
    nblk = (T * TOP_K) // bm + E
    blk_first = jnp.arange(nblk, dtype=I32) * bm
    blk_e = jnp.minimum(jnp.sum(pad_end[None, :] <= blk_first[:, None], axis=1), E - 1).astype(I32)
    n_used = (pad_end[-1:] // bm).astype(I32)
    xs = _moe_dispatch(h1t, dest, nblk * bm, sub)
    ys = _moe_experts(xs, blk_e, n_used, w_gu, b_gu[:, None, 0::2], b_gu[:, None, 1::2],
                      w_down, b_down[:, None, :], sub)
    return ys, dest


def _tail_kernel(dest_ref, h1_ref, gate_ref, p_ref, wg_ref, wp_ref, g_ref, b_ref, ys_hbm, o_ref,
                 ybuf, sem, *, tm, n_tok, sub):
    i = pl.program_id(0)

    def row_copy(tile, slot, r, k):
        d = dest_ref[k * n_tok + tile * tm + r]
        return pltpu.make_async_copy(ys_hbm.at[pl.ds(pl.multiple_of(d * sub, sub), sub)],
                                     ybuf.at[slot, k, pl.ds(pl.multiple_of(r * sub, sub), sub)], sem.at[slot])

    def start_tile(tile, slot):
        def body(r, c):
            for k in range(TOP_K):
                row_copy(tile, slot, r, k).start(priority=k % 2)
            return c
        lax.fori_loop(0, tm, body, 0)

    def wait_tile(tile, slot):
        def body(r, c):
            for k in range(TOP_K):
                row_copy(tile, slot, r, k).wait()
            return c
        lax.fori_loop(0, tm, body, 0)

    @pl.when(i == 0)
    def _():
        start_tile(0, 0)

    h1 = h1_ref[...]
    ple = _sigmoid(_dot(h1.astype(BF16), wg_ref[...])) * _dot(p_ref[...].astype(BF16), wp_ref[...])
    x = ALPHA * h1 + ple

    for slot in range(2):
        @pl.when(i % 2 == slot)
        def _(slot=slot):
            @pl.when(i + 1 < pl.num_programs(0))
            def _():
                start_tile(i + 1, 1 - slot)

            wait_tile(i, slot)
            cols = []
            for j in range(sub):
                col = ybuf[slot, 0, pl.ds(j, tm, stride=sub), :] * gate_ref[:, 0:1]
                for k in range(1, TOP_K):
                    col = col + ybuf[slot, k, pl.ds(j, tm, stride=sub), :] * gate_ref[:, k:k + 1]
                cols.append(col)
            ffn = jnp.concatenate(cols, axis=1)
            o_ref[...] = _layernorm_rows(x + ffn, g_ref[...], b_ref[...])


def _layer_tail(h1, ys, dest_flat, gate_t, p2d, ple_w_gate, ple_w_proj, ln_g, ln_b, tm=256):
    T, D = h1.shape
    PD = p2d.shape[1]
    sub = D // LANES
    row = lambda i, d: (i, 0)
    fix = lambda i, d: (0, 0)
    return pl.pallas_call(
        functools.partial(_tail_kernel, tm=tm, n_tok=T, sub=sub),
        grid_spec=pltpu.PrefetchScalarGridSpec(
            num_scalar_prefetch=1, grid=(T // tm,),
            in_specs=[pl.BlockSpec((tm, D), row), pl.BlockSpec((tm, TOP_K), row),
                      pl.BlockSpec((tm, PD), row), pl.BlockSpec((D, D), fix), pl.BlockSpec((PD, D), fix),
                      pl.BlockSpec((1, D), fix), pl.BlockSpec((1, D), fix),
                      pl.BlockSpec(memory_space=pl.ANY)],
            out_specs=pl.BlockSpec((tm, D), row),
            scratch_shapes=[pltpu.VMEM((2, TOP_K, tm * sub, LANES), F32), pltpu.SemaphoreType.DMA((2,))]),
        out_shape=jax.ShapeDtypeStruct((T, D), F32),
        compiler_params=_cparams(("arbitrary",)),
        name="layer_tail",
    )(dest_flat, h1, gate_t.T, p2d, ple_w_gate.astype(BF16), ple_w_proj.astype(BF16),
      ln_g.reshape(1, D), ln_b.reshape(1, D), ys)


def _cd_proj_kernel(x_ref, w_ref, q_ref, k_ref, v_ref, u_ref):
    x = x_ref[...].astype(BF16)
    n = H_C * DH_C
    q_ref[...] = (_dot(x, w_ref[:, 0:n]) * (DH_C ** -0.5 * LOG2E)).astype(q_ref.dtype)
    k_ref[...] = _dot(x, w_ref[:, n:2 * n]).astype(k_ref.dtype)
    v_ref[...] = _dot(x, w_ref[:, 2 * n:3 * n]).astype(v_ref.dtype)
    u_ref[...] = _dot(x, w_ref[:, 3 * n:])


def _cd_in_proj(h2d, w_in, tm=512):
    T, D = h2d.shape
    n = H_C * DH_C
    nu = w_in.shape[1] - 3 * n
    row = lambda i: (i, 0)
    return pl.pallas_call(
        _cd_proj_kernel,
        grid=(T // tm,),
        in_specs=[pl.BlockSpec((tm, D), row), pl.BlockSpec(w_in.shape, lambda i: (0, 0))],
        out_specs=[pl.BlockSpec((tm, n), row)] * 3 + [pl.BlockSpec((tm, nu), row)],
        out_shape=[jax.ShapeDtypeStruct((T, n), BF16)] * 3 + [jax.ShapeDtypeStruct((T, nu), F32)],
        compiler_params=_cparams(("parallel",)),
        name="cd_in_proj",
    )(h2d, w_in.astype(BF16))


def _dilated_bias_table(qb):
    import numpy as np
    max_w = max(w for w, _ in DILATED_PATTERNS)
    ndc = max_w // qb + 1
    r = np.arange(qb)[:, None]
    j = np.arange(qb)[None, :]
    tbl = np.empty((ndc, qb, qb), np.float32)
    for dc in range(ndc):
        delta = dc * qb + r - j
        mult = np.zeros((qb, qb), np.float64)
        for w, d in DILATED_PATTERNS:
            mult += (delta >= 0) & (delta <= w) & (delta % d == 0)
        with np.errstate(divide="ignore"):
            tbl[dc] = np.log2(mult)
    return jnp.asarray(tbl)


def _dilated_kernel(q_ref, k_ref, v_ref, bias_ref, o_ref, q2_ref, m_ref, l_ref, acc_ref, *, qb, ndc):
    i = pl.program_id(2)
    n_lt = qb // 128
    q = q_ref[0]
    lo = lax.broadcasted_iota(I32, q.shape, 1) < DH_C
    zero = jnp.zeros(q.shape, q.dtype)
    q2_ref[0:qb, :] = jnp.where(lo, q, zero)
    q2_ref[qb:, :] = jnp.where(lo, zero, q)
    m_ref[...] = jnp.full(m_ref.shape, -1e30, F32)
    l_ref[...] = jnp.zeros(l_ref.shape, F32)
    acc_ref[...] = jnp.zeros(acc_ref.shape, F32)

    def body(c, carry):
        r0 = pl.multiple_of(c * qb, qb)
        kk = k_ref[0, pl.ds(r0, qb), :]
        vv = v_ref[0, pl.ds(r0, qb), :]
        s = _dot_nt(q2_ref[...], kk)
        s = (s.reshape(2, qb, qb) + bias_ref[i - c][None]).reshape(2 * qb, qb)
        tiles = [s[:, j * 128:(j + 1) * 128] for j in range(n_lt)]
        m_cur = tiles[0]
        for t in tiles[1:]:
            m_cur = jnp.maximum(m_cur, t)
        m_old = m_ref[...]
        m_new = jnp.maximum(m_old, jnp.max(m_cur, axis=1, keepdims=True))
        ps = [jnp.exp2(t - m_new) for t in tiles]
        a = jnp.exp2(m_old - m_new)
        psum = ps[0]
        for t in ps[1:]:
            psum = psum + t
        l_ref[...] = a * l_ref[...] + psum
        p = jnp.concatenate([t.astype(BF16) for t in ps], axis=1)
        acc_ref[...] = a * acc_ref[...] + _dot(p, vv)
        m_ref[...] = m_new
        return carry

    lax.fori_loop(jnp.maximum(i - (ndc - 1), 0), i + 1, body, 0)
    o_all = acc_ref[...] / jnp.sum(l_ref[...], axis=1, keepdims=True)
    o_ref[0] = jnp.where(lo, o_all[0:qb, :], o_all[qb:, :]).astype(o_ref.dtype)


def _dilated_attention(q, k, v, qb=512):
    B, S, n = q.shape
    qb = min(qb, S)
    bias = _dilated_bias_table(qb)
    ndc = bias.shape[0]
    pw = 2 * DH_C
    kern = functools.partial(_dilated_kernel, qb=qb, ndc=ndc)
    blk = lambda b, pr, i: (b, i, pr)
    seq = lambda b, pr, i: (b, 0, pr)
    return pl.pallas_call(
        kern,
        grid=(B, n // pw, S // qb),
        in_specs=[pl.BlockSpec((1, qb, pw), blk), pl.BlockSpec((1, S, pw), seq),
                  pl.BlockSpec((1, S, pw), seq), pl.BlockSpec(bias.shape, lambda b, pr, i: (0, 0, 0))],
        out_specs=pl.BlockSpec((1, qb, pw), blk),
        out_shape=jax.ShapeDtypeStruct((B, S, n), BF16),
        scratch_shapes=[pltpu.VMEM((2 * qb, pw), BF16), pltpu.VMEM((2 * qb, pw), F32),
                        pltpu.VMEM((2 * qb, pw), F32), pltpu.VMEM((2 * qb, pw), F32)],
        compiler_params=_cparams(("parallel", "parallel", "arbitrary")),
        name="dilated_attention",
    )(q, k, v, bias)


POOL_HALO = 16


def _pool_kernel(u_ref, w_ref, sc_ref, o_ref, xbuf_ref, *, tm):
    j = pl.program_id(1)

    @pl.when(j == 0)
    def _():
        xbuf_ref[0:POOL_HALO, :] = jnp.zeros((POOL_HALO, xbuf_ref.shape[1]), F32)

    xbuf_ref[POOL_HALO:, :] = u_ref[0]
    x = xbuf_ref[POOL_HALO:, :]
    grp = lax.broadcasted_iota(I32, (1, x.shape[1]), 1) // POOL_GROUP
    run = x
    sel = jnp.zeros(x.shape, F32)
    win = jnp.zeros((1, x.shape[1]), F32)
    for d in range(1, max(POOL_WINDOWS)):
        run = run + xbuf_ref[pl.ds(POOL_HALO - d, tm), :]
        if d + 1 in POOL_WINDOWS:
            gi = POOL_WINDOWS.index(d + 1)
            sel = jnp.where(grp == gi, run, sel)
            win = jnp.where(grp == gi, float(d + 1), win)
    pos = j * tm + lax.broadcasted_iota(I32, (tm, 1), 0)
    mean = sel / jnp.minimum((pos + 1).astype(F32), win)
    o_ref[0] = _dot((mean - x).astype(BF16), w_ref[...]) * sc_ref[...]
    xbuf_ref[0:POOL_HALO, :] = xbuf_ref[tm:tm + POOL_HALO, :]


def _multiscale_pool(u, pool_w, pool_scale, tm=512):
    B, S, n = u.shape
    tm = min(tm, S)
    wbd = _block_diag([pool_w[g] for g in range(pool_w.shape[0])]).astype(BF16)
    blk = lambda b, j: (b, j, 0)
    fix = lambda b, j: (0, 0)
    return pl.pallas_call(
        functools.partial(_pool_kernel, tm=tm),
        grid=(B, S // tm),
        in_specs=[pl.BlockSpec((1, tm, n), blk), pl.BlockSpec((n, n), fix), pl.BlockSpec((1, n), fix)],
        out_specs=pl.BlockSpec((1, tm, n), blk),
        out_shape=jax.ShapeDtypeStruct((B, S, n), F32),
        scratch_shapes=[pltpu.VMEM((tm + POOL_HALO, n), F32)],
        compiler_params=_cparams(("parallel", "arbitrary")),
        name="multiscale_pool",
    )(u, wbd, pool_scale.reshape(1, n))


def kernel(x, p, ab_w_in, ab_q_norm_g, ab_kv_norm_g, ab_w_uq, ab_w_uk, ab_w_uv, ab_w_qidx,
           ab_kidx_norm_g, ab_kidx_norm_b, ab_conv_w, ab_a_log, ab_dt_bias, ab_out_norm_g, ab_w_out,
           cd_w_in, cd_pool_w, cd_pool_scale, cd_w_out, ln_mix_g, ln_mix_b, router_w, router_b,
           w_gu, b_gu, w_down, b_down, ple_w_proj, ple_w_gate, ln_ffn_g, ln_ffn_b):
    B, S, D = x.shape
    T = B * S
    h = x.reshape(T, D)
    for i in range(DEPTH):
        j = i // 2
        if i % 2 == 0:
            cq, ckv, kidx, small, qkvz = _ab_in_proj(h, ab_w_in[j], ab_q_norm_g[j], ab_kv_norm_g[j],
                                                     ab_kidx_norm_g[j], ab_kidx_norm_b[j])
            sh = lambda t: t.reshape(B, S, -1)
            o_w = D_IDX + H_IDX
            o_a = _dsa(sh(cq), sh(ckv), sh(kidx), sh(small[:, D_IDX:o_w]),
                       ab_w_uq[j], ab_w_uk[j], ab_w_uv[j], ab_w_qidx[j])
            o_b = _gdn(sh(qkvz), sh(small[:, o_w:o_w + H_B]), sh(small[:, o_w + H_B:o_w + 2 * H_B]),
                       ab_conv_w[j], ab_a_log[j], ab_dt_bias[j], ab_out_norm_g[j])
            a1, a2, w_out = o_a.reshape(T, -1), o_b.reshape(T, -1), ab_w_out[j]
        else:
            q, k, v, u = _cd_in_proj(h, cd_w_in[j])
            sh = lambda t: t.reshape(B, S, -1)
            o_c = _dilated_attention(sh(q), sh(k), sh(v))
            o_d = _multiscale_pool(sh(u), cd_pool_w[j], cd_pool_scale[j])
            a1, a2, w_out = o_c.reshape(T, -1), o_d.reshape(T, -1), cd_w_out[j]
        h1, h1t, top_e, gate_t, rank, cnt = _post_mixer(a1, a2, h, w_out, ln_mix_g[i], ln_mix_b[i],
                                                        router_w[i], router_b[i])
        ys, dest = _moe(h1t, top_e, rank, cnt, w_gu[i], b_gu[i], w_down[i], b_down[i])
        h = _layer_tail(h1, ys, dest, gate_t, p[i].reshape(T, -1), ple_w_gate[i], ple_w_proj[i],
                        ln_ffn_g[i], ln_ffn_b[i])
    return h.reshape(B, S, D)
```

```python
import functools
import math

import jax
import jax.numpy as jnp
from jax import lax
from jax.experimental import pallas as pl
from jax.experimental.pallas import tpu as pltpu

F32 = jnp.float32
BF16 = jnp.bfloat16
I32 = jnp.int32

DEPTH = 2
H_A, D_NOPE, D_VA, R_Q, R_KV, H_IDX, D_IDX = 8, 64, 64, 256, 128, 8, 64
TOPK_MAX = 256
H_B, D_B, CONV_K, DN_CHUNK = 4, 128, 4, 64
H_C, DH_C = 12, 64
DILATED_PATTERNS = ((128, 1), (512, 4), (2048, 16))
POOL_WINDOWS = (2, 4, 8, 16)
POOL_GROUP = 64
N_EXPERTS, TOP_K = 32, 4
SWIGLU_LIMIT, SWIGLU_ALPHA = 7.0, 1.702
ALPHA = (2 * DEPTH) ** 0.25

VMEM_LIMIT_BYTES = 56 * 1024 * 1024
LANES = 128
HIGHEST = lax.Precision.HIGHEST
NEG_INF = float("-inf")
LOG2E = math.log2(math.e)
INT_MIN = -2 ** 31
KEY_NEG_INF = (0xFF800000 ^ 0x7FFFFFFF) - 2 ** 32


def _cparams(sem):
    return pltpu.CompilerParams(dimension_semantics=sem, vmem_limit_bytes=VMEM_LIMIT_BYTES)


def _dot(a, b, precision=None):
    return jnp.dot(a, b, preferred_element_type=F32, precision=precision)


def _dot_nt(a, b, precision=None):
    return lax.dot_general(a, b, (((1,), (1,)), ((), ())), preferred_element_type=F32,
                           precision=precision)


def _dot_tn(a, b, precision=None):
    return lax.dot_general(a, b, (((0,), (0,)), ((), ())), preferred_element_type=F32,
                           precision=precision)


def _sigmoid(x):
    return 1.0 / (1.0 + jnp.exp(-x))


def _silu(x):
    return x * _sigmoid(x)


def _layernorm_rows(x, g, b, eps=1e-5):
    mu = jnp.mean(x, axis=-1, keepdims=True)
    xc = x - mu
    var = jnp.mean(xc * xc, axis=-1, keepdims=True)
    return xc * lax.rsqrt(var + eps) * g + b


AB_SMALL = 128


def _ab_proj_kernel(x_ref, w_ref, gq_ref, gkv_ref, gk_ref, bk_ref,
                    cq_ref, ckv_ref, kidx_ref, small_ref, qkvz_ref):
    x = x_ref[...].astype(BF16)
    cq = _dot(x, w_ref[:, 0:R_Q])
    cq = cq * lax.rsqrt(jnp.mean(cq * cq, axis=-1, keepdims=True) + 1e-6) * gq_ref[...]
    cq_ref[...] = cq.astype(cq_ref.dtype)
    ckv = _dot(x, w_ref[:, R_Q:R_Q + R_KV])
    ckv = ckv * lax.rsqrt(jnp.mean(ckv * ckv, axis=-1, keepdims=True) + 1e-6) * gkv_ref[...]
    ckv_ref[...] = ckv.astype(ckv_ref.dtype)
    off = R_Q + R_KV
    sm = _dot(x, w_ref[:, off:off + AB_SMALL])
    small_ref[...] = sm
    lane = lax.broadcasted_iota(I32, sm.shape, 1)
    is_k = lane < D_IDX
    mu = jnp.sum(jnp.where(is_k, sm, 0.0), axis=-1, keepdims=True) * (1.0 / D_IDX)
    xc = jnp.where(is_k, sm - mu, 0.0)
    var = jnp.sum(xc * xc, axis=-1, keepdims=True) * (1.0 / D_IDX)
    kn = xc * lax.rsqrt(var + 1e-5) * gk_ref[...] + bk_ref[...]
    kidx_ref[...] = kn[:, :D_IDX].astype(kidx_ref.dtype)
    off += AB_SMALL
    qkvz_ref[...] = _dot(x, w_ref[:, off:])


def _ab_in_proj(h2d, w_in, q_norm_g, kv_norm_g, kidx_g, kidx_b, tm=512):
    T, D = h2d.shape
    W = H_B * D_B
    o = [0, R_Q, R_Q + R_KV, R_Q + R_KV + D_IDX, R_Q + R_KV + D_IDX + H_IDX]
    o_q = o[4]
    o_b = o_q + 4 * W
    pad = AB_SMALL - (D_IDX + H_IDX + 2 * H_B)
    w_perm = jnp.concatenate([
        w_in[:, o[0]:o[2]],
        w_in[:, o[2]:o[4]], w_in[:, o_b:o_b + 2 * H_B],
        jnp.zeros((D, pad), w_in.dtype),
        w_in[:, o_q:o_b],
    ], axis=1).astype(BF16)
    n_all = w_perm.shape[1]
    gk = jnp.zeros((1, AB_SMALL), F32).at[0, :D_IDX].set(kidx_g)
    bk = jnp.zeros((1, AB_SMALL), F32).at[0, :D_IDX].set(kidx_b)
    row = lambda i: (i, 0)
    fixed = lambda i: (0, 0)
    return pl.pallas_call(
        _ab_proj_kernel,
        grid=(T // tm,),
        in_specs=[pl.BlockSpec((tm, D), row), pl.BlockSpec((D, n_all), fixed),
                  pl.BlockSpec((1, R_Q), fixed), pl.BlockSpec((1, R_KV), fixed),
                  pl.BlockSpec((1, AB_SMALL), fixed), pl.BlockSpec((1, AB_SMALL), fixed)],
        out_specs=[pl.BlockSpec((tm, R_Q), row), pl.BlockSpec((tm, R_KV), row),
                   pl.BlockSpec((tm, D_IDX), row), pl.BlockSpec((tm, AB_SMALL), row),
                   pl.BlockSpec((tm, 4 * W), row)],
        out_shape=[jax.ShapeDtypeStruct((T, R_Q), BF16), jax.ShapeDtypeStruct((T, R_KV), BF16),
                   jax.ShapeDtypeStruct((T, D_IDX), BF16), jax.ShapeDtypeStruct((T, AB_SMALL), F32),
                   jax.ShapeDtypeStruct((T, 4 * W), F32)],
        compiler_params=_cparams(("parallel",)),
        name="ab_in_proj",
    )(h2d, w_perm, q_norm_g.reshape(1, -1), kv_norm_g.reshape(1, -1), gk, bk)


def _sort_key(x):
    bits = pltpu.bitcast(x + 0.0, I32)
    return jnp.where(bits < 0, bits ^ 0x7FFFFFFF, bits)


def _dsa_kernel(cq_ref, ckv_ref, kidx_ref, widx_ref, wqidx_ref, wuq_ref, wukbd_ref, wuvbd_ref,
                o_ref, key_ref, qlat_ref, m_ref, l_ref, acc_ref, *, qb, kc, topk, seq_bits):
    i = pl.program_id(1)
    nck = ((i + 1) * qb + kc - 1) // kc
    n_lt = kc // 128
    cq = cq_ref[0]
    widx = widx_ref[0] * (H_IDX ** -0.5 * D_IDX ** -0.5)
    q_pos = i * qb + lax.broadcasted_iota(I32, (qb, 1), 0)

    qidx = [_dot(cq, wqidx_ref[h]).astype(BF16) for h in range(H_IDX)]

    def idx_body(c, carry):
        kblk = kidx_ref[0, pl.ds(pl.multiple_of(c * kc, kc), kc), :]
        isc = jnp.zeros((qb, kc), F32)
        for h in range(H_IDX):
            sc = _dot_nt(qidx[h], kblk)
            isc = isc + jnp.maximum(sc, 0.0) * widx[:, h:h + 1]
        k_pos = c * kc + lax.broadcasted_iota(I32, (1, kc), 1)
        key_ref[c] = jnp.where(k_pos <= q_pos, _sort_key(isc), KEY_NEG_INF)
        return carry

    lax.fori_loop(0, nck, idx_body, 0)

    rg = 128

    def row_group_threshold(r0):
        pos = q_pos[r0:r0 + rg]

        def count_where(pred):
            def body(c, acc):
                for j in range(n_lt):
                    acc = acc + jnp.where(pred(key_ref[c, r0:r0 + rg, j * 128:(j + 1) * 128], c, j), 1, 0)
                return acc
            acc = lax.fori_loop(0, nck, body, jnp.zeros((rg, 128), I32))
            return jnp.sum(acc, axis=1, keepdims=True)

        def bit_cond(carry):
            b, _, done = carry
            return (b < 32) & (jnp.min(done) == 0)

        def bit_body(carry):
            b, t, done = carry
            cand = t + (jnp.int32(1) << (31 - b))
            cnt = count_where(lambda k, c, j: k >= cand)
            t = jnp.where((cnt >= topk) & (done == 0), cand, t)
            done = jnp.where(cnt == topk, 1, done)
            return b + 1, t, done

        settled = jnp.where(pos + 1 <= topk, 1, 0)
        _, thr, done = lax.while_loop(bit_cond, bit_body,
                                      (jnp.int32(0), jnp.full((rg, 1), INT_MIN, I32), settled))
        thr = jnp.maximum(thr, KEY_NEG_INF + 1)

        def break_ties(need):
            def idx_of(c, j):
                return c * kc + j * 128 + lax.broadcasted_iota(I32, (1, 128), 1)

            def pos_body(b, m):
                cand = m + (jnp.int32(1) << (seq_bits - 1 - b))
                cnt = count_where(lambda k, c, j: (k == thr) & (idx_of(c, j) < cand))
                return jnp.where(cnt < need, cand, m)

            m = lax.fori_loop(0, seq_bits, pos_body, jnp.zeros((rg, 1), I32))

            def demote(c, carry):
                k = key_ref[c, r0:r0 + rg, :]
                idx = c * kc + lax.broadcasted_iota(I32, (1, kc), 1)
                key_ref[c, r0:r0 + rg, :] = jnp.where((k == thr) & (idx > m), thr - 1, k)
                return carry

            lax.fori_loop(0, nck, demote, 0)

        @pl.when(jnp.min(done) == 0)
        def _():
            c_gt = count_where(lambda k, c, j: k > thr)
            c_ge = count_where(lambda k, c, j: k >= thr)
            need = topk - c_gt

            @pl.when(jnp.max(c_ge - c_gt - need) > 0)
            def _():
                break_ties(need)

        return thr

    thr = jnp.concatenate([row_group_threshold(r0) for r0 in range(0, qb, rg)], axis=0)

    q = _dot(cq, wuq_ref[...]).astype(BF16)
    qlat = _dot(q, wukbd_ref[...]) * (D_NOPE ** -0.5 * LOG2E)
    for h in range(H_A):
        qlat_ref[h * qb:(h + 1) * qb, :] = qlat[:, h * R_KV:(h + 1) * R_KV].astype(BF16)
    m_ref[...] = jnp.full(m_ref.shape, -1e30, F32)
    l_ref[...] = jnp.zeros(l_ref.shape, F32)
    acc_ref[...] = jnp.zeros(acc_ref.shape, F32)

    def att_body(c, carry):
        kv = ckv_ref[0, pl.ds(pl.multiple_of(c * kc, kc), kc), :]
        bias = jnp.where(key_ref[c] >= thr, 0.0, NEG_INF)
        s = _dot_nt(qlat_ref[...], kv)
        s = (s.reshape(H_A, qb, kc) + bias[None]).reshape(H_A * qb, kc)
        tiles = [s[:, j * 128:(j + 1) * 128] for j in range(n_lt)]
        m_cur = tiles[0]
        for t in tiles[1:]:
            m_cur = jnp.maximum(m_cur, t)
        m_old = m_ref[...]
        m_new = jnp.maximum(m_old, jnp.max(m_cur, axis=1, keepdims=True))
        ps = [jnp.exp2(t - m_new) for t in tiles]
        a = jnp.exp2(m_old - m_new)
        psum = ps[0]
        for t in ps[1:]:
            psum = psum + t
        l_ref[...] = a * l_ref[...] + psum
        p = jnp.concatenate([t.astype(BF16) for t in ps], axis=1)
        acc_ref[...] = a * acc_ref[...] + _dot(p, kv)
        m_ref[...] = m_new
        return carry

    lax.fori_loop(0, nck, att_body, 0)
    o_all = acc_ref[...] / jnp.sum(l_ref[...], axis=1, keepdims=True)
    o_lat = jnp.concatenate([o_all[h * qb:(h + 1) * qb, :] for h in range(H_A)], axis=1)
    o_ref[0] = _dot(o_lat.astype(BF16), wuvbd_ref[...]).astype(o_ref.dtype)


def _block_diag(blocks):
    n = len(blocks)
    r, c = blocks[0].shape
    out = jnp.zeros((n * r, n * c), blocks[0].dtype)
    for k, blk in enumerate(blocks):
        out = out.at[k * r:(k + 1) * r, k * c:(k + 1) * c].set(blk)
    return out


def _dsa(cq, ckv, kidx, widx, w_uq, w_uk, w_uv, w_qidx, qb=256, kc=512):
    B, S, _ = cq.shape
    kc = min(kc, S)
    topk = min(TOPK_MAX, S // 4)
    assert S % kc == 0 and kc % qb == 0 and kc >= topk
    seq_bits = max(1, (S - 1).bit_length())
    wqidx = jnp.transpose(w_qidx, (1, 0, 2)).astype(BF16)
    wuq = w_uq.reshape(R_Q, H_A * D_NOPE).astype(BF16)
    wukbd = _block_diag([w_uk[:, h, :].T for h in range(H_A)]).astype(BF16)
    wuvbd = _block_diag([w_uv[:, h, :] for h in range(H_A)]).astype(BF16)
    kern = functools.partial(_dsa_kernel, qb=qb, kc=kc, topk=topk, seq_bits=seq_bits)
    blk_q = lambda b, i: (b, i, 0)
    seq = lambda b, i: (b, 0, 0)
    fix2 = lambda b, i: (0, 0)
    fix3 = lambda b, i: (0, 0, 0)
    return pl.pallas_call(
        kern,
        grid=(B, S // qb),
        in_specs=[pl.BlockSpec((1, qb, R_Q), blk_q), pl.BlockSpec((1, S, R_KV), seq),
                  pl.BlockSpec((1, S, D_IDX), seq), pl.BlockSpec((1, qb, H_IDX), blk_q),
                  pl.BlockSpec(wqidx.shape, fix3), pl.BlockSpec(wuq.shape, fix2),
                  pl.BlockSpec(wukbd.shape, fix2), pl.BlockSpec(wuvbd.shape, fix2)],
        out_specs=pl.BlockSpec((1, qb, H_A * D_VA), blk_q),
        out_shape=jax.ShapeDtypeStruct((B, S, H_A * D_VA), BF16),
        scratch_shapes=[pltpu.VMEM((S // kc, qb, kc), I32),
                        pltpu.VMEM((H_A * qb, R_KV), BF16),
                        pltpu.VMEM((H_A * qb, R_KV), F32), pltpu.VMEM((H_A * qb, R_KV), F32),
                        pltpu.VMEM((H_A * qb, R_KV), F32)],
        compiler_params=_cparams(("parallel", "arbitrary")),
        name="dsa_attention",
    )(cq, ckv, kidx, widx, wqidx, wuq, wukbd, wuvbd)


GDN_HALO = 8


def _softplus(x):
    return jnp.maximum(x, 0.0) + jnp.log1p(jnp.exp(-jnp.abs(x)))


def _gdn_kernel(qkvz_ref, abc_ref, abr_ref, convw_ref, prm_c_ref, prm_r_ref, ng_ref, tri_ref,
                o_ref, xbuf_ref, state_ref, conv_ref, *, cb):
    C = DN_CHUNK
    W = H_B * D_B
    j = pl.program_id(1)

    @pl.when(j == 0)
    def _():
        xbuf_ref[0:GDN_HALO, :] = jnp.zeros((GDN_HALO, 3 * W), F32)
        state_ref[...] = jnp.zeros(state_ref.shape, F32)

    xbuf_ref[GDN_HALO:, :] = qkvz_ref[0, :, 0:3 * W]
    acc = xbuf_ref[GDN_HALO:, :] * convw_ref[CONV_K - 1:CONV_K, :]
    for t in range(CONV_K - 1):
        sh = CONV_K - 1 - t
        acc = acc + xbuf_ref[pl.ds(GDN_HALO - sh, cb), :] * convw_ref[t:t + 1, :]
    conv_ref[...] = _silu(acc)
    xbuf_ref[0:GDN_HALO, :] = xbuf_ref[cb:cb + GDN_HALO, :]

    abc = abc_ref[0]
    abr = abr_ref[0]
    beta_c = _sigmoid(abc)
    g_c = -jnp.exp(prm_r_ref[0:1, :]) * _softplus(abc + prm_r_ref[1:2, :])
    g_r = -jnp.exp(prm_c_ref[:, 0:1]) * _softplus(abr + prm_c_ref[:, 1:2])
    gc_c = _dot(tri_ref[...], g_c, precision=HIGHEST)
    gc_r = _dot_nt(g_r, tri_ref[...], precision=HIGHEST)

    ri = lax.broadcasted_iota(I32, (cb, cb), 0)
    ci = lax.broadcasted_iota(I32, (cb, cb), 1)
    same = (ri // C) == (ci // C)
    lower = same & (ri >= ci)
    strict = same & (ri > ci)
    eye = jnp.where(ri == ci, 1.0, 0.0)
    bf = lambda t: t.astype(BF16)

    for h in range(H_B):
        q = conv_ref[:, h * D_B:(h + 1) * D_B]
        k = conv_ref[:, W + h * D_B:W + (h + 1) * D_B]
        v = conv_ref[:, 2 * W + h * D_B:2 * W + (h + 1) * D_B]
        q = q * lax.rsqrt(jnp.sum(q * q, axis=-1, keepdims=True) + 1e-6) * (D_B ** -0.5)
        k = k * lax.rsqrt(jnp.sum(k * k, axis=-1, keepdims=True) + 1e-6)
        beta = beta_c[:, h:h + 1]
        gcol = gc_c[:, H_B + h:H_B + h + 1]
        grow = gc_r[H_B + h:H_B + h + 1, :]
        decay = jnp.exp(jnp.where(lower, gcol - grow, NEG_INF))
        kb = k * beta
        vb = v * beta
        kq = bf(k)
        a_mat = jnp.where(strict, _dot_nt(bf(kb), kq) * decay, 0.0)
        xm = -a_mat
        t_mat = eye + xm
        for _ in range(int(math.log2(C)) - 1):
            xq = bf(xm)
            xm = _dot(xq, xq)
            t_mat = t_mat + _dot(bf(t_mat), bf(xm))
        egc = jnp.exp(gcol)
        tq = bf(t_mat)
        u = _dot(tq, bf(vb))
        w = bf(_dot(tq, bf(kb * egc)))
        qk = bf(jnp.where(lower, _dot_nt(bf(q), kq) * decay, 0.0))
        q_dec = bf(q * egc)
        for n in range(cb // C):
            r0 = n * C
            glast = gcol[r0 + C - 1:r0 + C, :]
            k_dec = bf(k[r0:r0 + C] * jnp.exp(glast - gcol[r0:r0 + C]))
            st = state_ref[h]
            stq = bf(st)
            v_new = u[r0:r0 + C] - _dot(w[r0:r0 + C], stq)
            vq = bf(v_new)
            o = _dot(q_dec[r0:r0 + C], stq) + _dot(qk[r0:r0 + C, r0:r0 + C], vq)
            state_ref[h] = st * jnp.exp(glast) + _dot_tn(k_dec, vq)
            o = o * lax.rsqrt(jnp.mean(o * o, axis=-1, keepdims=True) + 1e-6) * ng_ref[...]
            z = qkvz_ref[0, r0:r0 + C, 3 * W + h * D_B:3 * W + (h + 1) * D_B]
            o_ref[0, r0:r0 + C, h * D_B:(h + 1) * D_B] = (o * _silu(z)).astype(o_ref.dtype)


def _gdn(qkvz, b, a, conv_w, a_log, dt_bias, norm_g, cb=256):
    B, S, _ = qkvz.shape
    W = H_B * D_B
    cb = min(cb, S)
    abc = jnp.concatenate([b, a], axis=-1)
    abr = jnp.transpose(abc, (0, 2, 1))
    zeros = jnp.zeros((H_B,), F32)
    prm = jnp.stack([jnp.concatenate([zeros, a_log]), jnp.concatenate([zeros, dt_bias])])
    idx = jnp.arange(cb)
    tri = ((idx[:, None] >= idx[None, :]) & (idx[:, None] // DN_CHUNK == idx[None, :] // DN_CHUNK)).astype(F32)
    kern = functools.partial(_gdn_kernel, cb=cb)
    blk = lambda bi, j: (bi, j, 0)
    fix = lambda bi, j: (0, 0)
    return pl.pallas_call(
        kern,
        grid=(B, S // cb),
        in_specs=[pl.BlockSpec((1, cb, 4 * W), blk), pl.BlockSpec((1, cb, 2 * H_B), blk),
                  pl.BlockSpec((1, 2 * H_B, cb), lambda bi, j: (bi, 0, j)),
                  pl.BlockSpec((CONV_K, 3 * W), fix), pl.BlockSpec((2 * H_B, 2), fix),
                  pl.BlockSpec((2, 2 * H_B), fix), pl.BlockSpec((1, D_B), fix),
                  pl.BlockSpec((cb, cb), fix)],
        out_specs=pl.BlockSpec((1, cb, W), blk),
        out_shape=jax.ShapeDtypeStruct((B, S, W), BF16),
        scratch_shapes=[pltpu.VMEM((cb + GDN_HALO, 3 * W), F32), pltpu.VMEM((H_B, D_B, D_B), F32),
                        pltpu.VMEM((cb, 3 * W), F32)],
        compiler_params=_cparams(("parallel", "arbitrary")),
        name="gated_deltanet",
    )(qkvz, abc, abr, conv_w, prm.T, prm, norm_g.reshape(1, D_B), tri)


def _store_row_tiles(ref, val):
    n, d = val.shape
    sub = d // LANES
    for j in range(sub):
        ref[pl.ds(j, n, stride=sub), :] = val[:, j * LANES:(j + 1) * LANES]


def _load_row_tiles(ref, n, sub):
    return jnp.concatenate([ref[pl.ds(j, n, stride=sub), :] for j in range(sub)], axis=1)


def _post_mixer_kernel(a1_ref, a2_ref, h_ref, w1_ref, w2_ref, g_ref, b_ref, rw_ref, rb_ref, triu_ref,
                       h1_ref, h1t_ref, tope_ref, gate_ref, rank_ref, cnt_ref, carry_ref):
    i = pl.program_id(0)
    E, tm = rw_ref.shape[0], h_ref.shape[0]

    @pl.when(i == 0)
    def _():
        carry_ref[...] = jnp.zeros(carry_ref.shape, F32)

    mix = _dot(a1_ref[...].astype(BF16), w1_ref[...]) + _dot(a2_ref[...].astype(BF16), w2_ref[...])
    h1 = _layernorm_rows(ALPHA * h_ref[...] + mix, g_ref[...], b_ref[...])
    h1_ref[...] = h1
    _store_row_tiles(h1t_ref, h1)

    logits = _dot_nt(rw_ref[...], h1, precision=HIGHEST) + rb_ref[...]
    erow = lax.broadcasted_iota(I32, (E, tm), 0)
    sel = jnp.zeros((E, tm), F32)
    onehots, tops = [], []
    for k in range(TOP_K):
        mx = jnp.max(logits, axis=0, keepdims=True)
        idx = jnp.min(jnp.where(logits == mx, erow, E), axis=0, keepdims=True)
        oh = erow == idx
        logits = jnp.where(oh, NEG_INF, logits)
        sel = sel + jnp.where(oh, 1.0, 0.0)
        onehots.append(oh)
        tops.append(mx)
        tope_ref[k:k + 1, :] = idx
    exps = [jnp.exp(t - tops[0]) for t in tops]
    den = exps[0] + exps[1] + exps[2] + exps[3]
    for k in range(TOP_K):
        gate_ref[k:k + 1, :] = exps[k] / den
    incl = _dot(sel.astype(BF16), triu_ref[...])
    excl = incl - sel + carry_ref[:, 0:1]
    for k in range(TOP_K):
        rank_ref[k:k + 1, :] = jnp.sum(jnp.where(onehots[k], excl, 0.0), axis=0, keepdims=True).astype(I32)
    carry_ref[...] = carry_ref[...] + jnp.sum(sel, axis=1, keepdims=True)
    cnt_ref[...] = carry_ref[...]


def _post_mixer(a1, a2, h2d, w_out, ln_g, ln_b, router_w, router_b, tm=512):
    T, D = h2d.shape
    E = router_w.shape[1]
    n1, n2 = a1.shape[1], a2.shape[1]
    w1 = w_out[:n1].astype(BF16)
    w2 = w_out[n1:].astype(BF16)
    idx = jnp.arange(tm)
    triu = (idx[:, None] <= idx[None, :]).astype(BF16)
    row = lambda i: (i, 0)
    col = lambda i: (0, i)
    fix = lambda i: (0, 0)
    return pl.pallas_call(
        _post_mixer_kernel,
        grid=(T // tm,),
        in_specs=[pl.BlockSpec((tm, n1), row), pl.BlockSpec((tm, n2), row), pl.BlockSpec((tm, D), row),
                  pl.BlockSpec((n1, D), fix), pl.BlockSpec((n2, D), fix),
                  pl.BlockSpec((1, D), fix), pl.BlockSpec((1, D), fix),
                  pl.BlockSpec((E, D), fix), pl.BlockSpec((E, 1), fix), pl.BlockSpec((tm, tm), fix)],
        out_specs=[pl.BlockSpec((tm, D), row), pl.BlockSpec((tm * (D // LANES), LANES), row),
                   pl.BlockSpec((TOP_K, tm), col), pl.BlockSpec((TOP_K, tm), col),
                   pl.BlockSpec((TOP_K, tm), col), pl.BlockSpec((E, 128), fix)],
        out_shape=[jax.ShapeDtypeStruct((T, D), F32), jax.ShapeDtypeStruct((T * (D // LANES), LANES), F32),
                   jax.ShapeDtypeStruct((TOP_K, T), I32), jax.ShapeDtypeStruct((TOP_K, T), F32),
                   jax.ShapeDtypeStruct((TOP_K, T), I32), jax.ShapeDtypeStruct((E, 128), F32)],
        scratch_shapes=[pltpu.VMEM((E, 128), F32)],
        compiler_params=_cparams(("arbitrary",)),
        name="post_mixer_router",
    )(a1, a2, h2d, w1, w2, ln_g.reshape(1, D), ln_b.reshape(1, D), router_w.T, router_b.reshape(E, 1), triu)


def _dispatch_kernel(dest_ref, h_ref, xs_init_hbm, xs_hbm, sem, *, tt, n_tok, sub):
    del xs_init_hbm
    base = pl.program_id(0) * tt

    def row_copy(r, k):
        d = dest_ref[k * n_tok + base + r]
        return pltpu.make_async_copy(h_ref.at[pl.ds(pl.multiple_of(r * sub, sub), sub)],
                                     xs_hbm.at[pl.ds(pl.multiple_of(d * sub, sub), sub)], sem)

    def start(r, c):
        for k in range(TOP_K):
            row_copy(r, k).start(priority=k % 2)
        return c

    def wait(r, c):
        for k in range(TOP_K):
            row_copy(r, k).wait()
        return c

    lax.fori_loop(0, tt, start, 0)
    lax.fori_loop(0, tt, wait, 0)


def _moe_dispatch(h1t, dest_flat, n_slots, sub, tt=512):
    T = h1t.shape[0] // sub
    any_spec = pl.BlockSpec(memory_space=pl.ANY)
    return pl.pallas_call(
        functools.partial(_dispatch_kernel, tt=tt, n_tok=T, sub=sub),
        grid_spec=pltpu.PrefetchScalarGridSpec(
            num_scalar_prefetch=1, grid=(T // tt,),
            in_specs=[pl.BlockSpec((tt * sub, LANES), lambda i, d: (i, 0)), any_spec], out_specs=any_spec,
            scratch_shapes=[pltpu.SemaphoreType.DMA(())]),
        out_shape=jax.ShapeDtypeStruct((n_slots * sub, LANES), h1t.dtype),
        input_output_aliases={2: 0},
        compiler_params=_cparams(("arbitrary",)),
        name="moe_dispatch",
    )(dest_flat, h1t, jnp.zeros((n_slots * sub, LANES), h1t.dtype))


MOE_BLOCK = 256


def _moe_kernel(be_ref, nu_ref, x_ref, wgu_ref, bg_ref, bu_ref, wd_ref, bd_ref, y_ref,
                wt_ref, wg_ref, wu_ref, wdb_ref, *, bm, sub):
    i = pl.program_id(0)
    e = be_ref[i]
    F = wd_ref.shape[2]

    @pl.when((i < nu_ref[0]) & ((i == 0) | (e != be_ref[jnp.maximum(i - 1, 0)])))
    def _():
        ck = 512
        for c in range(0, 2 * F, ck):
            wt = wgu_ref[0, 0, :, c:c + ck].T
            for j in range(sub):
                wt_ref[pl.ds(c * sub + j, ck, stride=sub), :] = wt[:, j * LANES:(j + 1) * LANES]
        for j in range(sub):
            cols = slice(j * LANES, (j + 1) * LANES)
            wg_ref[:, cols] = wt_ref[pl.ds(j, F, stride=2 * sub), :].astype(BF16)
            wu_ref[:, cols] = wt_ref[pl.ds(sub + j, F, stride=2 * sub), :].astype(BF16)
        wdb_ref[...] = wd_ref[0, 0].astype(BF16)

    @pl.when(i < nu_ref[0])
    def _():
        x = _load_row_tiles(x_ref, bm, sub).astype(BF16)
        g = _dot_nt(x, wg_ref[...]) + bg_ref[0]
        u = _dot_nt(x, wu_ref[...]) + bu_ref[0]
        gt = jnp.minimum(g, SWIGLU_LIMIT)
        up = jnp.clip(u, -SWIGLU_LIMIT, SWIGLU_LIMIT)
        hid = (up + 1.0) * (gt * _sigmoid(gt * SWIGLU_ALPHA))
        _store_row_tiles(y_ref, _dot(hid.astype(BF16), wdb_ref[...]) + bd_ref[0])

    @pl.when(i >= nu_ref[0])
    def _():
        y_ref[...] = jnp.zeros(y_ref.shape, y_ref.dtype)


def _moe_experts(xs, blk_e, n_used, w_gu, bg, bu, w_down, bd, sub, layer):
    _, E, D, F2 = w_gu.shape
    F = F2 // 2
    bm = MOE_BLOCK
    P = xs.shape[0] // sub
    wsel = lambda i, be, nu: (be[i], 0, 0)
    wsel_l = lambda i, be, nu: (layer, be[i], 0, 0)
    row = lambda i, be, nu: (i, 0)
    return pl.pallas_call(
        functools.partial(_moe_kernel, bm=bm, sub=sub),
        grid_spec=pltpu.PrefetchScalarGridSpec(
            num_scalar_prefetch=2,
            grid=(P // bm,),
            in_specs=[pl.BlockSpec((bm * sub, LANES), row),
                      pl.BlockSpec((1, 1, D, F2), wsel_l),
                      pl.BlockSpec((1, 1, F), wsel), pl.BlockSpec((1, 1, F), wsel),
                      pl.BlockSpec((1, 1, F, D), wsel_l), pl.BlockSpec((1, 1, D), wsel)],
            out_specs=pl.BlockSpec((bm * sub, LANES), row),
            scratch_shapes=[pltpu.VMEM((F2 * sub, LANES), F32), pltpu.VMEM((F, D), BF16), pltpu.VMEM((F, D), BF16),
                            pltpu.VMEM((F, D), BF16)]),
        out_shape=jax.ShapeDtypeStruct((P * sub, LANES), F32),
        compiler_params=_cparams(("arbitrary",)),
        name="moe_experts",
    )(blk_e, n_used, xs, w_gu, bg, bu, w_down, bd)


def _moe(h1t, top_e, rank, cnt, w_gu, b_gu, w_down, b_down, layer):
    E, D = w_gu.shape[1], w_gu.shape[2]
    sub = D // LANES
    T = h1t.shape[0] // sub
    bm = MOE_BLOCK
    counts = cnt[:, 0].astype(I32)
    padded = (counts + bm - 1) // bm * bm
    pad_end = jnp.cumsum(padded)
    pad_start = pad_end - padded
    e_ids = jnp.arange(E, dtype=I32)[:, None, None]
    start_of = jnp.sum(jnp.where(top_e[None] == e_ids, pad_start[:, None, None], 0), axis=0)
    dest = (start_of + rank).reshape(-1)
    nblk = (T * TOP_K) // bm + E
    blk_first = jnp.arange(nblk, dtype=I32) * bm
    blk_e = jnp.minimum(jnp.sum(pad_end[None, :] <= blk_first[:, None], axis=1), E - 1).astype(I32)
    n_used = (pad_end[-1:] // bm).astype(I32)
    xs = _moe_dispatch(h1t, dest, nblk * bm, sub)
    ys = _moe_experts(xs, blk_e, n_used, w_gu, b_gu[:, None, 0::2], b_gu[:, None, 1::2],
                      w_down, b_down[:, None, :], sub, layer)
    return ys, dest


def _tail_kernel(dest_ref, h1_ref, gate_ref, p_ref, wg_ref, wp_ref, g_ref, b_ref, ys_hbm, o_ref,
                 ybuf, sem, *, tm, n_tok, sub):
    i = pl.program_id(0)

    def row_copy(tile, slot, r, k):
        d = dest_ref[k * n_tok + tile * tm + r]
        return pltpu.make_async_copy(ys_hbm.at[pl.ds(pl.multiple_of(d * sub, sub), sub)],
                                     ybuf.at[slot, k, pl.ds(pl.multiple_of(r * sub, sub), sub)], sem.at[slot])

    def start_tile(tile, slot):
        def body(r, c):
            for k in range(TOP_K):
                row_copy(tile, slot, r, k).start(priority=k % 2)
            return c
        lax.fori_loop(0, tm, body, 0)

    def wait_tile(tile, slot):
        def body(r, c):
            for k in range(TOP_K):
                row_copy(tile, slot, r, k).wait()
            return c
        lax.fori_loop(0, tm, body, 0)

    @pl.when(i == 0)
    def _():
        start_tile(0, 0)

    h1 = h1_ref[...]
    ple = _sigmoid(_dot(h1.astype(BF16), wg_ref[...])) * _dot(p_ref[...].astype(BF16), wp_ref[...])
    x = ALPHA * h1 + ple

    for slot in range(2):
        @pl.when(i % 2 == slot)
        def _(slot=slot):
            @pl.when(i + 1 < pl.num_programs(0))
            def _():
                start_tile(i + 1, 1 - slot)

            wait_tile(i, slot)
            cols = []
            for j in range(sub):
                col = ybuf[slot, 0, pl.ds(j, tm, stride=sub), :] * gate_ref[:, 0:1]
                for k in range(1, TOP_K):
                    col = col + ybuf[slot, k, pl.ds(j, tm, stride=sub), :] * gate_ref[:, k:k + 1]
                cols.append(col)
            ffn = jnp.concatenate(cols, axis=1)
            o_ref[...] = _layernorm_rows(x + ffn, g_ref[...], b_ref[...])


def _layer_tail(h1, ys, dest_flat, gate_t, p2d, ple_w_gate, ple_w_proj, ln_g, ln_b, tm=256):
    T, D = h1.shape
    PD = p2d.shape[1]
    sub = D // LANES
    row = lambda i, d: (i, 0)
    fix = lambda i, d: (0, 0)
    return pl.pallas_call(
        functools.partial(_tail_kernel, tm=tm, n_tok=T, sub=sub),
        grid_spec=pltpu.PrefetchScalarGridSpec(
            num_scalar_prefetch=1, grid=(T // tm,),
            in_specs=[pl.BlockSpec((tm, D), row), pl.BlockSpec((tm, TOP_K), row),
                      pl.BlockSpec((tm, PD), row), pl.BlockSpec((D, D), fix), pl.BlockSpec((PD, D), fix),
                      pl.BlockSpec((1, D), fix), pl.BlockSpec((1, D), fix),
                      pl.BlockSpec(memory_space=pl.ANY)],
            out_specs=pl.BlockSpec((tm, D), row),
            scratch_shapes=[pltpu.VMEM((2, TOP_K, tm * sub, LANES), F32), pltpu.SemaphoreType.DMA((2,))]),
        out_shape=jax.ShapeDtypeStruct((T, D), F32),
        compiler_params=_cparams(("arbitrary",)),
        name="layer_tail",
    )(dest_flat, h1, gate_t.T, p2d, ple_w_gate.astype(BF16), ple_w_proj.astype(BF16),
      ln_g.reshape(1, D), ln_b.reshape(1, D), ys)


def _cd_proj_kernel(x_ref, w_ref, q_ref, k_ref, v_ref, u_ref):
    x = x_ref[...].astype(BF16)
    n = H_C * DH_C
    q_ref[...] = (_dot(x, w_ref[:, 0:n]) * (DH_C ** -0.5 * LOG2E)).astype(q_ref.dtype)
    k_ref[...] = _dot(x, w_ref[:, n:2 * n]).astype(k_ref.dtype)
    v_ref[...] = _dot(x, w_ref[:, 2 * n:3 * n]).astype(v_ref.dtype)
    u_ref[...] = _dot(x, w_ref[:, 3 * n:])


def _cd_in_proj(h2d, w_in, tm=512):
    T, D = h2d.shape
    n = H_C * DH_C
    nu = w_in.shape[1] - 3 * n
    row = lambda i: (i, 0)
    return pl.pallas_call(
        _cd_proj_kernel,
        grid=(T // tm,),
        in_specs=[pl.BlockSpec((tm, D), row), pl.BlockSpec(w_in.shape, lambda i: (0, 0))],
        out_specs=[pl.BlockSpec((tm, n), row)] * 3 + [pl.BlockSpec((tm, nu), row)],
        out_shape=[jax.ShapeDtypeStruct((T, n), BF16)] * 3 + [jax.ShapeDtypeStruct((T, nu), F32)],
        compiler_params=_cparams(("parallel",)),
        name="cd_in_proj",
    )(h2d, w_in.astype(BF16))


def _dilated_bias_table(qb):
    import numpy as np
    max_w = max(w for w, _ in DILATED_PATTERNS)
    ndc = max_w // qb + 1
    r = np.arange(qb)[:, None]
    j = np.arange(qb)[None, :]
    tbl = np.empty((ndc, qb, qb), np.float32)
    for dc in range(ndc):
        delta = dc * qb + r - j
        mult = np.zeros((qb, qb), np.float64)
        for w, d in DILATED_PATTERNS:
            mult += (delta >= 0) & (delta <= w) & (delta % d == 0)
        with np.errstate(divide="ignore"):
            tbl[dc] = np.log2(mult)
    return jnp.asarray(tbl)


def _dilated_kernel(q_ref, k_ref, v_ref, bias_ref, o_ref, q2_ref, m_ref, l_ref, acc_ref, *, qb, ndc):
    i = pl.program_id(2)
    n_lt = qb // 128
    q = q_ref[0]
    lo = lax.broadcasted_iota(I32, q.shape, 1) < DH_C
    zero = jnp.zeros(q.shape, q.dtype)
    q2_ref[0:qb, :] = jnp.where(lo, q, zero)
    q2_ref[qb:, :] = jnp.where(lo, zero, q)
    m_ref[...] = jnp.full(m_ref.shape, -1e30, F32)
    l_ref[...] = jnp.zeros(l_ref.shape, F32)
    acc_ref[...] = jnp.zeros(acc_ref.shape, F32)

    def body(c, carry):
        r0 = pl.multiple_of(c * qb, qb)
        kk = k_ref[0, pl.ds(r0, qb), :]
        vv = v_ref[0, pl.ds(r0, qb), :]
        s = _dot_nt(q2_ref[...], kk)
        s = (s.reshape(2, qb, qb) + bias_ref[i - c][None]).reshape(2 * qb, qb)
        tiles = [s[:, j * 128:(j + 1) * 128] for j in range(n_lt)]
        m_cur = tiles[0]
        for t in tiles[1:]:
            m_cur = jnp.maximum(m_cur, t)
        m_old = m_ref[...]
        m_new = jnp.maximum(m_old, jnp.max(m_cur, axis=1, keepdims=True))
        ps = [jnp.exp2(t - m_new) for t in tiles]
        a = jnp.exp2(m_old - m_new)
        psum = ps[0]
        for t in ps[1:]:
            psum = psum + t
        l_ref[...] = a * l_ref[...] + psum
        p = jnp.concatenate([t.astype(BF16) for t in ps], axis=1)
        acc_ref[...] = a * acc_ref[...] + _dot(p, vv)
        m_ref[...] = m_new
        return carry

    lax.fori_loop(jnp.maximum(i - (ndc - 1), 0), i + 1, body, 0)
    o_all = acc_ref[...] / jnp.sum(l_ref[...], axis=1, keepdims=True)
    o_ref[0] = jnp.where(lo, o_all[0:qb, :], o_all[qb:, :]).astype(o_ref.dtype)


def _dilated_attention(q, k, v, qb=512):
    B, S, n = q.shape
    qb = min(qb, S)
    bias = _dilated_bias_table(qb)
    ndc = bias.shape[0]
    pw = 2 * DH_C
    kern = functools.partial(_dilated_kernel, qb=qb, ndc=ndc)
    blk = lambda b, pr, i: (b, i, pr)
    seq = lambda b, pr, i: (b, 0, pr)
    return pl.pallas_call(
        kern,
        grid=(B, n // pw, S // qb),
        in_specs=[pl.BlockSpec((1, qb, pw), blk), pl.BlockSpec((1, S, pw), seq),
                  pl.BlockSpec((1, S, pw), seq), pl.BlockSpec(bias.shape, lambda b, pr, i: (0, 0, 0))],
        out_specs=pl.BlockSpec((1, qb, pw), blk),
        out_shape=jax.ShapeDtypeStruct((B, S, n), BF16),
        scratch_shapes=[pltpu.VMEM((2 * qb, pw), BF16), pltpu.VMEM((2 * qb, pw), F32),
                        pltpu.VMEM((2 * qb, pw), F32), pltpu.VMEM((2 * qb, pw), F32)],
        compiler_params=_cparams(("parallel", "parallel", "arbitrary")),
        name="dilated_attention",
    )(q, k, v, bias)


POOL_HALO = 16


def _pool_kernel(u_ref, w_ref, sc_ref, o_ref, xbuf_ref, *, tm):
    j = pl.program_id(1)

    @pl.when(j == 0)
    def _():
        xbuf_ref[0:POOL_HALO, :] = jnp.zeros((POOL_HALO, xbuf_ref.shape[1]), F32)

    xbuf_ref[POOL_HALO:, :] = u_ref[0]
    x = xbuf_ref[POOL_HALO:, :]
    grp = lax.broadcasted_iota(I32, (1, x.shape[1]), 1) // POOL_GROUP
    run = x
    sel = jnp.zeros(x.shape, F32)
    win = jnp.zeros((1, x.shape[1]), F32)
    for d in range(1, max(POOL_WINDOWS)):
        run = run + xbuf_ref[pl.ds(POOL_HALO - d, tm), :]
        if d + 1 in POOL_WINDOWS:
            gi = POOL_WINDOWS.index(d + 1)
            sel = jnp.where(grp == gi, run, sel)
            win = jnp.where(grp == gi, float(d + 1), win)
    pos = j * tm + lax.broadcasted_iota(I32, (tm, 1), 0)
    mean = sel / jnp.minimum((pos + 1).astype(F32), win)
    o_ref[0] = _dot((mean - x).astype(BF16), w_ref[...]) * sc_ref[...]
    xbuf_ref[0:POOL_HALO, :] = xbuf_ref[tm:tm + POOL_HALO, :]


def _multiscale_pool(u, pool_w, pool_scale, tm=512):
    B, S, n = u.shape
    tm = min(tm, S)
    wbd = _block_diag([pool_w[g] for g in range(pool_w.shape[0])]).astype(BF16)
    blk = lambda b, j: (b, j, 0)
    fix = lambda b, j: (0, 0)
    return pl.pallas_call(
        functools.partial(_pool_kernel, tm=tm),
        grid=(B, S // tm),
        in_specs=[pl.BlockSpec((1, tm, n), blk), pl.BlockSpec((n, n), fix), pl.BlockSpec((1, n), fix)],
        out_specs=pl.BlockSpec((1, tm, n), blk),
        out_shape=jax.ShapeDtypeStruct((B, S, n), F32),
        scratch_shapes=[pltpu.VMEM((tm + POOL_HALO, n), F32)],
        compiler_params=_cparams(("parallel", "arbitrary")),
        name="multiscale_pool",
    )(u, wbd, pool_scale.reshape(1, n))


def kernel(x, p, ab_w_in, ab_q_norm_g, ab_kv_norm_g, ab_w_uq, ab_w_uk, ab_w_uv, ab_w_qidx,
           ab_kidx_norm_g, ab_kidx_norm_b, ab_conv_w, ab_a_log, ab_dt_bias, ab_out_norm_g, ab_w_out,
           cd_w_in, cd_pool_w, cd_pool_scale, cd_w_out, ln_mix_g, ln_mix_b, router_w, router_b,
           w_gu, b_gu, w_down, b_down, ple_w_proj, ple_w_gate, ln_ffn_g, ln_ffn_b):
    B, S, D = x.shape
    T = B * S
    h = x.reshape(T, D)
    for i in range(DEPTH):
        j = i // 2
        if i % 2 == 0:
            cq, ckv, kidx, small, qkvz = _ab_in_proj(h, ab_w_in[j], ab_q_norm_g[j], ab_kv_norm_g[j],
                                                     ab_kidx_norm_g[j], ab_kidx_norm_b[j])
            sh = lambda t: t.reshape(B, S, -1)
            o_w = D_IDX + H_IDX
            o_a = _dsa(sh(cq), sh(ckv), sh(kidx), sh(small[:, D_IDX:o_w]),
                       ab_w_uq[j], ab_w_uk[j], ab_w_uv[j], ab_w_qidx[j])
            o_b = _gdn(sh(qkvz), sh(small[:, o_w:o_w + H_B]), sh(small[:, o_w + H_B:o_w + 2 * H_B]),
                       ab_conv_w[j], ab_a_log[j], ab_dt_bias[j], ab_out_norm_g[j])
            a1, a2, w_out = o_a.reshape(T, -1), o_b.reshape(T, -1), ab_w_out[j]
        else:
            q, k, v, u = _cd_in_proj(h, cd_w_in[j])
            sh = lambda t: t.reshape(B, S, -1)
            o_c = _dilated_attention(sh(q), sh(k), sh(v))
            o_d = _multiscale_pool(sh(u), cd_pool_w[j], cd_pool_scale[j])
            a1, a2, w_out = o_c.reshape(T, -1), o_d.reshape(T, -1), cd_w_out[j]
        h1, h1t, top_e, gate_t, rank, cnt = _post_mixer(a1, a2, h, w_out, ln_mix_g[i], ln_mix_b[i],
                                                        router_w[i], router_b[i])
        ys, dest = _moe(h1t, top_e, rank, cnt, w_gu, b_gu[i], w_down, b_down[i], i)
        h = _layer_tail(h1, ys, dest, gate_t, p[i].reshape(T, -1), ple_w_gate[i], ple_w_proj[i],
                        ln_ffn_g[i], ln_ffn_b[i])
    return h.reshape(B, S, D)
```

```python
import functools
import math

import jax
import jax.numpy as jnp
from jax import lax
from jax.experimental import pallas as pl
from jax.experimental.pallas import tpu as pltpu

F32 = jnp.float32
BF16 = jnp.bfloat16
I32 = jnp.int32

DEPTH = 2
H_A, D_NOPE, D_VA, R_Q, R_KV, H_IDX, D_IDX = 8, 64, 64, 256, 128, 8, 64
TOPK_MAX = 256
H_B, D_B, CONV_K, DN_CHUNK = 4, 128, 4, 64
H_C, DH_C = 12, 64
DILATED_PATTERNS = ((128, 1), (512, 4), (2048, 16))
POOL_WINDOWS = (2, 4, 8, 16)
POOL_GROUP = 64
N_EXPERTS, TOP_K = 32, 4
SWIGLU_LIMIT, SWIGLU_ALPHA = 7.0, 1.702
ALPHA = (2 * DEPTH) ** 0.25

VMEM_LIMIT_BYTES = 56 * 1024 * 1024
LANES = 128
HIGHEST = lax.Precision.HIGHEST
NEG_INF = float("-inf")
LOG2E = math.log2(math.e)
INT_MIN = -2 ** 31
KEY_NEG_INF = (0xFF800000 ^ 0x7FFFFFFF) - 2 ** 32


def _cparams(sem):
    return pltpu.CompilerParams(dimension_semantics=sem, vmem_limit_bytes=VMEM_LIMIT_BYTES)


def _dot(a, b, precision=None):
    return jnp.dot(a, b, preferred_element_type=F32, precision=precision)


def _dot_nt(a, b, precision=None):
    return lax.dot_general(a, b, (((1,), (1,)), ((), ())), preferred_element_type=F32,
                           precision=precision)


def _dot_tn(a, b, precision=None):
    return lax.dot_general(a, b, (((0,), (0,)), ((), ())), preferred_element_type=F32,
                           precision=precision)


def _sigmoid(x):
    return 1.0 / (1.0 + jnp.exp(-x))


def _silu(x):
    return x * _sigmoid(x)


def _layernorm_rows(x, g, b, eps=1e-5):
    mu = jnp.mean(x, axis=-1, keepdims=True)
    xc = x - mu
    var = jnp.mean(xc * xc, axis=-1, keepdims=True)
    return xc * lax.rsqrt(var + eps) * g + b


AB_SMALL = 128


def _ab_proj_kernel(x_ref, w_ref, gq_ref, gkv_ref, gk_ref, bk_ref,
                    cq_ref, ckv_ref, kidx_ref, small_ref, qkvz_ref):
    x = x_ref[...].astype(BF16)
    cq = _dot(x, w_ref[:, 0:R_Q])
    cq = cq * lax.rsqrt(jnp.mean(cq * cq, axis=-1, keepdims=True) + 1e-6) * gq_ref[...]
    cq_ref[...] = cq.astype(cq_ref.dtype)
    ckv = _dot(x, w_ref[:, R_Q:R_Q + R_KV])
    ckv = ckv * lax.rsqrt(jnp.mean(ckv * ckv, axis=-1, keepdims=True) + 1e-6) * gkv_ref[...]
    ckv_ref[...] = ckv.astype(ckv_ref.dtype)
    off = R_Q + R_KV
    sm = _dot(x, w_ref[:, off:off + AB_SMALL])
    small_ref[...] = sm
    lane = lax.broadcasted_iota(I32, sm.shape, 1)
    is_k = lane < D_IDX
    mu = jnp.sum(jnp.where(is_k, sm, 0.0), axis=-1, keepdims=True) * (1.0 / D_IDX)
    xc = jnp.where(is_k, sm - mu, 0.0)
    var = jnp.sum(xc * xc, axis=-1, keepdims=True) * (1.0 / D_IDX)
    kn = xc * lax.rsqrt(var + 1e-5) * gk_ref[...] + bk_ref[...]
    kidx_ref[...] = kn[:, :D_IDX].astype(kidx_ref.dtype)
    off += AB_SMALL
    qkvz_ref[...] = _dot(x, w_ref[:, off:])


def _ab_in_proj(h2d, w_in, q_norm_g, kv_norm_g, kidx_g, kidx_b, tm=512):
    T, D = h2d.shape
    W = H_B * D_B
    o = [0, R_Q, R_Q + R_KV, R_Q + R_KV + D_IDX, R_Q + R_KV + D_IDX + H_IDX]
    o_q = o[4]
    o_b = o_q + 4 * W
    pad = AB_SMALL - (D_IDX + H_IDX + 2 * H_B)
    w_perm = jnp.concatenate([
        w_in[:, o[0]:o[2]],
        w_in[:, o[2]:o[4]], w_in[:, o_b:o_b + 2 * H_B],
        jnp.zeros((D, pad), w_in.dtype),
        w_in[:, o_q:o_b],
    ], axis=1).astype(BF16)
    n_all = w_perm.shape[1]
    gk = jnp.zeros((1, AB_SMALL), F32).at[0, :D_IDX].set(kidx_g)
    bk = jnp.zeros((1, AB_SMALL), F32).at[0, :D_IDX].set(kidx_b)
    row = lambda i: (i, 0)
    fixed = lambda i: (0, 0)
    return pl.pallas_call(
        _ab_proj_kernel,
        grid=(T // tm,),
        in_specs=[pl.BlockSpec((tm, D), row), pl.BlockSpec((D, n_all), fixed),
                  pl.BlockSpec((1, R_Q), fixed), pl.BlockSpec((1, R_KV), fixed),
                  pl.BlockSpec((1, AB_SMALL), fixed), pl.BlockSpec((1, AB_SMALL), fixed)],
        out_specs=[pl.BlockSpec((tm, R_Q), row), pl.BlockSpec((tm, R_KV), row),
                   pl.BlockSpec((tm, D_IDX), row), pl.BlockSpec((tm, AB_SMALL), row),
                   pl.BlockSpec((tm, 4 * W), row)],
        out_shape=[jax.ShapeDtypeStruct((T, R_Q), BF16), jax.ShapeDtypeStruct((T, R_KV), BF16),
                   jax.ShapeDtypeStruct((T, D_IDX), BF16), jax.ShapeDtypeStruct((T, AB_SMALL), F32),
                   jax.ShapeDtypeStruct((T, 4 * W), F32)],
        compiler_params=_cparams(("parallel",)),
        name="ab_in_proj",
    )(h2d, w_perm, q_norm_g.reshape(1, -1), kv_norm_g.reshape(1, -1), gk, bk)


def _sort_key(x):
    bits = pltpu.bitcast(x + 0.0, I32)
    return jnp.where(bits < 0, bits ^ 0x7FFFFFFF, bits)


def _dsa_kernel(cq_ref, ckv_ref, kidx_ref, widx_ref, wqidx_ref, wuq_ref, wukbd_ref, wuvbd_ref,
                o_ref, key_ref, qlat_ref, m_ref, l_ref, acc_ref, *, qb, kc, topk, seq_bits):
    i = pl.program_id(1)
    nck = ((i + 1) * qb + kc - 1) // kc
    n_lt = kc // 128
    cq = cq_ref[0]
    widx = widx_ref[0] * (H_IDX ** -0.5 * D_IDX ** -0.5)
    q_pos = i * qb + lax.broadcasted_iota(I32, (qb, 1), 0)

    qidx = [_dot(cq, wqidx_ref[h]).astype(BF16) for h in range(H_IDX)]

    def idx_body(c, carry):
        kblk = kidx_ref[0, pl.ds(pl.multiple_of(c * kc, kc), kc), :]
        isc = jnp.zeros((qb, kc), F32)
        for h in range(H_IDX):
            sc = _dot_nt(qidx[h], kblk)
            isc = isc + jnp.maximum(sc, 0.0) * widx[:, h:h + 1]
        k_pos = c * kc + lax.broadcasted_iota(I32, (1, kc), 1)
        key_ref[c] = jnp.where(k_pos <= q_pos, _sort_key(isc), KEY_NEG_INF)
        return carry

    lax.fori_loop(0, nck, idx_body, 0)

    rg = 128
    groups = list(range(0, qb, rg))

    def count_acc(r0, pred):
        def body(c, acc):
            for j in range(n_lt):
                acc = acc + jnp.where(pred(key_ref[c, r0:r0 + rg, j * 128:(j + 1) * 128], c, j), 1, 0)
            return acc
        return lax.fori_loop(0, nck, body, jnp.zeros((rg, 128), I32))

    def count_where(r0, pred):
        return jnp.sum(count_acc(r0, pred), axis=1, keepdims=True)

    def bit_cond(carry):
        b, _, dones = carry
        return (b < 32) & (functools.reduce(jnp.minimum, [jnp.min(d) for d in dones]) == 0)

    def bit_body(carry):
        b, ts, dones = carry
        bit = jnp.int32(1) << (31 - b)
        cands = [t + bit for t in ts]
        accs = [count_acc(r0, lambda k, c, j, cand=cand: k >= cand) for r0, cand in zip(groups, cands)]
        cnts = [jnp.sum(acc, axis=1, keepdims=True) for acc in accs]
        ts = tuple(jnp.where((cnt >= topk) & (d == 0), cand, t)
                   for t, d, cand, cnt in zip(ts, dones, cands, cnts))
        dones = tuple(jnp.where(cnt == topk, 1, d) for d, cnt in zip(dones, cnts))
        return b + 1, ts, dones

    settled = tuple(jnp.where(q_pos[r0:r0 + rg] + 1 <= topk, 1, 0) for r0 in groups)
    t0 = tuple(jnp.full((rg, 1), INT_MIN, I32) for _ in groups)
    _, thrs, dones = lax.while_loop(bit_cond, bit_body, (jnp.int32(0), t0, settled))
    thrs = [jnp.maximum(t, KEY_NEG_INF + 1) for t in thrs]

    def break_ties(r0, thr, need):
        def idx_of(c, j):
            return c * kc + j * 128 + lax.broadcasted_iota(I32, (1, 128), 1)

        def pos_body(b, m):
            cand = m + (jnp.int32(1) << (seq_bits - 1 - b))
            cnt = count_where(r0, lambda k, c, j: (k == thr) & (idx_of(c, j) < cand))
            return jnp.where(cnt < need, cand, m)

        m = lax.fori_loop(0, seq_bits, pos_body, jnp.zeros((rg, 1), I32))

        def demote(c, carry):
            k = key_ref[c, r0:r0 + rg, :]
            idx = c * kc + lax.broadcasted_iota(I32, (1, kc), 1)
            key_ref[c, r0:r0 + rg, :] = jnp.where((k == thr) & (idx > m), thr - 1, k)
            return carry

        lax.fori_loop(0, nck, demote, 0)

    for r0, thr_g, done_g in zip(groups, thrs, dones):
        @pl.when(jnp.min(done_g) == 0)
        def _(r0=r0, thr_g=thr_g):
            c_gt = count_where(r0, lambda k, c, j: k > thr_g)
            c_ge = count_where(r0, lambda k, c, j: k >= thr_g)
            need = topk - c_gt

            @pl.when(jnp.max(c_ge - c_gt - need) > 0)
            def _():
                break_ties(r0, thr_g, need)

    thr = jnp.concatenate(thrs, axis=0)

    q = _dot(cq, wuq_ref[...]).astype(BF16)
    qlat = _dot(q, wukbd_ref[...]) * (D_NOPE ** -0.5 * LOG2E)
    for h in range(H_A):
        qlat_ref[h * qb:(h + 1) * qb, :] = qlat[:, h * R_KV:(h + 1) * R_KV].astype(BF16)
    m_ref[...] = jnp.full(m_ref.shape, -1e30, F32)
    l_ref[...] = jnp.zeros(l_ref.shape, F32)
    acc_ref[...] = jnp.zeros(acc_ref.shape, F32)

    def att_body(c, carry):
        kv = ckv_ref[0, pl.ds(pl.multiple_of(c * kc, kc), kc), :]
        bias = jnp.where(key_ref[c] >= thr, 0.0, NEG_INF)
        s = _dot_nt(qlat_ref[...], kv)
        s = (s.reshape(H_A, qb, kc) + bias[None]).reshape(H_A * qb, kc)
        tiles = [s[:, j * 128:(j + 1) * 128] for j in range(n_lt)]
        m_cur = tiles[0]
        for t in tiles[1:]:
            m_cur = jnp.maximum(m_cur, t)
        m_old = m_ref[...]
        m_new = jnp.maximum(m_old, jnp.max(m_cur, axis=1, keepdims=True))
        ps = [jnp.exp2(t - m_new) for t in tiles]
        a = jnp.exp2(m_old - m_new)
        psum = ps[0]
        for t in ps[1:]:
            psum = psum + t
        l_ref[...] = a * l_ref[...] + psum
        p = jnp.concatenate([t.astype(BF16) for t in ps], axis=1)
        acc_ref[...] = a * acc_ref[...] + _dot(p, kv)
        m_ref[...] = m_new
        return carry

    lax.fori_loop(0, nck, att_body, 0)
    o_all = acc_ref[...] / jnp.sum(l_ref[...], axis=1, keepdims=True)
    o_lat = jnp.concatenate([o_all[h * qb:(h + 1) * qb, :] for h in range(H_A)], axis=1)
    o_ref[0] = _dot(o_lat.astype(BF16), wuvbd_ref[...]).astype(o_ref.dtype)


def _block_diag(blocks):
    n = len(blocks)
    r, c = blocks[0].shape
    out = jnp.zeros((n * r, n * c), blocks[0].dtype)
    for k, blk in enumerate(blocks):
        out = out.at[k * r:(k + 1) * r, k * c:(k + 1) * c].set(blk)
    return out


def _dsa(cq, ckv, kidx, widx, w_uq, w_uk, w_uv, w_qidx, qb=256, kc=512):
    B, S, _ = cq.shape
    kc = min(kc, S)
    topk = min(TOPK_MAX, S // 4)
    assert S % kc == 0 and kc % qb == 0 and kc >= topk
    seq_bits = max(1, (S - 1).bit_length())
    wqidx = jnp.transpose(w_qidx, (1, 0, 2)).astype(BF16)
    wuq = w_uq.reshape(R_Q, H_A * D_NOPE).astype(BF16)
    wukbd = _block_diag([w_uk[:, h, :].T for h in range(H_A)]).astype(BF16)
    wuvbd = _block_diag([w_uv[:, h, :] for h in range(H_A)]).astype(BF16)
    kern = functools.partial(_dsa_kernel, qb=qb, kc=kc, topk=topk, seq_bits=seq_bits)
    blk_q = lambda b, i: (b, i, 0)
    seq = lambda b, i: (b, 0, 0)
    fix2 = lambda b, i: (0, 0)
    fix3 = lambda b, i: (0, 0, 0)
    return pl.pallas_call(
        kern,
        grid=(B, S // qb),
        in_specs=[pl.BlockSpec((1, qb, R_Q), blk_q), pl.BlockSpec((1, S, R_KV), seq),
                  pl.BlockSpec((1, S, D_IDX), seq), pl.BlockSpec((1, qb, H_IDX), blk_q),
                  pl.BlockSpec(wqidx.shape, fix3), pl.BlockSpec(wuq.shape, fix2),
                  pl.BlockSpec(wukbd.shape, fix2), pl.BlockSpec(wuvbd.shape, fix2)],
        out_specs=pl.BlockSpec((1, qb, H_A * D_VA), blk_q),
        out_shape=jax.ShapeDtypeStruct((B, S, H_A * D_VA), BF16),
        scratch_shapes=[pltpu.VMEM((S // kc, qb, kc), I32),
                        pltpu.VMEM((H_A * qb, R_KV), BF16),
                        pltpu.VMEM((H_A * qb, R_KV), F32), pltpu.VMEM((H_A * qb, R_KV), F32),
                        pltpu.VMEM((H_A * qb, R_KV), F32)],
        compiler_params=_cparams(("parallel", "arbitrary")),
        name="dsa_attention",
    )(cq, ckv, kidx, widx, wqidx, wuq, wukbd, wuvbd)


GDN_HALO = 8


def _softplus(x):
    return jnp.maximum(x, 0.0) + jnp.log1p(jnp.exp(-jnp.abs(x)))


def _gdn_kernel(qkvz_ref, abc_ref, abr_ref, convw_ref, prm_c_ref, prm_r_ref, ng_ref, tri_ref,
                o_ref, xbuf_ref, state_ref, conv_ref, *, cb):
    C = DN_CHUNK
    W = H_B * D_B
    j = pl.program_id(1)

    @pl.when(j == 0)
    def _():
        xbuf_ref[0:GDN_HALO, :] = jnp.zeros((GDN_HALO, 3 * W), F32)
        state_ref[...] = jnp.zeros(state_ref.shape, F32)

    xbuf_ref[GDN_HALO:, :] = qkvz_ref[0, :, 0:3 * W]
    acc = xbuf_ref[GDN_HALO:, :] * convw_ref[CONV_K - 1:CONV_K, :]
    for t in range(CONV_K - 1):
        sh = CONV_K - 1 - t
        acc = acc + xbuf_ref[pl.ds(GDN_HALO - sh, cb), :] * convw_ref[t:t + 1, :]
    conv_ref[...] = _silu(acc)
    xbuf_ref[0:GDN_HALO, :] = xbuf_ref[cb:cb + GDN_HALO, :]

    abc = abc_ref[0]
    abr = abr_ref[0]
    beta_c = _sigmoid(abc)
    g_c = -jnp.exp(prm_r_ref[0:1, :]) * _softplus(abc + prm_r_ref[1:2, :])
    g_r = -jnp.exp(prm_c_ref[:, 0:1]) * _softplus(abr + prm_c_ref[:, 1:2])
    gc_c = _dot(tri_ref[...], g_c, precision=HIGHEST)
    gc_r = _dot_nt(g_r, tri_ref[...], precision=HIGHEST)

    ri = lax.broadcasted_iota(I32, (cb, cb), 0)
    ci = lax.broadcasted_iota(I32, (cb, cb), 1)
    same = (ri // C) == (ci // C)
    lower = same & (ri >= ci)
    strict = same & (ri > ci)
    eye = jnp.where(ri == ci, 1.0, 0.0)
    bf = lambda t: t.astype(BF16)

    for h in range(H_B):
        q = conv_ref[:, h * D_B:(h + 1) * D_B]
        k = conv_ref[:, W + h * D_B:W + (h + 1) * D_B]
        v = conv_ref[:, 2 * W + h * D_B:2 * W + (h + 1) * D_B]
        q = q * lax.rsqrt(jnp.sum(q * q, axis=-1, keepdims=True) + 1e-6) * (D_B ** -0.5)
        k = k * lax.rsqrt(jnp.sum(k * k, axis=-1, keepdims=True) + 1e-6)
        beta = beta_c[:, h:h + 1]
        gcol = gc_c[:, H_B + h:H_B + h + 1]
        grow = gc_r[H_B + h:H_B + h + 1, :]
        decay = jnp.exp(jnp.where(lower, gcol - grow, NEG_INF))
        kb = k * beta
        vb = v * beta
        kq = bf(k)
        a_mat = jnp.where(strict, _dot_nt(bf(kb), kq) * decay, 0.0)
        xm = -a_mat
        t_mat = eye + xm
        for _ in range(int(math.log2(C)) - 1):
            xq = bf(xm)
            xm = _dot(xq, xq)
            t_mat = t_mat + _dot(bf(t_mat), bf(xm))
        egc = jnp.exp(gcol)
        tq = bf(t_mat)
        u = _dot(tq, bf(vb))
        w = bf(_dot(tq, bf(kb * egc)))
        qk = bf(jnp.where(lower, _dot_nt(bf(q), kq) * decay, 0.0))
        q_dec = bf(q * egc)
        for n in range(cb // C):
            r0 = n * C
            glast = gcol[r0 + C - 1:r0 + C, :]
            k_dec = bf(k[r0:r0 + C] * jnp.exp(glast - gcol[r0:r0 + C]))
            st = state_ref[h]
            stq = bf(st)
            v_new = u[r0:r0 + C] - _dot(w[r0:r0 + C], stq)
            vq = bf(v_new)
            o = _dot(q_dec[r0:r0 + C], stq) + _dot(qk[r0:r0 + C, r0:r0 + C], vq)
            state_ref[h] = st * jnp.exp(glast) + _dot_tn(k_dec, vq)
            o = o * lax.rsqrt(jnp.mean(o * o, axis=-1, keepdims=True) + 1e-6) * ng_ref[...]
            z = qkvz_ref[0, r0:r0 + C, 3 * W + h * D_B:3 * W + (h + 1) * D_B]
            o_ref[0, r0:r0 + C, h * D_B:(h + 1) * D_B] = (o * _silu(z)).astype(o_ref.dtype)


def _gdn(qkvz, b, a, conv_w, a_log, dt_bias, norm_g, cb=256):
    B, S, _ = qkvz.shape
    W = H_B * D_B
    cb = min(cb, S)
    abc = jnp.concatenate([b, a], axis=-1)
    abr = jnp.transpose(abc, (0, 2, 1))
    zeros = jnp.zeros((H_B,), F32)
    prm = jnp.stack([jnp.concatenate([zeros, a_log]), jnp.concatenate([zeros, dt_bias])])
    idx = jnp.arange(cb)
    tri = ((idx[:, None] >= idx[None, :]) & (idx[:, None] // DN_CHUNK == idx[None, :] // DN_CHUNK)).astype(F32)
    kern = functools.partial(_gdn_kernel, cb=cb)
    blk = lambda bi, j: (bi, j, 0)
    fix = lambda bi, j: (0, 0)
    return pl.pallas_call(
        kern,
        grid=(B, S // cb),
        in_specs=[pl.BlockSpec((1, cb, 4 * W), blk), pl.BlockSpec((1, cb, 2 * H_B), blk),
                  pl.BlockSpec((1, 2 * H_B, cb), lambda bi, j: (bi, 0, j)),
                  pl.BlockSpec((CONV_K, 3 * W), fix), pl.BlockSpec((2 * H_B, 2), fix),
                  pl.BlockSpec((2, 2 * H_B), fix), pl.BlockSpec((1, D_B), fix),
                  pl.BlockSpec((cb, cb), fix)],
        out_specs=pl.BlockSpec((1, cb, W), blk),
        out_shape=jax.ShapeDtypeStruct((B, S, W), BF16),
        scratch_shapes=[pltpu.VMEM((cb + GDN_HALO, 3 * W), F32), pltpu.VMEM((H_B, D_B, D_B), F32),
                        pltpu.VMEM((cb, 3 * W), F32)],
        compiler_params=_cparams(("parallel", "arbitrary")),
        name="gated_deltanet",
    )(qkvz, abc, abr, conv_w, prm.T, prm, norm_g.reshape(1, D_B), tri)


def _store_row_tiles(ref, val):
    n, d = val.shape
    sub = d // LANES
    for j in range(sub):
        ref[pl.ds(j, n, stride=sub), :] = val[:, j * LANES:(j + 1) * LANES]


def _load_row_tiles(ref, n, sub):
    return jnp.concatenate([ref[pl.ds(j, n, stride=sub), :] for j in range(sub)], axis=1)


def _post_mixer_kernel(a1_ref, a2_ref, h_ref, w1_ref, w2_ref, g_ref, b_ref, rw_ref, rb_ref, triu_ref,
                       h1_ref, h1t_ref, tope_ref, gate_ref, rank_ref, cnt_ref, carry_ref):
    i = pl.program_id(0)
    E, tm = rw_ref.shape[0], h_ref.shape[0]

    @pl.when(i == 0)
    def _():
        carry_ref[...] = jnp.zeros(carry_ref.shape, F32)

    mix = _dot(a1_ref[...].astype(BF16), w1_ref[...]) + _dot(a2_ref[...].astype(BF16), w2_ref[...])
    h1 = _layernorm_rows(ALPHA * h_ref[...] + mix, g_ref[...], b_ref[...])
    h1_ref[...] = h1
    _store_row_tiles(h1t_ref, h1)

    logits = _dot_nt(rw_ref[...], h1, precision=HIGHEST) + rb_ref[...]
    erow = lax.broadcasted_iota(I32, (E, tm), 0)
    sel = jnp.zeros((E, tm), F32)
    onehots, tops = [], []
    for k in range(TOP_K):
        mx = jnp.max(logits, axis=0, keepdims=True)
        idx = jnp.min(jnp.where(logits == mx, erow, E), axis=0, keepdims=True)
        oh = erow == idx
        logits = jnp.where(oh, NEG_INF, logits)
        sel = sel + jnp.where(oh, 1.0, 0.0)
        onehots.append(oh)
        tops.append(mx)
        tope_ref[k:k + 1, :] = idx
    exps = [jnp.exp(t - tops[0]) for t in tops]
    den = exps[0] + exps[1] + exps[2] + exps[3]
    for k in range(TOP_K):
        gate_ref[k:k + 1, :] = exps[k] / den
    incl = _dot(sel.astype(BF16), triu_ref[...])
    excl = incl - sel + carry_ref[:, 0:1]
    for k in range(TOP_K):
        rank_ref[k:k + 1, :] = jnp.sum(jnp.where(onehots[k], excl, 0.0), axis=0, keepdims=True).astype(I32)
    carry_ref[...] = carry_ref[...] + jnp.sum(sel, axis=1, keepdims=True)
    cnt_ref[...] = carry_ref[...]


def _post_mixer(a1, a2, h2d, w_out, ln_g, ln_b, router_w, router_b, tm=512):
    T, D = h2d.shape
    E = router_w.shape[1]
    n1, n2 = a1.shape[1], a2.shape[1]
    w1 = w_out[:n1].astype(BF16)
    w2 = w_out[n1:].astype(BF16)
    idx = jnp.arange(tm)
    triu = (idx[:, None] <= idx[None, :]).astype(BF16)
    row = lambda i: (i, 0)
    col = lambda i: (0, i)
    fix = lambda i: (0, 0)
    return pl.pallas_call(
        _post_mixer_kernel,
        grid=(T // tm,),
        in_specs=[pl.BlockSpec((tm, n1), row), pl.BlockSpec((tm, n2), row), pl.BlockSpec((tm, D), row),
                  pl.BlockSpec((n1, D), fix), pl.BlockSpec((n2, D), fix),
                  pl.BlockSpec((1, D), fix), pl.BlockSpec((1, D), fix),
                  pl.BlockSpec((E, D), fix), pl.BlockSpec((E, 1), fix), pl.BlockSpec((tm, tm), fix)],
        out_specs=[pl.BlockSpec((tm, D), row), pl.BlockSpec((tm * (D // LANES), LANES), row),
                   pl.BlockSpec((TOP_K, tm), col), pl.BlockSpec((TOP_K, tm), col),
                   pl.BlockSpec((TOP_K, tm), col), pl.BlockSpec((E, 128), fix)],
        out_shape=[jax.ShapeDtypeStruct((T, D), F32), jax.ShapeDtypeStruct((T * (D // LANES), LANES), F32),
                   jax.ShapeDtypeStruct((TOP_K, T), I32), jax.ShapeDtypeStruct((TOP_K, T), F32),
                   jax.ShapeDtypeStruct((TOP_K, T), I32), jax.ShapeDtypeStruct((E, 128), F32)],
        scratch_shapes=[pltpu.VMEM((E, 128), F32)],
        compiler_params=_cparams(("arbitrary",)),
        name="post_mixer_router",
    )(a1, a2, h2d, w1, w2, ln_g.reshape(1, D), ln_b.reshape(1, D), router_w.T, router_b.reshape(E, 1), triu)


def _dispatch_kernel(dest_ref, h_ref, xs_init_hbm, xs_hbm, sem, *, tt, n_tok, sub):
    del xs_init_hbm
    base = pl.program_id(0) * tt

    def row_copy(r, k):
        d = dest_ref[k * n_tok + base + r]
        return pltpu.make_async_copy(h_ref.at[pl.ds(pl.multiple_of(r * sub, sub), sub)],
                                     xs_hbm.at[pl.ds(pl.multiple_of(d * sub, sub), sub)], sem)

    def start(r, c):
        for k in range(TOP_K):
            row_copy(r, k).start(priority=k % 2)
        return c

    def wait(r, c):
        for k in range(TOP_K):
            row_copy(r, k).wait()
        return c

    lax.fori_loop(0, tt, start, 0)
    lax.fori_loop(0, tt, wait, 0)


def _moe_dispatch(h1t, dest_flat, n_slots, sub, tt=512):
    T = h1t.shape[0] // sub
    any_spec = pl.BlockSpec(memory_space=pl.ANY)
    return pl.pallas_call(
        functools.partial(_dispatch_kernel, tt=tt, n_tok=T, sub=sub),
        grid_spec=pltpu.PrefetchScalarGridSpec(
            num_scalar_prefetch=1, grid=(T // tt,),
            in_specs=[pl.BlockSpec((tt * sub, LANES), lambda i, d: (i, 0)), any_spec], out_specs=any_spec,
            scratch_shapes=[pltpu.SemaphoreType.DMA(())]),
        out_shape=jax.ShapeDtypeStruct((n_slots * sub, LANES), h1t.dtype),
        input_output_aliases={2: 0},
        compiler_params=_cparams(("arbitrary",)),
        name="moe_dispatch",
    )(dest_flat, h1t, jnp.zeros((n_slots * sub, LANES), h1t.dtype))


MOE_BLOCK = 256


def _moe_kernel(be_ref, nu_ref, first_ref, slot_ref, next_ref, x_ref, wgu_hbm, bg_ref, bu_ref, wd_hbm, bd_ref,
                y_ref, wgu_buf, wd_buf, wt_ref, wg_ref, wu_ref, wdb_ref, sem, *, bm, sub, layer, n_exp):
    i = pl.program_id(0)
    F = wd_buf.shape[1]

    def weight_copies(e, s):
        return (pltpu.make_async_copy(wgu_hbm.at[layer, e], wgu_buf.at[s], sem.at[0, s]),
                pltpu.make_async_copy(wd_hbm.at[layer, e], wd_buf.at[s], sem.at[1, s]))

    @pl.when((i < nu_ref[0]) & (first_ref[i] == 1))
    def _():
        e, s, nxt = be_ref[i], slot_ref[i], next_ref[i]

        @pl.when(i == 0)
        def _():
            for cp in weight_copies(e, s):
                cp.start()

        for cp in weight_copies(e, s):
            cp.wait()

        @pl.when(nxt < n_exp)
        def _():
            for cp in weight_copies(nxt, 1 - s):
                cp.start()

        ck = 512
        for c in range(0, 2 * F, ck):
            wt = wgu_buf[s, :, c:c + ck].T
            for j in range(sub):
                wt_ref[pl.ds(c * sub + j, ck, stride=sub), :] = wt[:, j * LANES:(j + 1) * LANES]
        for j in range(sub):
            cols = slice(j * LANES, (j + 1) * LANES)
            wg_ref[:, cols] = wt_ref[pl.ds(j, F, stride=2 * sub), :].astype(BF16)
            wu_ref[:, cols] = wt_ref[pl.ds(sub + j, F, stride=2 * sub), :].astype(BF16)
        wdb_ref[...] = wd_buf[s].astype(BF16)

    @pl.when(i < nu_ref[0])
    def _():
        x = _load_row_tiles(x_ref, bm, sub).astype(BF16)
        g = _dot_nt(x, wg_ref[...]) + bg_ref[0]
        u = _dot_nt(x, wu_ref[...]) + bu_ref[0]
        gt = jnp.minimum(g, SWIGLU_LIMIT)
        up = jnp.clip(u, -SWIGLU_LIMIT, SWIGLU_LIMIT)
        hid = (up + 1.0) * (gt * _sigmoid(gt * SWIGLU_ALPHA))
        _store_row_tiles(y_ref, _dot(hid.astype(BF16), wdb_ref[...]) + bd_ref[0])

    @pl.when(i >= nu_ref[0])
    def _():
        y_ref[...] = jnp.zeros(y_ref.shape, y_ref.dtype)


def _moe_experts(xs, blk_e, n_used, padded, w_gu, bg, bu, w_down, bd, sub, layer):
    _, E, D, F2 = w_gu.shape
    F = F2 // 2
    bm = MOE_BLOCK
    P = xs.shape[0] // sub
    nblk = P // bm
    prev_e = jnp.concatenate([jnp.full((1,), -1, I32), blk_e[:-1]])
    first = (blk_e != prev_e).astype(I32)
    slot = (jnp.cumsum(first) - 1) % 2
    ids = jnp.where(padded > 0, jnp.arange(E, dtype=I32), E)
    later_min = lax.cummin(ids[::-1])[::-1]
    next_used = jnp.concatenate([later_min[1:], jnp.full((1,), E, I32)])
    nxt = jnp.sum(jnp.where(blk_e[:, None] == jnp.arange(E, dtype=I32)[None, :], next_used[None, :], 0), axis=1)
    wsel = lambda i, *_: (0, 0, 0)
    esel = lambda i, be, *_: (be[i], 0, 0)
    row = lambda i, *_: (i, 0)
    any_spec = pl.BlockSpec(memory_space=pl.ANY)
    return pl.pallas_call(
        functools.partial(_moe_kernel, bm=bm, sub=sub, layer=layer, n_exp=E),
        grid_spec=pltpu.PrefetchScalarGridSpec(
            num_scalar_prefetch=5,
            grid=(nblk,),
            in_specs=[pl.BlockSpec((bm * sub, LANES), row), any_spec,
                      pl.BlockSpec((1, 1, F), esel), pl.BlockSpec((1, 1, F), esel),
                      any_spec, pl.BlockSpec((1, 1, D), esel)],
            out_specs=pl.BlockSpec((bm * sub, LANES), row),
            scratch_shapes=[pltpu.VMEM((2, D, F2), F32), pltpu.VMEM((2, F, D), F32),
                            pltpu.VMEM((F2 * sub, LANES), F32), pltpu.VMEM((F, D), BF16), pltpu.VMEM((F, D), BF16),
                            pltpu.VMEM((F, D), BF16), pltpu.SemaphoreType.DMA((2, 2))]),
        out_shape=jax.ShapeDtypeStruct((P * sub, LANES), F32),
        compiler_params=_cparams(("arbitrary",)),
        name="moe_experts",
    )(blk_e, n_used, first, slot.astype(I32), nxt.astype(I32), xs, w_gu, bg, bu, w_down, bd)


def _moe(h1t, top_e, rank, cnt, w_gu, b_gu, w_down, b_down, layer):
    E, D = w_gu.shape[1], w_gu.shape[2]
    sub = D // LANES
    T = h1t.shape[0] // sub
    bm = MOE_BLOCK
    counts = cnt[:, 0].astype(I32)
    padded = (counts + bm - 1) // bm * bm
    pad_end = jnp.cumsum(padded)
    pad_start = pad_end - padded
    e_ids = jnp.arange(E, dtype=I32)[:, None, None]
    start_of = jnp.sum(jnp.where(top_e[None] == e_ids, pad_start[:, None, None], 0), axis=0)
    dest = (start_of + rank).reshape(-1)
    nblk = (T * TOP_K) // bm + E
    blk_first = jnp.arange(nblk, dtype=I32) * bm
    blk_e = jnp.minimum(jnp.sum(pad_end[None, :] <= blk_first[:, None], axis=1), E - 1).astype(I32)
    n_used = (pad_end[-1:] // bm).astype(I32)
    xs = _moe_dispatch(h1t, dest, nblk * bm, sub)
    ys = _moe_experts(xs, blk_e, n_used, padded, w_gu, b_gu[:, None, 0::2], b_gu[:, None, 1::2],
                      w_down, b_down[:, None, :], sub, layer)
    return ys, dest


def _tail_kernel(dest_ref, h1_ref, gate_ref, p_ref, wg_ref, wp_ref, g_ref, b_ref, ys_hbm, o_ref,
                 ybuf, sem, *, tm, n_tok, sub):
    i = pl.program_id(0)

    def row_copy(tile, slot, r, k):
        d = dest_ref[k * n_tok + tile * tm + r]
        return pltpu.make_async_copy(ys_hbm.at[pl.ds(pl.multiple_of(d * sub, sub), sub)],
                                     ybuf.at[slot, k, pl.ds(pl.multiple_of(r * sub, sub), sub)], sem.at[slot])

    def start_tile(tile, slot):
        def body(r, c):
            for k in range(TOP_K):
                row_copy(tile, slot, r, k).start(priority=k % 2)
            return c
        lax.fori_loop(0, tm, body, 0)

    def wait_tile(tile, slot):
        def body(r, c):
            for k in range(TOP_K):
                row_copy(tile, slot, r, k).wait()
            return c
        lax.fori_loop(0, tm, body, 0)

    @pl.when(i == 0)
    def _():
        start_tile(0, 0)

    h1 = h1_ref[...]
    ple = _sigmoid(_dot(h1.astype(BF16), wg_ref[...])) * _dot(p_ref[...].astype(BF16), wp_ref[...])
    x = ALPHA * h1 + ple

    for slot in range(2):
        @pl.when(i % 2 == slot)
        def _(slot=slot):
            @pl.when(i + 1 < pl.num_programs(0))
            def _():
                start_tile(i + 1, 1 - slot)

            wait_tile(i, slot)
            cols = []
            for j in range(sub):
                col = ybuf[slot, 0, pl.ds(j, tm, stride=sub), :] * gate_ref[:, 0:1]
                for k in range(1, TOP_K):
                    col = col + ybuf[slot, k, pl.ds(j, tm, stride=sub), :] * gate_ref[:, k:k + 1]
                cols.append(col)
            ffn = jnp.concatenate(cols, axis=1)
            o_ref[...] = _layernorm_rows(x + ffn, g_ref[...], b_ref[...])


def _layer_tail(h1, ys, dest_flat, gate_t, p2d, ple_w_gate, ple_w_proj, ln_g, ln_b, tm=256):
    T, D = h1.shape
    PD = p2d.shape[1]
    sub = D // LANES
    row = lambda i, d: (i, 0)
    fix = lambda i, d: (0, 0)
    return pl.pallas_call(
        functools.partial(_tail_kernel, tm=tm, n_tok=T, sub=sub),
        grid_spec=pltpu.PrefetchScalarGridSpec(
            num_scalar_prefetch=1, grid=(T // tm,),
            in_specs=[pl.BlockSpec((tm, D), row), pl.BlockSpec((tm, TOP_K), row),
                      pl.BlockSpec((tm, PD), row), pl.BlockSpec((D, D), fix), pl.BlockSpec((PD, D), fix),
                      pl.BlockSpec((1, D), fix), pl.BlockSpec((1, D), fix),
                      pl.BlockSpec(memory_space=pl.ANY)],
            out_specs=pl.BlockSpec((tm, D), row),
            scratch_shapes=[pltpu.VMEM((2, TOP_K, tm * sub, LANES), F32), pltpu.SemaphoreType.DMA((2,))]),
        out_shape=jax.ShapeDtypeStruct((T, D), F32),
        compiler_params=_cparams(("arbitrary",)),
        name="layer_tail",
    )(dest_flat, h1, gate_t.T, p2d, ple_w_gate.astype(BF16), ple_w_proj.astype(BF16),
      ln_g.reshape(1, D), ln_b.reshape(1, D), ys)


def _cd_proj_kernel(x_ref, w_ref, q_ref, k_ref, v_ref, u_ref):
    x = x_ref[...].astype(BF16)
    n = H_C * DH_C
    q_ref[...] = (_dot(x, w_ref[:, 0:n]) * (DH_C ** -0.5 * LOG2E)).astype(q_ref.dtype)
    k_ref[...] = _dot(x, w_ref[:, n:2 * n]).astype(k_ref.dtype)
    v_ref[...] = _dot(x, w_ref[:, 2 * n:3 * n]).astype(v_ref.dtype)
    u_ref[...] = _dot(x, w_ref[:, 3 * n:])


def _cd_in_proj(h2d, w_in, tm=512):
    T, D = h2d.shape
    n = H_C * DH_C
    nu = w_in.shape[1] - 3 * n
    row = lambda i: (i, 0)
    return pl.pallas_call(
        _cd_proj_kernel,
        grid=(T // tm,),
        in_specs=[pl.BlockSpec((tm, D), row), pl.BlockSpec(w_in.shape, lambda i: (0, 0))],
        out_specs=[pl.BlockSpec((tm, n), row)] * 3 + [pl.BlockSpec((tm, nu), row)],
        out_shape=[jax.ShapeDtypeStruct((T, n), BF16)] * 3 + [jax.ShapeDtypeStruct((T, nu), F32)],
        compiler_params=_cparams(("parallel",)),
        name="cd_in_proj",
    )(h2d, w_in.astype(BF16))


def _dilated_bias_table(qb):
    import numpy as np
    max_w = max(w for w, _ in DILATED_PATTERNS)
    ndc = max_w // qb + 1
    r = np.arange(qb)[:, None]
    j = np.arange(qb)[None, :]
    tbl = np.empty((ndc, qb, qb), np.float32)
    for dc in range(ndc):
        delta = dc * qb + r - j
        mult = np.zeros((qb, qb), np.float64)
        for w, d in DILATED_PATTERNS:
            mult += (delta >= 0) & (delta <= w) & (delta % d == 0)
        with np.errstate(divide="ignore"):
            tbl[dc] = np.log2(mult)
    return jnp.asarray(tbl)


def _dilated_kernel(q_ref, k_ref, v_ref, bias_ref, o_ref, q2_ref, m_ref, l_ref, acc_ref, *, qb, ndc):
    i = pl.program_id(2)
    n_lt = qb // 128
    q = q_ref[0]
    lo = lax.broadcasted_iota(I32, q.shape, 1) < DH_C
    zero = jnp.zeros(q.shape, q.dtype)
    q2_ref[0:qb, :] = jnp.where(lo, q, zero)
    q2_ref[qb:, :] = jnp.where(lo, zero, q)
    m_ref[...] = jnp.full(m_ref.shape, -1e30, F32)
    l_ref[...] = jnp.zeros(l_ref.shape, F32)
    acc_ref[...] = jnp.zeros(acc_ref.shape, F32)

    def body(c, carry):
        r0 = pl.multiple_of(c * qb, qb)
        kk = k_ref[0, pl.ds(r0, qb), :]
        vv = v_ref[0, pl.ds(r0, qb), :]
        s = _dot_nt(q2_ref[...], kk)
        s = (s.reshape(2, qb, qb) + bias_ref[i - c][None]).reshape(2 * qb, qb)
        tiles = [s[:, j * 128:(j + 1) * 128] for j in range(n_lt)]
        m_cur = tiles[0]
        for t in tiles[1:]:
            m_cur = jnp.maximum(m_cur, t)
        m_old = m_ref[...]
        m_new = jnp.maximum(m_old, jnp.max(m_cur, axis=1, keepdims=True))
        ps = [jnp.exp2(t - m_new) for t in tiles]
        a = jnp.exp2(m_old - m_new)
        psum = ps[0]
        for t in ps[1:]:
            psum = psum + t
        l_ref[...] = a * l_ref[...] + psum
        p = jnp.concatenate([t.astype(BF16) for t in ps], axis=1)
        acc_ref[...] = a * acc_ref[...] + _dot(p, vv)
        m_ref[...] = m_new
        return carry

    lax.fori_loop(jnp.maximum(i - (ndc - 1), 0), i + 1, body, 0)
    o_all = acc_ref[...] / jnp.sum(l_ref[...], axis=1, keepdims=True)
    o_ref[0] = jnp.where(lo, o_all[0:qb, :], o_all[qb:, :]).astype(o_ref.dtype)


def _dilated_attention(q, k, v, qb=512):
    B, S, n = q.shape
    qb = min(qb, S)
    bias = _dilated_bias_table(qb)
    ndc = bias.shape[0]
    pw = 2 * DH_C
    kern = functools.partial(_dilated_kernel, qb=qb, ndc=ndc)
    blk = lambda b, pr, i: (b, i, pr)
    seq = lambda b, pr, i: (b, 0, pr)
    return pl.pallas_call(
        kern,
        grid=(B, n // pw, S // qb),
        in_specs=[pl.BlockSpec((1, qb, pw), blk), pl.BlockSpec((1, S, pw), seq),
                  pl.BlockSpec((1, S, pw), seq), pl.BlockSpec(bias.shape, lambda b, pr, i: (0, 0, 0))],
        out_specs=pl.BlockSpec((1, qb, pw), blk),
        out_shape=jax.ShapeDtypeStruct((B, S, n), BF16),
        scratch_shapes=[pltpu.VMEM((2 * qb, pw), BF16), pltpu.VMEM((2 * qb, pw), F32),
                        pltpu.VMEM((2 * qb, pw), F32), pltpu.VMEM((2 * qb, pw), F32)],
        compiler_params=_cparams(("parallel", "parallel", "arbitrary")),
        name="dilated_attention",
    )(q, k, v, bias)


POOL_HALO = 16


def _pool_kernel(u_ref, w_ref, sc_ref, o_ref, xbuf_ref, *, tm):
    j = pl.program_id(1)

    @pl.when(j == 0)
    def _():
        xbuf_ref[0:POOL_HALO, :] = jnp.zeros((POOL_HALO, xbuf_ref.shape[1]), F32)

    xbuf_ref[POOL_HALO:, :] = u_ref[0]
    x = xbuf_ref[POOL_HALO:, :]
    grp = lax.broadcasted_iota(I32, (1, x.shape[1]), 1) // POOL_GROUP
    run = x
    sel = jnp.zeros(x.shape, F32)
    win = jnp.zeros((1, x.shape[1]), F32)
    for d in range(1, max(POOL_WINDOWS)):
        run = run + xbuf_ref[pl.ds(POOL_HALO - d, tm), :]
        if d + 1 in POOL_WINDOWS:
            gi = POOL_WINDOWS.index(d + 1)
            sel = jnp.where(grp == gi, run, sel)
            win = jnp.where(grp == gi, float(d + 1), win)
    pos = j * tm + lax.broadcasted_iota(I32, (tm, 1), 0)
    mean = sel / jnp.minimum((pos + 1).astype(F32), win)
    o_ref[0] = _dot((mean - x).astype(BF16), w_ref[...]) * sc_ref[...]
    xbuf_ref[0:POOL_HALO, :] = xbuf_ref[tm:tm + POOL_HALO, :]


def _multiscale_pool(u, pool_w, pool_scale, tm=512):
    B, S, n = u.shape
    tm = min(tm, S)
    wbd = _block_diag([pool_w[g] for g in range(pool_w.shape[0])]).astype(BF16)
    blk = lambda b, j: (b, j, 0)
    fix = lambda b, j: (0, 0)
    return pl.pallas_call(
        functools.partial(_pool_kernel, tm=tm),
        grid=(B, S // tm),
        in_specs=[pl.BlockSpec((1, tm, n), blk), pl.BlockSpec((n, n), fix), pl.BlockSpec((1, n), fix)],
        out_specs=pl.BlockSpec((1, tm, n), blk),
        out_shape=jax.ShapeDtypeStruct((B, S, n), F32),
        scratch_shapes=[pltpu.VMEM((tm + POOL_HALO, n), F32)],
        compiler_params=_cparams(("parallel", "arbitrary")),
        name="multiscale_pool",
    )(u, wbd, pool_scale.reshape(1, n))


def kernel(x, p, ab_w_in, ab_q_norm_g, ab_kv_norm_g, ab_w_uq, ab_w_uk, ab_w_uv, ab_w_qidx,
           ab_kidx_norm_g, ab_kidx_norm_b, ab_conv_w, ab_a_log, ab_dt_bias, ab_out_norm_g, ab_w_out,
           cd_w_in, cd_pool_w, cd_pool_scale, cd_w_out, ln_mix_g, ln_mix_b, router_w, router_b,
           w_gu, b_gu, w_down, b_down, ple_w_proj, ple_w_gate, ln_ffn_g, ln_ffn_b):
    B, S, D = x.shape
    T = B * S
    h = x.reshape(T, D)
    for i in range(DEPTH):
        j = i // 2
        if i % 2 == 0:
            cq, ckv, kidx, small, qkvz = _ab_in_proj(h, ab_w_in[j], ab_q_norm_g[j], ab_kv_norm_g[j],
                                                     ab_kidx_norm_g[j], ab_kidx_norm_b[j])
            sh = lambda t: t.reshape(B, S, -1)
            o_w = D_IDX + H_IDX
            o_a = _dsa(sh(cq), sh(ckv), sh(kidx), sh(small[:, D_IDX:o_w]),
                       ab_w_uq[j], ab_w_uk[j], ab_w_uv[j], ab_w_qidx[j])
            o_b = _gdn(sh(qkvz), sh(small[:, o_w:o_w + H_B]), sh(small[:, o_w + H_B:o_w + 2 * H_B]),
                       ab_conv_w[j], ab_a_log[j], ab_dt_bias[j], ab_out_norm_g[j])
            a1, a2, w_out = o_a.reshape(T, -1), o_b.reshape(T, -1), ab_w_out[j]
        else:
            q, k, v, u = _cd_in_proj(h, cd_w_in[j])
            sh = lambda t: t.reshape(B, S, -1)
            o_c = _dilated_attention(sh(q), sh(k), sh(v))
            o_d = _multiscale_pool(sh(u), cd_pool_w[j], cd_pool_scale[j])
            a1, a2, w_out = o_c.reshape(T, -1), o_d.reshape(T, -1), cd_w_out[j]
        h1, h1t, top_e, gate_t, rank, cnt = _post_mixer(a1, a2, h, w_out, ln_mix_g[i], ln_mix_b[i],
                                                        router_w[i], router_b[i])
        ys, dest = _moe(h1t, top_e, rank, cnt, w_gu, b_gu[i], w_down, b_down[i], i)
        h = _layer_tail(h1, ys, dest, gate_t, p[i].reshape(T, -1), ple_w_gate[i], ple_w_proj[i],
                        ln_ffn_g[i], ln_ffn_b[i])
    return h.reshape(B, S, D)
```

```python
import functools
import math

import jax
import jax.numpy as jnp
from jax import lax
from jax.experimental import pallas as pl
from jax.experimental.pallas import tpu as pltpu

F32 = jnp.float32
BF16 = jnp.bfloat16
I32 = jnp.int32

DEPTH = 2
H_A, D_NOPE, D_VA, R_Q, R_KV, H_IDX, D_IDX = 8, 64, 64, 256, 128, 8, 64
TOPK_MAX = 256
H_B, D_B, CONV_K, DN_CHUNK = 4, 128, 4, 64
H_C, DH_C = 12, 64
DILATED_PATTERNS = ((128, 1), (512, 4), (2048, 16))
POOL_WINDOWS = (2, 4, 8, 16)
POOL_GROUP = 64
N_EXPERTS, TOP_K = 32, 4
SWIGLU_LIMIT, SWIGLU_ALPHA = 7.0, 1.702
ALPHA = (2 * DEPTH) ** 0.25

VMEM_LIMIT_BYTES = 56 * 1024 * 1024
LANES = 128
HIGHEST = lax.Precision.HIGHEST
NEG_INF = float("-inf")
LOG2E = math.log2(math.e)
INT_MIN = -2 ** 31
KEY_NEG_INF = (0xFF800000 ^ 0x7FFFFFFF) - 2 ** 32


def _cparams(sem):
    return pltpu.CompilerParams(dimension_semantics=sem, vmem_limit_bytes=VMEM_LIMIT_BYTES)


def _dot(a, b, precision=None):
    return jnp.dot(a, b, preferred_element_type=F32, precision=precision)


def _dot_nt(a, b, precision=None):
    return lax.dot_general(a, b, (((1,), (1,)), ((), ())), preferred_element_type=F32,
                           precision=precision)


def _dot_tn(a, b, precision=None):
    return lax.dot_general(a, b, (((0,), (0,)), ((), ())), preferred_element_type=F32,
                           precision=precision)


def _sigmoid(x):
    return 1.0 / (1.0 + jnp.exp(-x))


def _silu(x):
    return x * _sigmoid(x)


def _layernorm_rows(x, g, b, eps=1e-5):
    mu = jnp.mean(x, axis=-1, keepdims=True)
    xc = x - mu
    var = jnp.mean(xc * xc, axis=-1, keepdims=True)
    return xc * lax.rsqrt(var + eps) * g + b


AB_SMALL = 128


def _ab_proj_kernel(x_ref, w_ref, gq_ref, gkv_ref, gk_ref, bk_ref,
                    cq_ref, ckv_ref, kidx_ref, small_ref, qkvz_ref):
    x = x_ref[...].astype(BF16)
    cq = _dot(x, w_ref[:, 0:R_Q])
    cq = cq * lax.rsqrt(jnp.mean(cq * cq, axis=-1, keepdims=True) + 1e-6) * gq_ref[...]
    cq_ref[...] = cq.astype(cq_ref.dtype)
    ckv = _dot(x, w_ref[:, R_Q:R_Q + R_KV])
    ckv = ckv * lax.rsqrt(jnp.mean(ckv * ckv, axis=-1, keepdims=True) + 1e-6) * gkv_ref[...]
    ckv_ref[...] = ckv.astype(ckv_ref.dtype)
    off = R_Q + R_KV
    sm = _dot(x, w_ref[:, off:off + AB_SMALL])
    small_ref[...] = sm
    lane = lax.broadcasted_iota(I32, sm.shape, 1)
    is_k = lane < D_IDX
    mu = jnp.sum(jnp.where(is_k, sm, 0.0), axis=-1, keepdims=True) * (1.0 / D_IDX)
    xc = jnp.where(is_k, sm - mu, 0.0)
    var = jnp.sum(xc * xc, axis=-1, keepdims=True) * (1.0 / D_IDX)
    kn = xc * lax.rsqrt(var + 1e-5) * gk_ref[...] + bk_ref[...]
    kidx_ref[...] = kn[:, :D_IDX].astype(kidx_ref.dtype)
    off += AB_SMALL
    qkvz_ref[...] = _dot(x, w_ref[:, off:])


def _ab_in_proj(h2d, w_in, q_norm_g, kv_norm_g, kidx_g, kidx_b, tm=512):
    T, D = h2d.shape
    W = H_B * D_B
    o = [0, R_Q, R_Q + R_KV, R_Q + R_KV + D_IDX, R_Q + R_KV + D_IDX + H_IDX]
    o_q = o[4]
    o_b = o_q + 4 * W
    pad = AB_SMALL - (D_IDX + H_IDX + 2 * H_B)
    w_perm = jnp.concatenate([
        w_in[:, o[0]:o[2]],
        w_in[:, o[2]:o[4]], w_in[:, o_b:o_b + 2 * H_B],
        jnp.zeros((D, pad), w_in.dtype),
        w_in[:, o_q:o_b],
    ], axis=1).astype(BF16)
    n_all = w_perm.shape[1]
    gk = jnp.zeros((1, AB_SMALL), F32).at[0, :D_IDX].set(kidx_g)
    bk = jnp.zeros((1, AB_SMALL), F32).at[0, :D_IDX].set(kidx_b)
    row = lambda i: (i, 0)
    fixed = lambda i: (0, 0)
    return pl.pallas_call(
        _ab_proj_kernel,
        grid=(T // tm,),
        in_specs=[pl.BlockSpec((tm, D), row), pl.BlockSpec((D, n_all), fixed),
                  pl.BlockSpec((1, R_Q), fixed), pl.BlockSpec((1, R_KV), fixed),
                  pl.BlockSpec((1, AB_SMALL), fixed), pl.BlockSpec((1, AB_SMALL), fixed)],
        out_specs=[pl.BlockSpec((tm, R_Q), row), pl.BlockSpec((tm, R_KV), row),
                   pl.BlockSpec((tm, D_IDX), row), pl.BlockSpec((tm, AB_SMALL), row),
                   pl.BlockSpec((tm, 4 * W), row)],
        out_shape=[jax.ShapeDtypeStruct((T, R_Q), BF16), jax.ShapeDtypeStruct((T, R_KV), BF16),
                   jax.ShapeDtypeStruct((T, D_IDX), BF16), jax.ShapeDtypeStruct((T, AB_SMALL), F32),
                   jax.ShapeDtypeStruct((T, 4 * W), F32)],
        compiler_params=_cparams(("parallel",)),
        name="ab_in_proj",
    )(h2d, w_perm, q_norm_g.reshape(1, -1), kv_norm_g.reshape(1, -1), gk, bk)


def _sort_key(x):
    bits = pltpu.bitcast(x + 0.0, I32)
    return jnp.where(bits < 0, bits ^ 0x7FFFFFFF, bits)


def _dsa_kernel(cq_ref, ckv_ref, kidx_ref, widx_ref, wqidx_ref, wuq_ref, wukbd_ref, wuvbd_ref,
                o_ref, key_ref, qlat_ref, m_ref, l_ref, acc_ref, *, qb, kc, topk, seq_bits):
    i = pl.program_id(1)
    nck = ((i + 1) * qb + kc - 1) // kc
    n_lt = kc // 128
    cq = cq_ref[0]
    widx = widx_ref[0] * (H_IDX ** -0.5 * D_IDX ** -0.5)
    q_pos = i * qb + lax.broadcasted_iota(I32, (qb, 1), 0)

    qidx = [_dot(cq, wqidx_ref[h]).astype(BF16) for h in range(H_IDX)]

    def idx_body(c, carry):
        kblk = kidx_ref[0, pl.ds(pl.multiple_of(c * kc, kc), kc), :]
        scs = [_dot_nt(qidx[h], kblk) for h in range(H_IDX)]
        isc = jnp.maximum(scs[0], 0.0) * widx[:, 0:1]
        for h in range(1, H_IDX):
            isc = isc + jnp.maximum(scs[h], 0.0) * widx[:, h:h + 1]
        k_pos = c * kc + lax.broadcasted_iota(I32, (1, kc), 1)
        key_ref[c] = jnp.where(k_pos <= q_pos, _sort_key(isc), KEY_NEG_INF)
        return carry

    lax.fori_loop(0, nck, idx_body, 0)

    rg = 128
    groups = list(range(0, qb, rg))

    def count_acc(r0, pred):
        def body(c, acc):
            for j in range(n_lt):
                acc = acc + jnp.where(pred(key_ref[c, r0:r0 + rg, j * 128:(j + 1) * 128], c, j), 1, 0)
            return acc
        return lax.fori_loop(0, nck, body, jnp.zeros((rg, 128), I32))

    def count_where(r0, pred):
        return jnp.sum(count_acc(r0, pred), axis=1, keepdims=True)

    def bit_cond(carry):
        b, _, dones = carry
        return (b < 32) & (functools.reduce(jnp.minimum, [jnp.min(d) for d in dones]) == 0)

    def bit_body(carry):
        b, ts, dones = carry
        bit = jnp.int32(1) << (31 - b)
        cands = [t + bit for t in ts]
        accs = [count_acc(r0, lambda k, c, j, cand=cand: k >= cand) for r0, cand in zip(groups, cands)]
        cnts = [jnp.sum(acc, axis=1, keepdims=True) for acc in accs]
        ts = tuple(jnp.where((cnt >= topk) & (d == 0), cand, t)
                   for t, d, cand, cnt in zip(ts, dones, cands, cnts))
        dones = tuple(jnp.where(cnt == topk, 1, d) for d, cnt in zip(dones, cnts))
        return b + 1, ts, dones

    settled = tuple(jnp.where(q_pos[r0:r0 + rg] + 1 <= topk, 1, 0) for r0 in groups)
    t0 = tuple(jnp.full((rg, 1), INT_MIN, I32) for _ in groups)
    _, thrs, dones = lax.while_loop(bit_cond, bit_body, (jnp.int32(0), t0, settled))
    thrs = [jnp.maximum(t, KEY_NEG_INF + 1) for t in thrs]

    def break_ties(r0, thr, need):
        def idx_of(c, j):
            return c * kc + j * 128 + lax.broadcasted_iota(I32, (1, 128), 1)

        def pos_body(b, m):
            cand = m + (jnp.int32(1) << (seq_bits - 1 - b))
            cnt = count_where(r0, lambda k, c, j: (k == thr) & (idx_of(c, j) < cand))
            return jnp.where(cnt < need, cand, m)

        m = lax.fori_loop(0, seq_bits, pos_body, jnp.zeros((rg, 1), I32))

        def demote(c, carry):
            k = key_ref[c, r0:r0 + rg, :]
            idx = c * kc + lax.broadcasted_iota(I32, (1, kc), 1)
            key_ref[c, r0:r0 + rg, :] = jnp.where((k == thr) & (idx > m), thr - 1, k)
            return carry

        lax.fori_loop(0, nck, demote, 0)

    for r0, thr_g, done_g in zip(groups, thrs, dones):
        @pl.when(jnp.min(done_g) == 0)
        def _(r0=r0, thr_g=thr_g):
            c_gt = count_where(r0, lambda k, c, j: k > thr_g)
            c_ge = count_where(r0, lambda k, c, j: k >= thr_g)
            need = topk - c_gt

            @pl.when(jnp.max(c_ge - c_gt - need) > 0)
            def _():
                break_ties(r0, thr_g, need)

    thr = jnp.concatenate(thrs, axis=0)

    q = _dot(cq, wuq_ref[...]).astype(BF16)
    qlat = _dot(q, wukbd_ref[...]) * (D_NOPE ** -0.5 * LOG2E)
    for h in range(H_A):
        qlat_ref[h * qb:(h + 1) * qb, :] = qlat[:, h * R_KV:(h + 1) * R_KV].astype(BF16)
    m_ref[...] = jnp.full(m_ref.shape, -1e30, F32)
    l_ref[...] = jnp.zeros(l_ref.shape, F32)
    acc_ref[...] = jnp.zeros(acc_ref.shape, F32)

    def att_body(c, carry):
        kv = ckv_ref[0, pl.ds(pl.multiple_of(c * kc, kc), kc), :]
        bias = jnp.where(key_ref[c] >= thr, 0.0, NEG_INF)
        s = _dot_nt(qlat_ref[...], kv)
        s = (s.reshape(H_A, qb, kc) + bias[None]).reshape(H_A * qb, kc)
        tiles = [s[:, j * 128:(j + 1) * 128] for j in range(n_lt)]
        m_cur = tiles[0]
        for t in tiles[1:]:
            m_cur = jnp.maximum(m_cur, t)
        m_old = m_ref[...]
        m_new = jnp.maximum(m_old, jnp.max(m_cur, axis=1, keepdims=True))
        ps = [jnp.exp2(t - m_new) for t in tiles]
        a = jnp.exp2(m_old - m_new)
        psum = ps[0]
        for t in ps[1:]:
            psum = psum + t
        l_ref[...] = a * l_ref[...] + psum
        p = jnp.concatenate([t.astype(BF16) for t in ps], axis=1)
        acc_ref[...] = a * acc_ref[...] + _dot(p, kv)
        m_ref[...] = m_new
        return carry

    lax.fori_loop(0, nck, att_body, 0)
    o_all = acc_ref[...] / jnp.sum(l_ref[...], axis=1, keepdims=True)
    o_lat = jnp.concatenate([o_all[h * qb:(h + 1) * qb, :] for h in range(H_A)], axis=1)
    o_ref[0] = _dot(o_lat.astype(BF16), wuvbd_ref[...]).astype(o_ref.dtype)


def _block_diag(blocks):
    n = len(blocks)
    r, c = blocks[0].shape
    out = jnp.zeros((n * r, n * c), blocks[0].dtype)
    for k, blk in enumerate(blocks):
        out = out.at[k * r:(k + 1) * r, k * c:(k + 1) * c].set(blk)
    return out


def _dsa(cq, ckv, kidx, widx, w_uq, w_uk, w_uv, w_qidx, qb=256, kc=512):
    B, S, _ = cq.shape
    kc = min(kc, S)
    topk = min(TOPK_MAX, S // 4)
    assert S % kc == 0 and kc % qb == 0 and kc >= topk
    seq_bits = max(1, (S - 1).bit_length())
    wqidx = jnp.transpose(w_qidx, (1, 0, 2)).astype(BF16)
    wuq = w_uq.reshape(R_Q, H_A * D_NOPE).astype(BF16)
    wukbd = _block_diag([w_uk[:, h, :].T for h in range(H_A)]).astype(BF16)
    wuvbd = _block_diag([w_uv[:, h, :] for h in range(H_A)]).astype(BF16)
    kern = functools.partial(_dsa_kernel, qb=qb, kc=kc, topk=topk, seq_bits=seq_bits)
    blk_q = lambda b, i: (b, i, 0)
    seq = lambda b, i: (b, 0, 0)
    fix2 = lambda b, i: (0, 0)
    fix3 = lambda b, i: (0, 0, 0)
    return pl.pallas_call(
        kern,
        grid=(B, S // qb),
        in_specs=[pl.BlockSpec((1, qb, R_Q), blk_q), pl.BlockSpec((1, S, R_KV), seq),
                  pl.BlockSpec((1, S, D_IDX), seq), pl.BlockSpec((1, qb, H_IDX), blk_q),
                  pl.BlockSpec(wqidx.shape, fix3), pl.BlockSpec(wuq.shape, fix2),
                  pl.BlockSpec(wukbd.shape, fix2), pl.BlockSpec(wuvbd.shape, fix2)],
        out_specs=pl.BlockSpec((1, qb, H_A * D_VA), blk_q),
        out_shape=jax.ShapeDtypeStruct((B, S, H_A * D_VA), BF16),
        scratch_shapes=[pltpu.VMEM((S // kc, qb, kc), I32),
                        pltpu.VMEM((H_A * qb, R_KV), BF16),
                        pltpu.VMEM((H_A * qb, R_KV), F32), pltpu.VMEM((H_A * qb, R_KV), F32),
                        pltpu.VMEM((H_A * qb, R_KV), F32)],
        compiler_params=_cparams(("parallel", "arbitrary")),
        name="dsa_attention",
    )(cq, ckv, kidx, widx, wqidx, wuq, wukbd, wuvbd)


GDN_HALO = 8


def _softplus(x):
    return jnp.maximum(x, 0.0) + jnp.log1p(jnp.exp(-jnp.abs(x)))


def _gdn_kernel(qkvz_ref, abc_ref, abr_ref, convw_ref, prm_c_ref, prm_r_ref, ng_ref, tri_ref,
                o_ref, xbuf_ref, state_ref, conv_ref, *, cb):
    C = DN_CHUNK
    W = H_B * D_B
    j = pl.program_id(1)

    @pl.when(j == 0)
    def _():
        xbuf_ref[0:GDN_HALO, :] = jnp.zeros((GDN_HALO, 3 * W), F32)
        state_ref[...] = jnp.zeros(state_ref.shape, F32)

    xbuf_ref[GDN_HALO:, :] = qkvz_ref[0, :, 0:3 * W]
    acc = xbuf_ref[GDN_HALO:, :] * convw_ref[CONV_K - 1:CONV_K, :]
    for t in range(CONV_K - 1):
        sh = CONV_K - 1 - t
        acc = acc + xbuf_ref[pl.ds(GDN_HALO - sh, cb), :] * convw_ref[t:t + 1, :]
    conv_ref[...] = _silu(acc)
    xbuf_ref[0:GDN_HALO, :] = xbuf_ref[cb:cb + GDN_HALO, :]

    abc = abc_ref[0]
    abr = abr_ref[0]
    beta_c = _sigmoid(abc)
    g_c = -jnp.exp(prm_r_ref[0:1, :]) * _softplus(abc + prm_r_ref[1:2, :])
    g_r = -jnp.exp(prm_c_ref[:, 0:1]) * _softplus(abr + prm_c_ref[:, 1:2])
    gc_c = _dot(tri_ref[...], g_c, precision=HIGHEST)
    gc_r = _dot_nt(g_r, tri_ref[...], precision=HIGHEST)

    ri = lax.broadcasted_iota(I32, (cb, cb), 0)
    ci = lax.broadcasted_iota(I32, (cb, cb), 1)
    same = (ri // C) == (ci // C)
    lower = same & (ri >= ci)
    strict = same & (ri > ci)
    eye = jnp.where(ri == ci, 1.0, 0.0)
    bf = lambda t: t.astype(BF16)

    heads = range(H_B)
    qs, ks, gcols, kbs, vbs, kqs, decays, xms, tms = [], [], [], [], [], [], [], [], []
    for h in heads:
        q = conv_ref[:, h * D_B:(h + 1) * D_B]
        k = conv_ref[:, W + h * D_B:W + (h + 1) * D_B]
        v = conv_ref[:, 2 * W + h * D_B:2 * W + (h + 1) * D_B]
        q = q * lax.rsqrt(jnp.sum(q * q, axis=-1, keepdims=True) + 1e-6) * (D_B ** -0.5)
        k = k * lax.rsqrt(jnp.sum(k * k, axis=-1, keepdims=True) + 1e-6)
        beta = beta_c[:, h:h + 1]
        gcol = gc_c[:, H_B + h:H_B + h + 1]
        grow = gc_r[H_B + h:H_B + h + 1, :]
        decay = jnp.exp(jnp.where(lower, gcol - grow, NEG_INF))
        kb = k * beta
        kq = bf(k)
        a_mat = jnp.where(strict, _dot_nt(bf(kb), kq) * decay, 0.0)
        qs.append(q), ks.append(k), gcols.append(gcol), kbs.append(kb), vbs.append(v * beta)
        kqs.append(kq), decays.append(decay), xms.append(-a_mat), tms.append(eye - a_mat)
    for _ in range(int(math.log2(C)) - 1):
        xqs = [bf(xm) for xm in xms]
        xms = [_dot(xq, xq) for xq in xqs]
        tms = [t_mat + _dot(bf(t_mat), bf(xm)) for t_mat, xm in zip(tms, xms)]
    egcs = [jnp.exp(gcol) for gcol in gcols]
    tqs = [bf(t_mat) for t_mat in tms]
    us = [_dot(tq, bf(vb)) for tq, vb in zip(tqs, vbs)]
    ws = [bf(_dot(tq, bf(kb * egc))) for tq, kb, egc in zip(tqs, kbs, egcs)]
    qks = [bf(jnp.where(lower, _dot_nt(bf(q), kq) * decay, 0.0)) for q, kq, decay in zip(qs, kqs, decays)]
    q_decs = [bf(q * egc) for q, egc in zip(qs, egcs)]
    for n in range(cb // C):
        r0 = n * C
        for h in heads:
            gcol, k = gcols[h], ks[h]
            glast = gcol[r0 + C - 1:r0 + C, :]
            k_dec = bf(k[r0:r0 + C] * jnp.exp(glast - gcol[r0:r0 + C]))
            st = state_ref[h]
            stq = bf(st)
            v_new = us[h][r0:r0 + C] - _dot(ws[h][r0:r0 + C], stq)
            vq = bf(v_new)
            o = _dot(q_decs[h][r0:r0 + C], stq) + _dot(qks[h][r0:r0 + C, r0:r0 + C], vq)
            state_ref[h] = st * jnp.exp(glast) + _dot_tn(k_dec, vq)
            o = o * lax.rsqrt(jnp.mean(o * o, axis=-1, keepdims=True) + 1e-6) * ng_ref[...]
            z = qkvz_ref[0, r0:r0 + C, 3 * W + h * D_B:3 * W + (h + 1) * D_B]
            o_ref[0, r0:r0 + C, h * D_B:(h + 1) * D_B] = (o * _silu(z)).astype(o_ref.dtype)


def _gdn(qkvz, b, a, conv_w, a_log, dt_bias, norm_g, cb=256):
    B, S, _ = qkvz.shape
    W = H_B * D_B
    cb = min(cb, S)
    abc = jnp.concatenate([b, a], axis=-1)
    abr = jnp.transpose(abc, (0, 2, 1))
    zeros = jnp.zeros((H_B,), F32)
    prm = jnp.stack([jnp.concatenate([zeros, a_log]), jnp.concatenate([zeros, dt_bias])])
    idx = jnp.arange(cb)
    tri = ((idx[:, None] >= idx[None, :]) & (idx[:, None] // DN_CHUNK == idx[None, :] // DN_CHUNK)).astype(F32)
    kern = functools.partial(_gdn_kernel, cb=cb)
    blk = lambda bi, j: (bi, j, 0)
    fix = lambda bi, j: (0, 0)
    return pl.pallas_call(
        kern,
        grid=(B, S // cb),
        in_specs=[pl.BlockSpec((1, cb, 4 * W), blk), pl.BlockSpec((1, cb, 2 * H_B), blk),
                  pl.BlockSpec((1, 2 * H_B, cb), lambda bi, j: (bi, 0, j)),
                  pl.BlockSpec((CONV_K, 3 * W), fix), pl.BlockSpec((2 * H_B, 2), fix),
                  pl.BlockSpec((2, 2 * H_B), fix), pl.BlockSpec((1, D_B), fix),
                  pl.BlockSpec((cb, cb), fix)],
        out_specs=pl.BlockSpec((1, cb, W), blk),
        out_shape=jax.ShapeDtypeStruct((B, S, W), BF16),
        scratch_shapes=[pltpu.VMEM((cb + GDN_HALO, 3 * W), F32), pltpu.VMEM((H_B, D_B, D_B), F32),
                        pltpu.VMEM((cb, 3 * W), F32)],
        compiler_params=_cparams(("parallel", "arbitrary")),
        name="gated_deltanet",
    )(qkvz, abc, abr, conv_w, prm.T, prm, norm_g.reshape(1, D_B), tri)


def _store_row_tiles(ref, val):
    n, d = val.shape
    sub = d // LANES
    for j in range(sub):
        ref[pl.ds(j, n, stride=sub), :] = val[:, j * LANES:(j + 1) * LANES]


def _load_row_tiles(ref, n, sub):
    return jnp.concatenate([ref[pl.ds(j, n, stride=sub), :] for j in range(sub)], axis=1)


def _post_mixer_kernel(a1_ref, a2_ref, h_ref, w1_ref, w2_ref, g_ref, b_ref, rw_ref, rb_ref, triu_ref,
                       h1_ref, h1t_ref, tope_ref, gate_ref, rank_ref, cnt_ref, carry_ref):
    i = pl.program_id(0)
    E, tm = rw_ref.shape[0], h_ref.shape[0]

    @pl.when(i == 0)
    def _():
        carry_ref[...] = jnp.zeros(carry_ref.shape, F32)

    mix = _dot(a1_ref[...].astype(BF16), w1_ref[...]) + _dot(a2_ref[...].astype(BF16), w2_ref[...])
    h1 = _layernorm_rows(ALPHA * h_ref[...] + mix, g_ref[...], b_ref[...])
    h1_ref[...] = h1
    _store_row_tiles(h1t_ref, h1)

    logits = _dot_nt(rw_ref[...], h1, precision=HIGHEST) + rb_ref[...]
    erow = lax.broadcasted_iota(I32, (E, tm), 0)
    sel = jnp.zeros((E, tm), F32)
    onehots, tops = [], []
    for k in range(TOP_K):
        mx = jnp.max(logits, axis=0, keepdims=True)
        idx = jnp.min(jnp.where(logits == mx, erow, E), axis=0, keepdims=True)
        oh = erow == idx
        logits = jnp.where(oh, NEG_INF, logits)
        sel = sel + jnp.where(oh, 1.0, 0.0)
        onehots.append(oh)
        tops.append(mx)
        tope_ref[k:k + 1, :] = idx
    exps = [jnp.exp(t - tops[0]) for t in tops]
    den = exps[0] + exps[1] + exps[2] + exps[3]
    for k in range(TOP_K):
        gate_ref[k:k + 1, :] = exps[k] / den
    incl = _dot(sel.astype(BF16), triu_ref[...])
    excl = incl - sel + carry_ref[:, 0:1]
    for k in range(TOP_K):
        rank_ref[k:k + 1, :] = jnp.sum(jnp.where(onehots[k], excl, 0.0), axis=0, keepdims=True).astype(I32)
    carry_ref[...] = carry_ref[...] + jnp.sum(sel, axis=1, keepdims=True)
    cnt_ref[...] = carry_ref[...]


def _post_mixer(a1, a2, h2d, w_out, ln_g, ln_b, router_w, router_b, tm=512):
    T, D = h2d.shape
    E = router_w.shape[1]
    n1, n2 = a1.shape[1], a2.shape[1]
    w1 = w_out[:n1].astype(BF16)
    w2 = w_out[n1:].astype(BF16)
    idx = jnp.arange(tm)
    triu = (idx[:, None] <= idx[None, :]).astype(BF16)
    row = lambda i: (i, 0)
    col = lambda i: (0, i)
    fix = lambda i: (0, 0)
    return pl.pallas_call(
        _post_mixer_kernel,
        grid=(T // tm,),
        in_specs=[pl.BlockSpec((tm, n1), row), pl.BlockSpec((tm, n2), row), pl.BlockSpec((tm, D), row),
                  pl.BlockSpec((n1, D), fix), pl.BlockSpec((n2, D), fix),
                  pl.BlockSpec((1, D), fix), pl.BlockSpec((1, D), fix),
                  pl.BlockSpec((E, D), fix), pl.BlockSpec((E, 1), fix), pl.BlockSpec((tm, tm), fix)],
        out_specs=[pl.BlockSpec((tm, D), row), pl.BlockSpec((tm * (D // LANES), LANES), row),
                   pl.BlockSpec((TOP_K, tm), col), pl.BlockSpec((TOP_K, tm), col),
                   pl.BlockSpec((TOP_K, tm), col), pl.BlockSpec((E, 128), fix)],
        out_shape=[jax.ShapeDtypeStruct((T, D), F32), jax.ShapeDtypeStruct((T * (D // LANES), LANES), F32),
                   jax.ShapeDtypeStruct((TOP_K, T), I32), jax.ShapeDtypeStruct((TOP_K, T), F32),
                   jax.ShapeDtypeStruct((TOP_K, T), I32), jax.ShapeDtypeStruct((E, 128), F32)],
        scratch_shapes=[pltpu.VMEM((E, 128), F32)],
        compiler_params=_cparams(("arbitrary",)),
        name="post_mixer_router",
    )(a1, a2, h2d, w1, w2, ln_g.reshape(1, D), ln_b.reshape(1, D), router_w.T, router_b.reshape(E, 1), triu)


def _dispatch_kernel(dest_ref, h_ref, xs_init_hbm, xs_hbm, sem, *, tt, n_tok, sub):
    del xs_init_hbm
    base = pl.program_id(0) * tt

    def row_copy(r, k):
        d = dest_ref[k * n_tok + base + r]
        return pltpu.make_async_copy(h_ref.at[pl.ds(pl.multiple_of(r * sub, sub), sub)],
                                     xs_hbm.at[pl.ds(pl.multiple_of(d * sub, sub), sub)], sem)

    def start(r, c):
        for k in range(TOP_K):
            row_copy(r, k).start(priority=k % 2)
        return c

    def wait(r, c):
        for k in range(TOP_K):
            row_copy(r, k).wait()
        return c

    lax.fori_loop(0, tt, start, 0)
    lax.fori_loop(0, tt, wait, 0)


def _moe_dispatch(h1t, dest_flat, n_slots, sub, tt=512):
    T = h1t.shape[0] // sub
    any_spec = pl.BlockSpec(memory_space=pl.ANY)
    return pl.pallas_call(
        functools.partial(_dispatch_kernel, tt=tt, n_tok=T, sub=sub),
        grid_spec=pltpu.PrefetchScalarGridSpec(
            num_scalar_prefetch=1, grid=(T // tt,),
            in_specs=[pl.BlockSpec((tt * sub, LANES), lambda i, d: (i, 0)), any_spec], out_specs=any_spec,
            scratch_shapes=[pltpu.SemaphoreType.DMA(())]),
        out_shape=jax.ShapeDtypeStruct((n_slots * sub, LANES), h1t.dtype),
        input_output_aliases={2: 0},
        compiler_params=_cparams(("arbitrary",)),
        name="moe_dispatch",
    )(dest_flat, h1t, jnp.zeros((n_slots * sub, LANES), h1t.dtype))


MOE_BLOCK = 256


def _moe_kernel(be_ref, nu_ref, first_ref, slot_ref, next_ref, x_ref, wgu_hbm, bg_ref, bu_ref, wd_hbm, bd_ref,
                y_ref, wgu_buf, wd_buf, wt_ref, wg_ref, wu_ref, wdb_ref, sem, *, bm, sub, layer, n_exp):
    i = pl.program_id(0)
    F = wd_buf.shape[1]

    def weight_copies(e, s):
        return (pltpu.make_async_copy(wgu_hbm.at[layer, e], wgu_buf.at[s], sem.at[0, s]),
                pltpu.make_async_copy(wd_hbm.at[layer, e], wd_buf.at[s], sem.at[1, s]))

    @pl.when((i < nu_ref[0]) & (first_ref[i] == 1))
    def _():
        e, s, nxt = be_ref[i], slot_ref[i], next_ref[i]

        @pl.when(i == 0)
        def _():
            for cp in weight_copies(e, s):
                cp.start()

        for cp in weight_copies(e, s):
            cp.wait()

        @pl.when(nxt < n_exp)
        def _():
            for cp in weight_copies(nxt, 1 - s):
                cp.start()

        ck = 512
        for c in range(0, 2 * F, ck):
            wt = wgu_buf[s, :, c:c + ck].T
            for j in range(sub):
                wt_ref[pl.ds(c * sub + j, ck, stride=sub), :] = wt[:, j * LANES:(j + 1) * LANES]
        for j in range(sub):
            cols = slice(j * LANES, (j + 1) * LANES)
            wg_ref[:, cols] = wt_ref[pl.ds(j, F, stride=2 * sub), :].astype(BF16)
            wu_ref[:, cols] = wt_ref[pl.ds(sub + j, F, stride=2 * sub), :].astype(BF16)
        wdb_ref[...] = wd_buf[s].astype(BF16)

    @pl.when(i < nu_ref[0])
    def _():
        x = _load_row_tiles(x_ref, bm, sub).astype(BF16)
        g = _dot_nt(x, wg_ref[...]) + bg_ref[0]
        u = _dot_nt(x, wu_ref[...]) + bu_ref[0]
        gt = jnp.minimum(g, SWIGLU_LIMIT)
        up = jnp.clip(u, -SWIGLU_LIMIT, SWIGLU_LIMIT)
        hid = (up + 1.0) * (gt * _sigmoid(gt * SWIGLU_ALPHA))
        _store_row_tiles(y_ref, _dot(hid.astype(BF16), wdb_ref[...]) + bd_ref[0])

    @pl.when(i >= nu_ref[0])
    def _():
        y_ref[...] = jnp.zeros(y_ref.shape, y_ref.dtype)


def _moe_experts(xs, blk_e, n_used, padded, w_gu, bg, bu, w_down, bd, sub, layer):
    _, E, D, F2 = w_gu.shape
    F = F2 // 2
    bm = MOE_BLOCK
    P = xs.shape[0] // sub
    nblk = P // bm
    prev_e = jnp.concatenate([jnp.full((1,), -1, I32), blk_e[:-1]])
    first = (blk_e != prev_e).astype(I32)
    slot = (jnp.cumsum(first) - 1) % 2
    ids = jnp.where(padded > 0, jnp.arange(E, dtype=I32), E)
    later_min = lax.cummin(ids[::-1])[::-1]
    next_used = jnp.concatenate([later_min[1:], jnp.full((1,), E, I32)])
    nxt = jnp.sum(jnp.where(blk_e[:, None] == jnp.arange(E, dtype=I32)[None, :], next_used[None, :], 0), axis=1)
    wsel = lambda i, *_: (0, 0, 0)
    esel = lambda i, be, *_: (be[i], 0, 0)
    row = lambda i, *_: (i, 0)
    any_spec = pl.BlockSpec(memory_space=pl.ANY)
    return pl.pallas_call(
        functools.partial(_moe_kernel, bm=bm, sub=sub, layer=layer, n_exp=E),
        grid_spec=pltpu.PrefetchScalarGridSpec(
            num_scalar_prefetch=5,
            grid=(nblk,),
            in_specs=[pl.BlockSpec((bm * sub, LANES), row), any_spec,
                      pl.BlockSpec((1, 1, F), esel), pl.BlockSpec((1, 1, F), esel),
                      any_spec, pl.BlockSpec((1, 1, D), esel)],
            out_specs=pl.BlockSpec((bm * sub, LANES), row),
            scratch_shapes=[pltpu.VMEM((2, D, F2), F32), pltpu.VMEM((2, F, D), F32),
                            pltpu.VMEM((F2 * sub, LANES), F32), pltpu.VMEM((F, D), BF16), pltpu.VMEM((F, D), BF16),
                            pltpu.VMEM((F, D), BF16), pltpu.SemaphoreType.DMA((2, 2))]),
        out_shape=jax.ShapeDtypeStruct((P * sub, LANES), F32),
        compiler_params=_cparams(("arbitrary",)),
        name="moe_experts",
    )(blk_e, n_used, first, slot.astype(I32), nxt.astype(I32), xs, w_gu, bg, bu, w_down, bd)


def _moe(h1t, top_e, rank, cnt, w_gu, b_gu, w_down, b_down, layer):
    E, D = w_gu.shape[1], w_gu.shape[2]
    sub = D // LANES
    T = h1t.shape[0] // sub
    bm = MOE_BLOCK
    counts = cnt[:, 0].astype(I32)
    padded = (counts + bm - 1) // bm * bm
    pad_end = jnp.cumsum(padded)
    pad_start = pad_end - padded
    e_ids = jnp.arange(E, dtype=I32)[:, None, None]
    start_of = jnp.sum(jnp.where(top_e[None] == e_ids, pad_start[:, None, None], 0), axis=0)
    dest = (start_of + rank).reshape(-1)
    nblk = (T * TOP_K) // bm + E
    blk_first = jnp.arange(nblk, dtype=I32) * bm
    blk_e = jnp.minimum(jnp.sum(pad_end[None, :] <= blk_first[:, None], axis=1), E - 1).astype(I32)
    n_used = (pad_end[-1:] // bm).astype(I32)
    xs = _moe_dispatch(h1t, dest, nblk * bm, sub)
    ys = _moe_experts(xs, blk_e, n_used, padded, w_gu, b_gu[:, None, 0::2], b_gu[:, None, 1::2],
                      w_down, b_down[:, None, :], sub, layer)
    return ys, dest


def _tail_kernel(dest_ref, h1_ref, gate_ref, p_ref, wg_ref, wp_ref, g_ref, b_ref, ys_hbm, o_ref,
                 ybuf, sem, *, tm, n_tok, sub):
    i = pl.program_id(0)

    def row_copy(tile, slot, r, k):
        d = dest_ref[k * n_tok + tile * tm + r]
        return pltpu.make_async_copy(ys_hbm.at[pl.ds(pl.multiple_of(d * sub, sub), sub)],
                                     ybuf.at[slot, k, pl.ds(pl.multiple_of(r * sub, sub), sub)], sem.at[slot])

    def start_tile(tile, slot):
        def body(r, c):
            for k in range(TOP_K):
                row_copy(tile, slot, r, k).start(priority=k % 2)
            return c
        lax.fori_loop(0, tm, body, 0)

    def wait_tile(tile, slot):
        def body(r, c):
            for k in range(TOP_K):
                row_copy(tile, slot, r, k).wait()
            return c
        lax.fori_loop(0, tm, body, 0)

    @pl.when(i == 0)
    def _():
        start_tile(0, 0)

    h1 = h1_ref[...]
    ple = _sigmoid(_dot(h1.astype(BF16), wg_ref[...])) * _dot(p_ref[...].astype(BF16), wp_ref[...])
    x = ALPHA * h1 + ple

    for slot in range(2):
        @pl.when(i % 2 == slot)
        def _(slot=slot):
            @pl.when(i + 1 < pl.num_programs(0))
            def _():
                start_tile(i + 1, 1 - slot)

            wait_tile(i, slot)
            cols = []
            for j in range(sub):
                col = ybuf[slot, 0, pl.ds(j, tm, stride=sub), :] * gate_ref[:, 0:1]
                for k in range(1, TOP_K):
                    col = col + ybuf[slot, k, pl.ds(j, tm, stride=sub), :] * gate_ref[:, k:k + 1]
                cols.append(col)
            ffn = jnp.concatenate(cols, axis=1)
            o_ref[...] = _layernorm_rows(x + ffn, g_ref[...], b_ref[...])


def _layer_tail(h1, ys, dest_flat, gate_t, p2d, ple_w_gate, ple_w_proj, ln_g, ln_b, tm=256):
    T, D = h1.shape
    PD = p2d.shape[1]
    sub = D // LANES
    row = lambda i, d: (i, 0)
    fix = lambda i, d: (0, 0)
    return pl.pallas_call(
        functools.partial(_tail_kernel, tm=tm, n_tok=T, sub=sub),
        grid_spec=pltpu.PrefetchScalarGridSpec(
            num_scalar_prefetch=1, grid=(T // tm,),
            in_specs=[pl.BlockSpec((tm, D), row), pl.BlockSpec((tm, TOP_K), row),
                      pl.BlockSpec((tm, PD), row), pl.BlockSpec((D, D), fix), pl.BlockSpec((PD, D), fix),
                      pl.BlockSpec((1, D), fix), pl.BlockSpec((1, D), fix),
                      pl.BlockSpec(memory_space=pl.ANY)],
            out_specs=pl.BlockSpec((tm, D), row),
            scratch_shapes=[pltpu.VMEM((2, TOP_K, tm * sub, LANES), F32), pltpu.SemaphoreType.DMA((2,))]),
        out_shape=jax.ShapeDtypeStruct((T, D), F32),
        compiler_params=_cparams(("arbitrary",)),
        name="layer_tail",
    )(dest_flat, h1, gate_t.T, p2d, ple_w_gate.astype(BF16), ple_w_proj.astype(BF16),
      ln_g.reshape(1, D), ln_b.reshape(1, D), ys)


def _cd_proj_kernel(x_ref, w_ref, q_ref, k_ref, v_ref, u_ref):
    x = x_ref[...].astype(BF16)
    n = H_C * DH_C
    q_ref[...] = (_dot(x, w_ref[:, 0:n]) * (DH_C ** -0.5 * LOG2E)).astype(q_ref.dtype)
    k_ref[...] = _dot(x, w_ref[:, n:2 * n]).astype(k_ref.dtype)
    v_ref[...] = _dot(x, w_ref[:, 2 * n:3 * n]).astype(v_ref.dtype)
    u_ref[...] = _dot(x, w_ref[:, 3 * n:])


def _cd_in_proj(h2d, w_in, tm=512):
    T, D = h2d.shape
    n = H_C * DH_C
    nu = w_in.shape[1] - 3 * n
    row = lambda i: (i, 0)
    return pl.pallas_call(
        _cd_proj_kernel,
        grid=(T // tm,),
        in_specs=[pl.BlockSpec((tm, D), row), pl.BlockSpec(w_in.shape, lambda i: (0, 0))],
        out_specs=[pl.BlockSpec((tm, n), row)] * 3 + [pl.BlockSpec((tm, nu), row)],
        out_shape=[jax.ShapeDtypeStruct((T, n), BF16)] * 3 + [jax.ShapeDtypeStruct((T, nu), F32)],
        compiler_params=_cparams(("parallel",)),
        name="cd_in_proj",
    )(h2d, w_in.astype(BF16))


def _dilated_bias_table(qb):
    import numpy as np
    max_w = max(w for w, _ in DILATED_PATTERNS)
    ndc = max_w // qb + 1
    r = np.arange(qb)[:, None]
    j = np.arange(qb)[None, :]
    tbl = np.empty((ndc, qb, qb), np.float32)
    for dc in range(ndc):
        delta = dc * qb + r - j
        mult = np.zeros((qb, qb), np.float64)
        for w, d in DILATED_PATTERNS:
            mult += (delta >= 0) & (delta <= w) & (delta % d == 0)
        with np.errstate(divide="ignore"):
            tbl[dc] = np.log2(mult)
    return jnp.asarray(tbl)


def _dilated_kernel(q_ref, k_ref, v_ref, bias_ref, o_ref, q2_ref, m_ref, l_ref, acc_ref, *, qb, ndc):
    i = pl.program_id(2)
    n_lt = qb // 128
    q = q_ref[0]
    lo = lax.broadcasted_iota(I32, q.shape, 1) < DH_C
    zero = jnp.zeros(q.shape, q.dtype)
    q2_ref[0:qb, :] = jnp.where(lo, q, zero)
    q2_ref[qb:, :] = jnp.where(lo, zero, q)
    m_ref[...] = jnp.full(m_ref.shape, -1e30, F32)
    l_ref[...] = jnp.zeros(l_ref.shape, F32)
    acc_ref[...] = jnp.zeros(acc_ref.shape, F32)

    def body(c, carry):
        r0 = pl.multiple_of(c * qb, qb)
        kk = k_ref[0, pl.ds(r0, qb), :]
        vv = v_ref[0, pl.ds(r0, qb), :]
        s = _dot_nt(q2_ref[...], kk)
        s = (s.reshape(2, qb, qb) + bias_ref[i - c][None]).reshape(2 * qb, qb)
        tiles = [s[:, j * 128:(j + 1) * 128] for j in range(n_lt)]
        m_cur = tiles[0]
        for t in tiles[1:]:
            m_cur = jnp.maximum(m_cur, t)
        m_old = m_ref[...]
        m_new = jnp.maximum(m_old, jnp.max(m_cur, axis=1, keepdims=True))
        ps = [jnp.exp2(t - m_new) for t in tiles]
        a = jnp.exp2(m_old - m_new)
        psum = ps[0]
        for t in ps[1:]:
            psum = psum + t
        l_ref[...] = a * l_ref[...] + psum
        p = jnp.concatenate([t.astype(BF16) for t in ps], axis=1)
        acc_ref[...] = a * acc_ref[...] + _dot(p, vv)
        m_ref[...] = m_new
        return carry

    lax.fori_loop(jnp.maximum(i - (ndc - 1), 0), i + 1, body, 0)
    o_all = acc_ref[...] / jnp.sum(l_ref[...], axis=1, keepdims=True)
    o_ref[0] = jnp.where(lo, o_all[0:qb, :], o_all[qb:, :]).astype(o_ref.dtype)


def _dilated_attention(q, k, v, qb=512):
    B, S, n = q.shape
    qb = min(qb, S)
    bias = _dilated_bias_table(qb)
    ndc = bias.shape[0]
    pw = 2 * DH_C
    kern = functools.partial(_dilated_kernel, qb=qb, ndc=ndc)
    blk = lambda b, pr, i: (b, i, pr)
    seq = lambda b, pr, i: (b, 0, pr)
    return pl.pallas_call(
        kern,
        grid=(B, n // pw, S // qb),
        in_specs=[pl.BlockSpec((1, qb, pw), blk), pl.BlockSpec((1, S, pw), seq),
                  pl.BlockSpec((1, S, pw), seq), pl.BlockSpec(bias.shape, lambda b, pr, i: (0, 0, 0))],
        out_specs=pl.BlockSpec((1, qb, pw), blk),
        out_shape=jax.ShapeDtypeStruct((B, S, n), BF16),
        scratch_shapes=[pltpu.VMEM((2 * qb, pw), BF16), pltpu.VMEM((2 * qb, pw), F32),
                        pltpu.VMEM((2 * qb, pw), F32), pltpu.VMEM((2 * qb, pw), F32)],
        compiler_params=_cparams(("parallel", "parallel", "arbitrary")),
        name="dilated_attention",
    )(q, k, v, bias)


POOL_HALO = 16


def _pool_kernel(u_ref, w_ref, sc_ref, o_ref, xbuf_ref, *, tm):
    j = pl.program_id(1)

    @pl.when(j == 0)
    def _():
        xbuf_ref[0:POOL_HALO, :] = jnp.zeros((POOL_HALO, xbuf_ref.shape[1]), F32)

    xbuf_ref[POOL_HALO:, :] = u_ref[0]
    x = xbuf_ref[POOL_HALO:, :]
    grp = lax.broadcasted_iota(I32, (1, x.shape[1]), 1) // POOL_GROUP
    run = x
    sel = jnp.zeros(x.shape, F32)
    win = jnp.zeros((1, x.shape[1]), F32)
    for d in range(1, max(POOL_WINDOWS)):
        run = run + xbuf_ref[pl.ds(POOL_HALO - d, tm), :]
        if d + 1 in POOL_WINDOWS:
            gi = POOL_WINDOWS.index(d + 1)
            sel = jnp.where(grp == gi, run, sel)
            win = jnp.where(grp == gi, float(d + 1), win)
    pos = j * tm + lax.broadcasted_iota(I32, (tm, 1), 0)
    mean = sel / jnp.minimum((pos + 1).astype(F32), win)
    o_ref[0] = _dot((mean - x).astype(BF16), w_ref[...]) * sc_ref[...]
    xbuf_ref[0:POOL_HALO, :] = xbuf_ref[tm:tm + POOL_HALO, :]


def _multiscale_pool(u, pool_w, pool_scale, tm=512):
    B, S, n = u.shape
    tm = min(tm, S)
    wbd = _block_diag([pool_w[g] for g in range(pool_w.shape[0])]).astype(BF16)
    blk = lambda b, j: (b, j, 0)
    fix = lambda b, j: (0, 0)
    return pl.pallas_call(
        functools.partial(_pool_kernel, tm=tm),
        grid=(B, S // tm),
        in_specs=[pl.BlockSpec((1, tm, n), blk), pl.BlockSpec((n, n), fix), pl.BlockSpec((1, n), fix)],
        out_specs=pl.BlockSpec((1, tm, n), blk),
        out_shape=jax.ShapeDtypeStruct((B, S, n), F32),
        scratch_shapes=[pltpu.VMEM((tm + POOL_HALO, n), F32)],
        compiler_params=_cparams(("parallel", "arbitrary")),
        name="multiscale_pool",
    )(u, wbd, pool_scale.reshape(1, n))


def kernel(x, p, ab_w_in, ab_q_norm_g, ab_kv_norm_g, ab_w_uq, ab_w_uk, ab_w_uv, ab_w_qidx,
           ab_kidx_norm_g, ab_kidx_norm_b, ab_conv_w, ab_a_log, ab_dt_bias, ab_out_norm_g, ab_w_out,
           cd_w_in, cd_pool_w, cd_pool_scale, cd_w_out, ln_mix_g, ln_mix_b, router_w, router_b,
           w_gu, b_gu, w_down, b_down, ple_w_proj, ple_w_gate, ln_ffn_g, ln_ffn_b):
    B, S, D = x.shape
    T = B * S
    h = x.reshape(T, D)
    for i in range(DEPTH):
        j = i // 2
        if i % 2 == 0:
            cq, ckv, kidx, small, qkvz = _ab_in_proj(h, ab_w_in[j], ab_q_norm_g[j], ab_kv_norm_g[j],
                                                     ab_kidx_norm_g[j], ab_kidx_norm_b[j])
            sh = lambda t: t.reshape(B, S, -1)
            o_w = D_IDX + H_IDX
            o_a = _dsa(sh(cq), sh(ckv), sh(kidx), sh(small[:, D_IDX:o_w]),
                       ab_w_uq[j], ab_w_uk[j], ab_w_uv[j], ab_w_qidx[j])
            o_b = _gdn(sh(qkvz), sh(small[:, o_w:o_w + H_B]), sh(small[:, o_w + H_B:o_w + 2 * H_B]),
                       ab_conv_w[j], ab_a_log[j], ab_dt_bias[j], ab_out_norm_g[j])
            a1, a2, w_out = o_a.reshape(T, -1), o_b.reshape(T, -1), ab_w_out[j]
        else:
            q, k, v, u = _cd_in_proj(h, cd_w_in[j])
            sh = lambda t: t.reshape(B, S, -1)
            o_c = _dilated_attention(sh(q), sh(k), sh(v))
            o_d = _multiscale_pool(sh(u), cd_pool_w[j], cd_pool_scale[j])
            a1, a2, w_out = o_c.reshape(T, -1), o_d.reshape(T, -1), cd_w_out[j]
        h1, h1t, top_e, gate_t, rank, cnt = _post_mixer(a1, a2, h, w_out, ln_mix_g[i], ln_mix_b[i],
                                                        router_w[i], router_b[i])
        ys, dest = _moe(h1t, top_e, rank, cnt, w_gu, b_gu[i], w_down, b_down[i], i)
        h = _layer_tail(h1, ys, dest, gate_t, p[i].reshape(T, -1), ple_w_gate[i], ple_w_proj[i],
                        ln_ffn_g[i], ln_ffn_b[i])
    return h.reshape(B, S, D)
```

```python
import functools
import math

import jax
import jax.numpy as jnp
from jax import lax
from jax.experimental import pallas as pl
from jax.experimental.pallas import tpu as pltpu

F32 = jnp.float32
BF16 = jnp.bfloat16
I32 = jnp.int32

DEPTH = 2
H_A, D_NOPE, D_VA, R_Q, R_KV, H_IDX, D_IDX = 8, 64, 64, 256, 128, 8, 64
TOPK_MAX = 256
H_B, D_B, CONV_K, DN_CHUNK = 4, 128, 4, 64
H_C, DH_C = 12, 64
DILATED_PATTERNS = ((128, 1), (512, 4), (2048, 16))
POOL_WINDOWS = (2, 4, 8, 16)
POOL_GROUP = 64
N_EXPERTS, TOP_K = 32, 4
SWIGLU_LIMIT, SWIGLU_ALPHA = 7.0, 1.702
ALPHA = (2 * DEPTH) ** 0.25

VMEM_LIMIT_BYTES = 56 * 1024 * 1024
LANES = 128
HIGHEST = lax.Precision.HIGHEST
NEG_INF = float("-inf")
LOG2E = math.log2(math.e)
INT_MIN = -2 ** 31
KEY_NEG_INF = (0xFF800000 ^ 0x7FFFFFFF) - 2 ** 32


def _cparams(sem):
    return pltpu.CompilerParams(dimension_semantics=sem, vmem_limit_bytes=VMEM_LIMIT_BYTES)


def _dot(a, b, precision=None):
    return jnp.dot(a, b, preferred_element_type=F32, precision=precision)


def _dot_nt(a, b, precision=None):
    return lax.dot_general(a, b, (((1,), (1,)), ((), ())), preferred_element_type=F32,
                           precision=precision)


def _dot_tn(a, b, precision=None):
    return lax.dot_general(a, b, (((0,), (0,)), ((), ())), preferred_element_type=F32,
                           precision=precision)


def _sigmoid(x):
    return 1.0 / (1.0 + jnp.exp(-x))


def _silu(x):
    return x * _sigmoid(x)


def _layernorm_rows(x, g, b, eps=1e-5):
    mu = jnp.mean(x, axis=-1, keepdims=True)
    xc = x - mu
    var = jnp.mean(xc * xc, axis=-1, keepdims=True)
    return xc * lax.rsqrt(var + eps) * g + b


AB_SMALL = 128


def _ab_proj_kernel(x_ref, w_ref, gq_ref, gkv_ref, gk_ref, bk_ref,
                    cq_ref, ckv_ref, kidx_ref, small_ref, qkvz_ref):
    x = x_ref[...].astype(BF16)
    cq = _dot(x, w_ref[:, 0:R_Q])
    cq = cq * lax.rsqrt(jnp.mean(cq * cq, axis=-1, keepdims=True) + 1e-6) * gq_ref[...]
    cq_ref[...] = cq.astype(cq_ref.dtype)
    ckv = _dot(x, w_ref[:, R_Q:R_Q + R_KV])
    ckv = ckv * lax.rsqrt(jnp.mean(ckv * ckv, axis=-1, keepdims=True) + 1e-6) * gkv_ref[...]
    ckv_ref[...] = ckv.astype(ckv_ref.dtype)
    off = R_Q + R_KV
    sm = _dot(x, w_ref[:, off:off + AB_SMALL])
    small_ref[...] = sm
    lane = lax.broadcasted_iota(I32, sm.shape, 1)
    is_k = lane < D_IDX
    mu = jnp.sum(jnp.where(is_k, sm, 0.0), axis=-1, keepdims=True) * (1.0 / D_IDX)
    xc = jnp.where(is_k, sm - mu, 0.0)
    var = jnp.sum(xc * xc, axis=-1, keepdims=True) * (1.0 / D_IDX)
    kn = xc * lax.rsqrt(var + 1e-5) * gk_ref[...] + bk_ref[...]
    kidx_ref[...] = kn[:, :D_IDX].astype(kidx_ref.dtype)
    off += AB_SMALL
    qkvz_ref[...] = _dot(x, w_ref[:, off:])


def _ab_in_proj(h2d, w_in, q_norm_g, kv_norm_g, kidx_g, kidx_b, tm=512):
    T, D = h2d.shape
    W = H_B * D_B
    o = [0, R_Q, R_Q + R_KV, R_Q + R_KV + D_IDX, R_Q + R_KV + D_IDX + H_IDX]
    o_q = o[4]
    o_b = o_q + 4 * W
    pad = AB_SMALL - (D_IDX + H_IDX + 2 * H_B)
    w_perm = jnp.concatenate([
        w_in[:, o[0]:o[2]],
        w_in[:, o[2]:o[4]], w_in[:, o_b:o_b + 2 * H_B],
        jnp.zeros((D, pad), w_in.dtype),
        w_in[:, o_q:o_b],
    ], axis=1).astype(BF16)
    n_all = w_perm.shape[1]
    gk = jnp.zeros((1, AB_SMALL), F32).at[0, :D_IDX].set(kidx_g)
    bk = jnp.zeros((1, AB_SMALL), F32).at[0, :D_IDX].set(kidx_b)
    row = lambda i: (i, 0)
    fixed = lambda i: (0, 0)
    return pl.pallas_call(
        _ab_proj_kernel,
        grid=(T // tm,),
        in_specs=[pl.BlockSpec((tm, D), row), pl.BlockSpec((D, n_all), fixed),
                  pl.BlockSpec((1, R_Q), fixed), pl.BlockSpec((1, R_KV), fixed),
                  pl.BlockSpec((1, AB_SMALL), fixed), pl.BlockSpec((1, AB_SMALL), fixed)],
        out_specs=[pl.BlockSpec((tm, R_Q), row), pl.BlockSpec((tm, R_KV), row),
                   pl.BlockSpec((tm, D_IDX), row), pl.BlockSpec((tm, AB_SMALL), row),
                   pl.BlockSpec((tm, 4 * W), row)],
        out_shape=[jax.ShapeDtypeStruct((T, R_Q), BF16), jax.ShapeDtypeStruct((T, R_KV), BF16),
                   jax.ShapeDtypeStruct((T, D_IDX), BF16), jax.ShapeDtypeStruct((T, AB_SMALL), F32),
                   jax.ShapeDtypeStruct((T, 4 * W), F32)],
        compiler_params=_cparams(("parallel",)),
        name="ab_in_proj",
    )(h2d, w_perm, q_norm_g.reshape(1, -1), kv_norm_g.reshape(1, -1), gk, bk)


def _sort_key(x):
    bits = pltpu.bitcast(x + 0.0, I32)
    return jnp.where(bits < 0, bits ^ 0x7FFFFFFF, bits)


def _dsa_kernel(cq_ref, ckv_ref, kidx_ref, widx_ref, wqidx_ref, wuq_ref, wukbd_ref, wuvbd_ref,
                o_ref, key_ref, qlat_ref, m_ref, l_ref, acc_ref, *, qb, kc, topk, seq_bits):
    i = pl.program_id(1)
    nck = ((i + 1) * qb + kc - 1) // kc
    n_lt = kc // 128
    cq = cq_ref[0]
    widx = widx_ref[0] * (H_IDX ** -0.5 * D_IDX ** -0.5)
    q_pos = i * qb + lax.broadcasted_iota(I32, (qb, 1), 0)

    qidx = [_dot(cq, wqidx_ref[h]).astype(BF16) for h in range(H_IDX)]

    def idx_body(c, carry):
        kblk = kidx_ref[0, pl.ds(pl.multiple_of(c * kc, kc), kc), :]
        scs = [_dot_nt(qidx[h], kblk) for h in range(H_IDX)]
        isc = jnp.maximum(scs[0], 0.0) * widx[:, 0:1]
        for h in range(1, H_IDX):
            isc = isc + jnp.maximum(scs[h], 0.0) * widx[:, h:h + 1]
        k_pos = c * kc + lax.broadcasted_iota(I32, (1, kc), 1)
        key_ref[c] = jnp.where(k_pos <= q_pos, _sort_key(isc), KEY_NEG_INF)
        return carry

    lax.fori_loop(0, nck, idx_body, 0)

    rg = 128
    groups = list(range(0, qb, rg))

    def count_acc(r0, pred):
        def body(c, acc):
            for j in range(n_lt):
                acc = acc + jnp.where(pred(key_ref[c, r0:r0 + rg, j * 128:(j + 1) * 128], c, j), 1, 0)
            return acc
        return lax.fori_loop(0, nck, body, jnp.zeros((rg, 128), I32))

    def count_where(r0, pred):
        return jnp.sum(count_acc(r0, pred), axis=1, keepdims=True)

    def bit_cond(carry):
        b, _, dones = carry
        return (b < 32) & (functools.reduce(jnp.minimum, [jnp.min(d) for d in dones]) == 0)

    def bit_body(carry):
        b, ts, dones = carry
        bit = jnp.int32(1) << (31 - b)
        cands = [t + bit for t in ts]
        accs = [count_acc(r0, lambda k, c, j, cand=cand: k >= cand) for r0, cand in zip(groups, cands)]
        cnts = [jnp.sum(acc, axis=1, keepdims=True) for acc in accs]
        ts = tuple(jnp.where((cnt >= topk) & (d == 0), cand, t)
                   for t, d, cand, cnt in zip(ts, dones, cands, cnts))
        dones = tuple(jnp.where(cnt == topk, 1, d) for d, cnt in zip(dones, cnts))
        return b + 1, ts, dones

    settled = tuple(jnp.where(q_pos[r0:r0 + rg] + 1 <= topk, 1, 0) for r0 in groups)
    t0 = tuple(jnp.full((rg, 1), INT_MIN, I32) for _ in groups)
    _, thrs, dones = lax.while_loop(bit_cond, bit_body, (jnp.int32(0), t0, settled))
    thrs = [jnp.maximum(t, KEY_NEG_INF + 1) for t in thrs]

    def break_ties(r0, thr, need):
        def idx_of(c, j):
            return c * kc + j * 128 + lax.broadcasted_iota(I32, (1, 128), 1)

        def pos_body(b, m):
            cand = m + (jnp.int32(1) << (seq_bits - 1 - b))
            cnt = count_where(r0, lambda k, c, j: (k == thr) & (idx_of(c, j) < cand))
            return jnp.where(cnt < need, cand, m)

        m = lax.fori_loop(0, seq_bits, pos_body, jnp.zeros((rg, 1), I32))

        def demote(c, carry):
            k = key_ref[c, r0:r0 + rg, :]
            idx = c * kc + lax.broadcasted_iota(I32, (1, kc), 1)
            key_ref[c, r0:r0 + rg, :] = jnp.where((k == thr) & (idx > m), thr - 1, k)
            return carry

        lax.fori_loop(0, nck, demote, 0)

    for r0, thr_g, done_g in zip(groups, thrs, dones):
        @pl.when(jnp.min(done_g) == 0)
        def _(r0=r0, thr_g=thr_g):
            c_gt = count_where(r0, lambda k, c, j: k > thr_g)
            c_ge = count_where(r0, lambda k, c, j: k >= thr_g)
            need = topk - c_gt

            @pl.when(jnp.max(c_ge - c_gt - need) > 0)
            def _():
                break_ties(r0, thr_g, need)

    thr = jnp.concatenate(thrs, axis=0)

    q = _dot(cq, wuq_ref[...]).astype(BF16)
    qlat = _dot(q, wukbd_ref[...]) * (D_NOPE ** -0.5 * LOG2E)
    for h in range(H_A):
        qlat_ref[h * qb:(h + 1) * qb, :] = qlat[:, h * R_KV:(h + 1) * R_KV].astype(BF16)
    m_ref[...] = jnp.full(m_ref.shape, -1e30, F32)
    l_ref[...] = jnp.zeros(l_ref.shape, F32)
    acc_ref[...] = jnp.zeros(acc_ref.shape, F32)

    def att_body(c, carry):
        kv = ckv_ref[0, pl.ds(pl.multiple_of(c * kc, kc), kc), :]
        bias = jnp.where(key_ref[c] >= thr, 0.0, NEG_INF)
        s = _dot_nt(qlat_ref[...], kv)
        s = (s.reshape(H_A, qb, kc) + bias[None]).reshape(H_A * qb, kc)
        tiles = [s[:, j * 128:(j + 1) * 128] for j in range(n_lt)]
        m_cur = tiles[0]
        for t in tiles[1:]:
            m_cur = jnp.maximum(m_cur, t)
        m_old = m_ref[...]
        m_new = jnp.maximum(m_old, jnp.max(m_cur, axis=1, keepdims=True))
        ps = [jnp.exp2(t - m_new) for t in tiles]
        a = jnp.exp2(m_old - m_new)
        psum = ps[0]
        for t in ps[1:]:
            psum = psum + t
        l_ref[...] = a * l_ref[...] + psum
        p = jnp.concatenate([t.astype(BF16) for t in ps], axis=1)
        acc_ref[...] = a * acc_ref[...] + _dot(p, kv)
        m_ref[...] = m_new
        return carry

    lax.fori_loop(0, nck, att_body, 0)
    o_all = acc_ref[...] / jnp.sum(l_ref[...], axis=1, keepdims=True)
    o_lat = jnp.concatenate([o_all[h * qb:(h + 1) * qb, :] for h in range(H_A)], axis=1)
    o_ref[0] = _dot(o_lat.astype(BF16), wuvbd_ref[...]).astype(o_ref.dtype)


def _block_diag(blocks):
    n = len(blocks)
    r, c = blocks[0].shape
    out = jnp.zeros((n * r, n * c), blocks[0].dtype)
    for k, blk in enumerate(blocks):
        out = out.at[k * r:(k + 1) * r, k * c:(k + 1) * c].set(blk)
    return out


def _dsa(cq, ckv, kidx, widx, w_uq, w_uk, w_uv, w_qidx, qb=256, kc=512):
    B, S, _ = cq.shape
    kc = min(kc, S)
    topk = min(TOPK_MAX, S // 4)
    assert S % kc == 0 and kc % qb == 0 and kc >= topk
    seq_bits = max(1, (S - 1).bit_length())
    wqidx = jnp.transpose(w_qidx, (1, 0, 2)).astype(BF16)
    wuq = w_uq.reshape(R_Q, H_A * D_NOPE).astype(BF16)
    wukbd = _block_diag([w_uk[:, h, :].T for h in range(H_A)]).astype(BF16)
    wuvbd = _block_diag([w_uv[:, h, :] for h in range(H_A)]).astype(BF16)
    kern = functools.partial(_dsa_kernel, qb=qb, kc=kc, topk=topk, seq_bits=seq_bits)
    blk_q = lambda b, i: (b, i, 0)
    seq = lambda b, i: (b, 0, 0)
    fix2 = lambda b, i: (0, 0)
    fix3 = lambda b, i: (0, 0, 0)
    return pl.pallas_call(
        kern,
        grid=(B, S // qb),
        in_specs=[pl.BlockSpec((1, qb, R_Q), blk_q), pl.BlockSpec((1, S, R_KV), seq),
                  pl.BlockSpec((1, S, D_IDX), seq), pl.BlockSpec((1, qb, H_IDX), blk_q),
                  pl.BlockSpec(wqidx.shape, fix3), pl.BlockSpec(wuq.shape, fix2),
                  pl.BlockSpec(wukbd.shape, fix2), pl.BlockSpec(wuvbd.shape, fix2)],
        out_specs=pl.BlockSpec((1, qb, H_A * D_VA), blk_q),
        out_shape=jax.ShapeDtypeStruct((B, S, H_A * D_VA), BF16),
        scratch_shapes=[pltpu.VMEM((S // kc, qb, kc), I32),
                        pltpu.VMEM((H_A * qb, R_KV), BF16),
                        pltpu.VMEM((H_A * qb, R_KV), F32), pltpu.VMEM((H_A * qb, R_KV), F32),
                        pltpu.VMEM((H_A * qb, R_KV), F32)],
        compiler_params=_cparams(("parallel", "arbitrary")),
        name="dsa_attention",
    )(cq, ckv, kidx, widx, wqidx, wuq, wukbd, wuvbd)


GDN_HALO = 8


def _softplus(x):
    return jnp.maximum(x, 0.0) + jnp.log1p(jnp.exp(-jnp.abs(x)))


def _gdn_kernel(qkvz_ref, abc_ref, abr_ref, convw_ref, prm_c_ref, prm_r_ref, ng_ref, tri_ref,
                o_ref, xbuf_ref, state_ref, conv_ref, *, cb):
    C = DN_CHUNK
    W = H_B * D_B
    j = pl.program_id(1)

    @pl.when(j == 0)
    def _():
        xbuf_ref[0:GDN_HALO, :] = jnp.zeros((GDN_HALO, 3 * W), F32)
        state_ref[...] = jnp.zeros(state_ref.shape, F32)

    xbuf_ref[GDN_HALO:, :] = qkvz_ref[0, :, 0:3 * W]
    acc = xbuf_ref[GDN_HALO:, :] * convw_ref[CONV_K - 1:CONV_K, :]
    for t in range(CONV_K - 1):
        sh = CONV_K - 1 - t
        acc = acc + xbuf_ref[pl.ds(GDN_HALO - sh, cb), :] * convw_ref[t:t + 1, :]
    conv_ref[...] = _silu(acc)
    xbuf_ref[0:GDN_HALO, :] = xbuf_ref[cb:cb + GDN_HALO, :]

    abc = abc_ref[0]
    abr = abr_ref[0]
    beta_c = _sigmoid(abc)
    g_c = -jnp.exp(prm_r_ref[0:1, :]) * _softplus(abc + prm_r_ref[1:2, :])
    g_r = -jnp.exp(prm_c_ref[:, 0:1]) * _softplus(abr + prm_c_ref[:, 1:2])
    gc_c = _dot(tri_ref[...], g_c, precision=HIGHEST)
    gc_r = _dot_nt(g_r, tri_ref[...], precision=HIGHEST)

    ri = lax.broadcasted_iota(I32, (cb, cb), 0)
    ci = lax.broadcasted_iota(I32, (cb, cb), 1)
    same = (ri // C) == (ci // C)
    lower = same & (ri >= ci)
    strict = same & (ri > ci)
    eye = jnp.where(ri == ci, 1.0, 0.0)
    bf = lambda t: t.astype(BF16)

    heads = range(H_B)
    qs, ks, gcols, kbs, vbs, kqs, decays, xms, tms = [], [], [], [], [], [], [], [], []
    for h in heads:
        q = conv_ref[:, h * D_B:(h + 1) * D_B]
        k = conv_ref[:, W + h * D_B:W + (h + 1) * D_B]
        v = conv_ref[:, 2 * W + h * D_B:2 * W + (h + 1) * D_B]
        q = q * lax.rsqrt(jnp.sum(q * q, axis=-1, keepdims=True) + 1e-6) * (D_B ** -0.5)
        k = k * lax.rsqrt(jnp.sum(k * k, axis=-1, keepdims=True) + 1e-6)
        beta = beta_c[:, h:h + 1]
        gcol = gc_c[:, H_B + h:H_B + h + 1]
        grow = gc_r[H_B + h:H_B + h + 1, :]
        decay = jnp.exp(jnp.where(lower, gcol - grow, NEG_INF))
        kb = k * beta
        kq = bf(k)
        a_mat = jnp.where(strict, _dot_nt(bf(kb), kq) * decay, 0.0)
        qs.append(q), ks.append(k), gcols.append(gcol), kbs.append(kb), vbs.append(v * beta)
        kqs.append(kq), decays.append(decay), xms.append(-a_mat), tms.append(eye - a_mat)
    for _ in range(int(math.log2(C)) - 1):
        xqs = [bf(xm) for xm in xms]
        xms = [_dot(xq, xq) for xq in xqs]
        tms = [t_mat + _dot(bf(t_mat), bf(xm)) for t_mat, xm in zip(tms, xms)]
    egcs = [jnp.exp(gcol) for gcol in gcols]
    tqs = [bf(t_mat) for t_mat in tms]
    us = [_dot(tq, bf(vb)) for tq, vb in zip(tqs, vbs)]
    ws = [bf(_dot(tq, bf(kb * egc))) for tq, kb, egc in zip(tqs, kbs, egcs)]
    qks = [bf(jnp.where(lower, _dot_nt(bf(q), kq) * decay, 0.0)) for q, kq, decay in zip(qs, kqs, decays)]
    q_decs = [bf(q * egc) for q, egc in zip(qs, egcs)]
    for n in range(cb // C):
        r0 = n * C
        for h in heads:
            gcol, k = gcols[h], ks[h]
            glast = gcol[r0 + C - 1:r0 + C, :]
            k_dec = bf(k[r0:r0 + C] * jnp.exp(glast - gcol[r0:r0 + C]))
            st = state_ref[h]
            stq = bf(st)
            v_new = us[h][r0:r0 + C] - _dot(ws[h][r0:r0 + C], stq)
            vq = bf(v_new)
            o = _dot(q_decs[h][r0:r0 + C], stq) + _dot(qks[h][r0:r0 + C, r0:r0 + C], vq)
            state_ref[h] = st * jnp.exp(glast) + _dot_tn(k_dec, vq)
            o = o * lax.rsqrt(jnp.mean(o * o, axis=-1, keepdims=True) + 1e-6) * ng_ref[...]
            z = qkvz_ref[0, r0:r0 + C, 3 * W + h * D_B:3 * W + (h + 1) * D_B]
            o_ref[0, r0:r0 + C, h * D_B:(h + 1) * D_B] = (o * _silu(z)).astype(o_ref.dtype)


def _gdn(qkvz, b, a, conv_w, a_log, dt_bias, norm_g, cb=256):
    B, S, _ = qkvz.shape
    W = H_B * D_B
    cb = min(cb, S)
    abc = jnp.concatenate([b, a], axis=-1)
    abr = jnp.transpose(abc, (0, 2, 1))
    zeros = jnp.zeros((H_B,), F32)
    prm = jnp.stack([jnp.concatenate([zeros, a_log]), jnp.concatenate([zeros, dt_bias])])
    idx = jnp.arange(cb)
    tri = ((idx[:, None] >= idx[None, :]) & (idx[:, None] // DN_CHUNK == idx[None, :] // DN_CHUNK)).astype(F32)
    kern = functools.partial(_gdn_kernel, cb=cb)
    blk = lambda bi, j: (bi, j, 0)
    fix = lambda bi, j: (0, 0)
    return pl.pallas_call(
        kern,
        grid=(B, S // cb),
        in_specs=[pl.BlockSpec((1, cb, 4 * W), blk), pl.BlockSpec((1, cb, 2 * H_B), blk),
                  pl.BlockSpec((1, 2 * H_B, cb), lambda bi, j: (bi, 0, j)),
                  pl.BlockSpec((CONV_K, 3 * W), fix), pl.BlockSpec((2 * H_B, 2), fix),
                  pl.BlockSpec((2, 2 * H_B), fix), pl.BlockSpec((1, D_B), fix),
                  pl.BlockSpec((cb, cb), fix)],
        out_specs=pl.BlockSpec((1, cb, W), blk),
        out_shape=jax.ShapeDtypeStruct((B, S, W), BF16),
        scratch_shapes=[pltpu.VMEM((cb + GDN_HALO, 3 * W), F32), pltpu.VMEM((H_B, D_B, D_B), F32),
                        pltpu.VMEM((cb, 3 * W), F32)],
        compiler_params=_cparams(("parallel", "arbitrary")),
        name="gated_deltanet",
    )(qkvz, abc, abr, conv_w, prm.T, prm, norm_g.reshape(1, D_B), tri)


def _store_row_tiles(ref, val):
    n, d = val.shape
    sub = d // LANES
    for j in range(sub):
        ref[pl.ds(j, n, stride=sub), :] = val[:, j * LANES:(j + 1) * LANES]


def _load_row_tiles(ref, n, sub):
    return jnp.concatenate([ref[pl.ds(j, n, stride=sub), :] for j in range(sub)], axis=1)


def _post_mixer_kernel(a1_ref, a2_ref, h_ref, w1_ref, w2_ref, g_ref, b_ref, rw_ref, rb_ref, triu_ref,
                       h1_ref, h1t_ref, tope_ref, gate_ref, rank_ref, cnt_ref, carry_ref):
    i = pl.program_id(0)
    E, tm = rw_ref.shape[0], h_ref.shape[0]

    @pl.when(i == 0)
    def _():
        carry_ref[...] = jnp.zeros(carry_ref.shape, F32)

    mix = _dot(a1_ref[...].astype(BF16), w1_ref[...]) + _dot(a2_ref[...].astype(BF16), w2_ref[...])
    h1 = _layernorm_rows(ALPHA * h_ref[...] + mix, g_ref[...], b_ref[...])
    h1_ref[...] = h1
    _store_row_tiles(h1t_ref, h1)

    logits = _dot_nt(rw_ref[...], h1, precision=HIGHEST) + rb_ref[...]
    erow = lax.broadcasted_iota(I32, (E, tm), 0)
    sel = jnp.zeros((E, tm), F32)
    onehots, tops = [], []
    for k in range(TOP_K):
        mx = jnp.max(logits, axis=0, keepdims=True)
        idx = jnp.min(jnp.where(logits == mx, erow, E), axis=0, keepdims=True)
        oh = erow == idx
        logits = jnp.where(oh, NEG_INF, logits)
        sel = sel + jnp.where(oh, 1.0, 0.0)
        onehots.append(oh)
        tops.append(mx)
        tope_ref[k:k + 1, :] = idx
    exps = [jnp.exp(t - tops[0]) for t in tops]
    den = exps[0] + exps[1] + exps[2] + exps[3]
    for k in range(TOP_K):
        gate_ref[k:k + 1, :] = exps[k] / den
    incl = _dot(sel.astype(BF16), triu_ref[...])
    excl = incl - sel + carry_ref[:, 0:1]
    for k in range(TOP_K):
        rank_ref[k:k + 1, :] = jnp.sum(jnp.where(onehots[k], excl, 0.0), axis=0, keepdims=True).astype(I32)
    carry_ref[...] = carry_ref[...] + jnp.sum(sel, axis=1, keepdims=True)
    cnt_ref[...] = carry_ref[...]


def _post_mixer(a1, a2, h2d, w_out, ln_g, ln_b, router_w, router_b, tm=512):
    T, D = h2d.shape
    E = router_w.shape[1]
    n1, n2 = a1.shape[1], a2.shape[1]
    w1 = w_out[:n1].astype(BF16)
    w2 = w_out[n1:].astype(BF16)
    idx = jnp.arange(tm)
    triu = (idx[:, None] <= idx[None, :]).astype(BF16)
    row = lambda i: (i, 0)
    col = lambda i: (0, i)
    fix = lambda i: (0, 0)
    return pl.pallas_call(
        _post_mixer_kernel,
        grid=(T // tm,),
        in_specs=[pl.BlockSpec((tm, n1), row), pl.BlockSpec((tm, n2), row), pl.BlockSpec((tm, D), row),
                  pl.BlockSpec((n1, D), fix), pl.BlockSpec((n2, D), fix),
                  pl.BlockSpec((1, D), fix), pl.BlockSpec((1, D), fix),
                  pl.BlockSpec((E, D), fix), pl.BlockSpec((E, 1), fix), pl.BlockSpec((tm, tm), fix)],
        out_specs=[pl.BlockSpec((tm, D), row), pl.BlockSpec((tm * (D // LANES), LANES), row),
                   pl.BlockSpec((TOP_K, tm), col), pl.BlockSpec((TOP_K, tm), col),
                   pl.BlockSpec((TOP_K, tm), col), pl.BlockSpec((E, 128), fix)],
        out_shape=[jax.ShapeDtypeStruct((T, D), F32), jax.ShapeDtypeStruct((T * (D // LANES), LANES), F32),
                   jax.ShapeDtypeStruct((TOP_K, T), I32), jax.ShapeDtypeStruct((TOP_K, T), F32),
                   jax.ShapeDtypeStruct((TOP_K, T), I32), jax.ShapeDtypeStruct((E, 128), F32)],
        scratch_shapes=[pltpu.VMEM((E, 128), F32)],
        compiler_params=_cparams(("arbitrary",)),
        name="post_mixer_router",
    )(a1, a2, h2d, w1, w2, ln_g.reshape(1, D), ln_b.reshape(1, D), router_w.T, router_b.reshape(E, 1), triu)


def _dispatch_kernel(dest_ref, fill0_ref, filln_ref, h_ref, xs_hbm, zero_ref, sem, zsem, *, tt, n_tok, sub, n_exp):
    base = pl.program_id(0) * tt

    @pl.when(pl.program_id(0) == 0)
    def _():
        zero_ref[...] = jnp.zeros(zero_ref.shape, zero_ref.dtype)

        def zero_copy(e, r):
            row = fill0_ref[e] + r
            return pltpu.make_async_copy(zero_ref, xs_hbm.at[pl.ds(pl.multiple_of(row * sub, sub), sub)], zsem)

        def each_row(fn):
            def per_expert(e, c):
                def per_row(r, c2):
                    fn(zero_copy(e, r))
                    return c2
                lax.fori_loop(0, filln_ref[e], per_row, 0)
                return c
            lax.fori_loop(0, n_exp, per_expert, 0)

        each_row(lambda cp: cp.start())
        each_row(lambda cp: cp.wait())

    def row_copy(r, k):
        d = dest_ref[k * n_tok + base + r]
        return pltpu.make_async_copy(h_ref.at[pl.ds(pl.multiple_of(r * sub, sub), sub)],
                                     xs_hbm.at[pl.ds(pl.multiple_of(d * sub, sub), sub)], sem)

    def start(r, c):
        for k in range(TOP_K):
            row_copy(r, k).start(priority=k % 2)
        return c

    def wait(r, c):
        for k in range(TOP_K):
            row_copy(r, k).wait()
        return c

    lax.fori_loop(0, tt, start, 0)
    lax.fori_loop(0, tt, wait, 0)


def _moe_dispatch(h1t, dest_flat, fill_start, fill_len, n_slots, sub, tt=512):
    T = h1t.shape[0] // sub
    any_spec = pl.BlockSpec(memory_space=pl.ANY)
    return pl.pallas_call(
        functools.partial(_dispatch_kernel, tt=tt, n_tok=T, sub=sub, n_exp=fill_start.shape[0]),
        grid_spec=pltpu.PrefetchScalarGridSpec(
            num_scalar_prefetch=3, grid=(T // tt,),
            in_specs=[pl.BlockSpec((tt * sub, LANES), lambda i, *_: (i, 0))], out_specs=any_spec,
            scratch_shapes=[pltpu.VMEM((sub, LANES), h1t.dtype), pltpu.SemaphoreType.DMA(()),
                            pltpu.SemaphoreType.DMA(())]),
        out_shape=jax.ShapeDtypeStruct((n_slots * sub, LANES), h1t.dtype),
        compiler_params=_cparams(("arbitrary",)),
        name="moe_dispatch",
    )(dest_flat, fill_start, fill_len, h1t)


MOE_BLOCK = 256


def _moe_kernel(be_ref, nu_ref, first_ref, slot_ref, next_ref, x_ref, wgu_hbm, bg_ref, bu_ref, wd_hbm, bd_ref,
                y_ref, wgu_buf, wd_buf, wt_ref, wg_ref, wu_ref, wdb_ref, sem, *, bm, sub, layer, n_exp):
    i = pl.program_id(0)
    F = wd_buf.shape[1]

    def weight_copies(e, s):
        return (pltpu.make_async_copy(wgu_hbm.at[layer, e], wgu_buf.at[s], sem.at[0, s]),
                pltpu.make_async_copy(wd_hbm.at[layer, e], wd_buf.at[s], sem.at[1, s]))

    @pl.when((i < nu_ref[0]) & (first_ref[i] == 1))
    def _():
        e, s, nxt = be_ref[i], slot_ref[i], next_ref[i]

        @pl.when(i == 0)
        def _():
            for cp in weight_copies(e, s):
                cp.start()

        for cp in weight_copies(e, s):
            cp.wait()

        @pl.when(nxt < n_exp)
        def _():
            for cp in weight_copies(nxt, 1 - s):
                cp.start()

        ck = 512
        for c in range(0, 2 * F, ck):
            wt = wgu_buf[s, :, c:c + ck].T
            for j in range(sub):
                wt_ref[pl.ds(c * sub + j, ck, stride=sub), :] = wt[:, j * LANES:(j + 1) * LANES]
        for j in range(sub):
            cols = slice(j * LANES, (j + 1) * LANES)
            wg_ref[:, cols] = wt_ref[pl.ds(j, F, stride=2 * sub), :].astype(BF16)
            wu_ref[:, cols] = wt_ref[pl.ds(sub + j, F, stride=2 * sub), :].astype(BF16)
        wdb_ref[...] = wd_buf[s].astype(BF16)

    @pl.when(i < nu_ref[0])
    def _():
        x = _load_row_tiles(x_ref, bm, sub).astype(BF16)
        g = _dot_nt(x, wg_ref[...]) + bg_ref[0]
        u = _dot_nt(x, wu_ref[...]) + bu_ref[0]
        gt = jnp.minimum(g, SWIGLU_LIMIT)
        up = jnp.clip(u, -SWIGLU_LIMIT, SWIGLU_LIMIT)
        hid = (up + 1.0) * (gt * _sigmoid(gt * SWIGLU_ALPHA))
        _store_row_tiles(y_ref, _dot(hid.astype(BF16), wdb_ref[...]) + bd_ref[0])

    @pl.when(i >= nu_ref[0])
    def _():
        y_ref[...] = jnp.zeros(y_ref.shape, y_ref.dtype)


def _moe_experts(xs, blk_e, n_used, padded, w_gu, bg, bu, w_down, bd, sub, layer):
    _, E, D, F2 = w_gu.shape
    F = F2 // 2
    bm = MOE_BLOCK
    P = xs.shape[0] // sub
    nblk = P // bm
    prev_e = jnp.concatenate([jnp.full((1,), -1, I32), blk_e[:-1]])
    first = (blk_e != prev_e).astype(I32)
    slot = (jnp.cumsum(first) - 1) % 2
    ids = jnp.where(padded > 0, jnp.arange(E, dtype=I32), E)
    later_min = lax.cummin(ids[::-1])[::-1]
    next_used = jnp.concatenate([later_min[1:], jnp.full((1,), E, I32)])
    nxt = jnp.sum(jnp.where(blk_e[:, None] == jnp.arange(E, dtype=I32)[None, :], next_used[None, :], 0), axis=1)
    wsel = lambda i, *_: (0, 0, 0)
    esel = lambda i, be, *_: (be[i], 0, 0)
    row = lambda i, *_: (i, 0)
    any_spec = pl.BlockSpec(memory_space=pl.ANY)
    return pl.pallas_call(
        functools.partial(_moe_kernel, bm=bm, sub=sub, layer=layer, n_exp=E),
        grid_spec=pltpu.PrefetchScalarGridSpec(
            num_scalar_prefetch=5,
            grid=(nblk,),
            in_specs=[pl.BlockSpec((bm * sub, LANES), row), any_spec,
                      pl.BlockSpec((1, 1, F), esel), pl.BlockSpec((1, 1, F), esel),
                      any_spec, pl.BlockSpec((1, 1, D), esel)],
            out_specs=pl.BlockSpec((bm * sub, LANES), row),
            scratch_shapes=[pltpu.VMEM((2, D, F2), F32), pltpu.VMEM((2, F, D), F32),
                            pltpu.VMEM((F2 * sub, LANES), F32), pltpu.VMEM((F, D), BF16), pltpu.VMEM((F, D), BF16),
                            pltpu.VMEM((F, D), BF16), pltpu.SemaphoreType.DMA((2, 2))]),
        out_shape=jax.ShapeDtypeStruct((P * sub, LANES), F32),
        compiler_params=_cparams(("arbitrary",)),
        name="moe_experts",
    )(blk_e, n_used, first, slot.astype(I32), nxt.astype(I32), xs, w_gu, bg, bu, w_down, bd)


def _moe(h1t, top_e, rank, cnt, w_gu, b_gu, w_down, b_down, layer):
    E, D = w_gu.shape[1], w_gu.shape[2]
    sub = D // LANES
    T = h1t.shape[0] // sub
    bm = MOE_BLOCK
    counts = cnt[:, 0].astype(I32)
    padded = (counts + bm - 1) // bm * bm
    pad_end = jnp.cumsum(padded)
    pad_start = pad_end - padded
    e_ids = jnp.arange(E, dtype=I32)[:, None, None]
    start_of = jnp.sum(jnp.where(top_e[None] == e_ids, pad_start[:, None, None], 0), axis=0)
    dest = (start_of + rank).reshape(-1)
    nblk = (T * TOP_K) // bm + E
    blk_first = jnp.arange(nblk, dtype=I32) * bm
    blk_e = jnp.minimum(jnp.sum(pad_end[None, :] <= blk_first[:, None], axis=1), E - 1).astype(I32)
    n_used = (pad_end[-1:] // bm).astype(I32)
    xs = _moe_dispatch(h1t, dest, pad_start + counts, padded - counts, nblk * bm, sub)
    ys = _moe_experts(xs, blk_e, n_used, padded, w_gu, b_gu[:, None, 0::2], b_gu[:, None, 1::2],
                      w_down, b_down[:, None, :], sub, layer)
    return ys, dest


def _tail_kernel(dest_ref, h1_ref, gate_ref, p_ref, wg_ref, wp_ref, g_ref, b_ref, ys_hbm, o_ref,
                 ybuf, sem, *, tm, n_tok, sub):
    i = pl.program_id(0)

    def row_copy(tile, slot, r, k):
        d = dest_ref[k * n_tok + tile * tm + r]
        return pltpu.make_async_copy(ys_hbm.at[pl.ds(pl.multiple_of(d * sub, sub), sub)],
                                     ybuf.at[slot, k, pl.ds(pl.multiple_of(r * sub, sub), sub)], sem.at[slot])

    def start_tile(tile, slot):
        def body(r, c):
            for k in range(TOP_K):
                row_copy(tile, slot, r, k).start(priority=k % 2)
            return c
        lax.fori_loop(0, tm, body, 0)

    def wait_tile(tile, slot):
        def body(r, c):
            for k in range(TOP_K):
                row_copy(tile, slot, r, k).wait()
            return c
        lax.fori_loop(0, tm, body, 0)

    @pl.when(i == 0)
    def _():
        start_tile(0, 0)

    h1 = h1_ref[...]
    ple = _sigmoid(_dot(h1.astype(BF16), wg_ref[...])) * _dot(p_ref[...].astype(BF16), wp_ref[...])
    x = ALPHA * h1 + ple

    for slot in range(2):
        @pl.when(i % 2 == slot)
        def _(slot=slot):
            @pl.when(i + 1 < pl.num_programs(0))
            def _():
                start_tile(i + 1, 1 - slot)

            wait_tile(i, slot)
            cols = []
            for j in range(sub):
                col = ybuf[slot, 0, pl.ds(j, tm, stride=sub), :] * gate_ref[:, 0:1]
                for k in range(1, TOP_K):
                    col = col + ybuf[slot, k, pl.ds(j, tm, stride=sub), :] * gate_ref[:, k:k + 1]
                cols.append(col)
            ffn = jnp.concatenate(cols, axis=1)
            o_ref[...] = _layernorm_rows(x + ffn, g_ref[...], b_ref[...])


def _layer_tail(h1, ys, dest_flat, gate_t, p2d, ple_w_gate, ple_w_proj, ln_g, ln_b, tm=256):
    T, D = h1.shape
    PD = p2d.shape[1]
    sub = D // LANES
    row = lambda i, d: (i, 0)
    fix = lambda i, d: (0, 0)
    return pl.pallas_call(
        functools.partial(_tail_kernel, tm=tm, n_tok=T, sub=sub),
        grid_spec=pltpu.PrefetchScalarGridSpec(
            num_scalar_prefetch=1, grid=(T // tm,),
            in_specs=[pl.BlockSpec((tm, D), row), pl.BlockSpec((tm, TOP_K), row),
                      pl.BlockSpec((tm, PD), row), pl.BlockSpec((D, D), fix), pl.BlockSpec((PD, D), fix),
                      pl.BlockSpec((1, D), fix), pl.BlockSpec((1, D), fix),
                      pl.BlockSpec(memory_space=pl.ANY)],
            out_specs=pl.BlockSpec((tm, D), row),
            scratch_shapes=[pltpu.VMEM((2, TOP_K, tm * sub, LANES), F32), pltpu.SemaphoreType.DMA((2,))]),
        out_shape=jax.ShapeDtypeStruct((T, D), F32),
        compiler_params=_cparams(("arbitrary",)),
        name="layer_tail",
    )(dest_flat, h1, gate_t.T, p2d, ple_w_gate.astype(BF16), ple_w_proj.astype(BF16),
      ln_g.reshape(1, D), ln_b.reshape(1, D), ys)


def _cd_proj_kernel(x_ref, w_ref, q_ref, k_ref, v_ref, u_ref):
    x = x_ref[...].astype(BF16)
    n = H_C * DH_C
    q_ref[...] = (_dot(x, w_ref[:, 0:n]) * (DH_C ** -0.5 * LOG2E)).astype(q_ref.dtype)
    k_ref[...] = _dot(x, w_ref[:, n:2 * n]).astype(k_ref.dtype)
    v_ref[...] = _dot(x, w_ref[:, 2 * n:3 * n]).astype(v_ref.dtype)
    u_ref[...] = _dot(x, w_ref[:, 3 * n:])


def _cd_in_proj(h2d, w_in, tm=512):
    T, D = h2d.shape
    n = H_C * DH_C
    nu = w_in.shape[1] - 3 * n
    row = lambda i: (i, 0)
    return pl.pallas_call(
        _cd_proj_kernel,
        grid=(T // tm,),
        in_specs=[pl.BlockSpec((tm, D), row), pl.BlockSpec(w_in.shape, lambda i: (0, 0))],
        out_specs=[pl.BlockSpec((tm, n), row)] * 3 + [pl.BlockSpec((tm, nu), row)],
        out_shape=[jax.ShapeDtypeStruct((T, n), BF16)] * 3 + [jax.ShapeDtypeStruct((T, nu), F32)],
        compiler_params=_cparams(("parallel",)),
        name="cd_in_proj",
    )(h2d, w_in.astype(BF16))


def _dilated_bias_table(qb):
    import numpy as np
    max_w = max(w for w, _ in DILATED_PATTERNS)
    ndc = max_w // qb + 1
    r = np.arange(qb)[:, None]
    j = np.arange(qb)[None, :]
    tbl = np.empty((ndc, qb, qb), np.float32)
    for dc in range(ndc):
        delta = dc * qb + r - j
        mult = np.zeros((qb, qb), np.float64)
        for w, d in DILATED_PATTERNS:
            mult += (delta >= 0) & (delta <= w) & (delta % d == 0)
        with np.errstate(divide="ignore"):
            tbl[dc] = np.log2(mult)
    return jnp.asarray(tbl)


def _dilated_kernel(q_ref, k_ref, v_ref, bias_ref, o_ref, q2_ref, m_ref, l_ref, acc_ref, *, qb, ndc):
    i = pl.program_id(2)
    n_lt = qb // 128
    q = q_ref[0]
    lo = lax.broadcasted_iota(I32, q.shape, 1) < DH_C
    zero = jnp.zeros(q.shape, q.dtype)
    q2_ref[0:qb, :] = jnp.where(lo, q, zero)
    q2_ref[qb:, :] = jnp.where(lo, zero, q)
    m_ref[...] = jnp.full(m_ref.shape, -1e30, F32)
    l_ref[...] = jnp.zeros(l_ref.shape, F32)
    acc_ref[...] = jnp.zeros(acc_ref.shape, F32)

    def body(c, carry):
        r0 = pl.multiple_of(c * qb, qb)
        kk = k_ref[0, pl.ds(r0, qb), :]
        vv = v_ref[0, pl.ds(r0, qb), :]
        s = _dot_nt(q2_ref[...], kk)
        s = (s.reshape(2, qb, qb) + bias_ref[i - c][None]).reshape(2 * qb, qb)
        tiles = [s[:, j * 128:(j + 1) * 128] for j in range(n_lt)]
        m_cur = tiles[0]
        for t in tiles[1:]:
            m_cur = jnp.maximum(m_cur, t)
        m_old = m_ref[...]
        m_new = jnp.maximum(m_old, jnp.max(m_cur, axis=1, keepdims=True))
        ps = [jnp.exp2(t - m_new) for t in tiles]
        a = jnp.exp2(m_old - m_new)
        psum = ps[0]
        for t in ps[1:]:
            psum = psum + t
        l_ref[...] = a * l_ref[...] + psum
        p = jnp.concatenate([t.astype(BF16) for t in ps], axis=1)
        acc_ref[...] = a * acc_ref[...] + _dot(p, vv)
        m_ref[...] = m_new
        return carry

    lax.fori_loop(jnp.maximum(i - (ndc - 1), 0), i + 1, body, 0)
    o_all = acc_ref[...] / jnp.sum(l_ref[...], axis=1, keepdims=True)
    o_ref[0] = jnp.where(lo, o_all[0:qb, :], o_all[qb:, :]).astype(o_ref.dtype)


def _dilated_attention(q, k, v, qb=512):
    B, S, n = q.shape
    qb = min(qb, S)
    bias = _dilated_bias_table(qb)
    ndc = bias.shape[0]
    pw = 2 * DH_C
    kern = functools.partial(_dilated_kernel, qb=qb, ndc=ndc)
    blk = lambda b, pr, i: (b, i, pr)
    seq = lambda b, pr, i: (b, 0, pr)
    return pl.pallas_call(
        kern,
        grid=(B, n // pw, S // qb),
        in_specs=[pl.BlockSpec((1, qb, pw), blk), pl.BlockSpec((1, S, pw), seq),
                  pl.BlockSpec((1, S, pw), seq), pl.BlockSpec(bias.shape, lambda b, pr, i: (0, 0, 0))],
        out_specs=pl.BlockSpec((1, qb, pw), blk),
        out_shape=jax.ShapeDtypeStruct((B, S, n), BF16),
        scratch_shapes=[pltpu.VMEM((2 * qb, pw), BF16), pltpu.VMEM((2 * qb, pw), F32),
                        pltpu.VMEM((2 * qb, pw), F32), pltpu.VMEM((2 * qb, pw), F32)],
        compiler_params=_cparams(("parallel", "parallel", "arbitrary")),
        name="dilated_attention",
    )(q, k, v, bias)


POOL_HALO = 16


def _pool_kernel(u_ref, w_ref, sc_ref, o_ref, xbuf_ref, *, tm):
    j = pl.program_id(1)

    @pl.when(j == 0)
    def _():
        xbuf_ref[0:POOL_HALO, :] = jnp.zeros((POOL_HALO, xbuf_ref.shape[1]), F32)

    xbuf_ref[POOL_HALO:, :] = u_ref[0]
    x = xbuf_ref[POOL_HALO:, :]
    grp = lax.broadcasted_iota(I32, (1, x.shape[1]), 1) // POOL_GROUP
    run = x
    sel = jnp.zeros(x.shape, F32)
    win = jnp.zeros((1, x.shape[1]), F32)
    for d in range(1, max(POOL_WINDOWS)):
        run = run + xbuf_ref[pl.ds(POOL_HALO - d, tm), :]
        if d + 1 in POOL_WINDOWS:
            gi = POOL_WINDOWS.index(d + 1)
            sel = jnp.where(grp == gi, run, sel)
            win = jnp.where(grp == gi, float(d + 1), win)
    pos = j * tm + lax.broadcasted_iota(I32, (tm, 1), 0)
    mean = sel / jnp.minimum((pos + 1).astype(F32), win)
    o_ref[0] = _dot((mean - x).astype(BF16), w_ref[...]) * sc_ref[...]
    xbuf_ref[0:POOL_HALO, :] = xbuf_ref[tm:tm + POOL_HALO, :]


def _multiscale_pool(u, pool_w, pool_scale, tm=512):
    B, S, n = u.shape
    tm = min(tm, S)
    wbd = _block_diag([pool_w[g] for g in range(pool_w.shape[0])]).astype(BF16)
    blk = lambda b, j: (b, j, 0)
    fix = lambda b, j: (0, 0)
    return pl.pallas_call(
        functools.partial(_pool_kernel, tm=tm),
        grid=(B, S // tm),
        in_specs=[pl.BlockSpec((1, tm, n), blk), pl.BlockSpec((n, n), fix), pl.BlockSpec((1, n), fix)],
        out_specs=pl.BlockSpec((1, tm, n), blk),
        out_shape=jax.ShapeDtypeStruct((B, S, n), F32),
        scratch_shapes=[pltpu.VMEM((tm + POOL_HALO, n), F32)],
        compiler_params=_cparams(("parallel", "arbitrary")),
        name="multiscale_pool",
    )(u, wbd, pool_scale.reshape(1, n))


def kernel(x, p, ab_w_in, ab_q_norm_g, ab_kv_norm_g, ab_w_uq, ab_w_uk, ab_w_uv, ab_w_qidx,
           ab_kidx_norm_g, ab_kidx_norm_b, ab_conv_w, ab_a_log, ab_dt_bias, ab_out_norm_g, ab_w_out,
           cd_w_in, cd_pool_w, cd_pool_scale, cd_w_out, ln_mix_g, ln_mix_b, router_w, router_b,
           w_gu, b_gu, w_down, b_down, ple_w_proj, ple_w_gate, ln_ffn_g, ln_ffn_b):
    B, S, D = x.shape
    T = B * S
    h = x.reshape(T, D)
    for i in range(DEPTH):
        j = i // 2
        if i % 2 == 0:
            cq, ckv, kidx, small, qkvz = _ab_in_proj(h, ab_w_in[j], ab_q_norm_g[j], ab_kv_norm_g[j],
                                                     ab_kidx_norm_g[j], ab_kidx_norm_b[j])
            sh = lambda t: t.reshape(B, S, -1)
            o_w = D_IDX + H_IDX
            o_a = _dsa(sh(cq), sh(ckv), sh(kidx), sh(small[:, D_IDX:o_w]),
                       ab_w_uq[j], ab_w_uk[j], ab_w_uv[j], ab_w_qidx[j])
            o_b = _gdn(sh(qkvz), sh(small[:, o_w:o_w + H_B]), sh(small[:, o_w + H_B:o_w + 2 * H_B]),
                       ab_conv_w[j], ab_a_log[j], ab_dt_bias[j], ab_out_norm_g[j])
            a1, a2, w_out = o_a.reshape(T, -1), o_b.reshape(T, -1), ab_w_out[j]
        else:
            q, k, v, u = _cd_in_proj(h, cd_w_in[j])
            sh = lambda t: t.reshape(B, S, -1)
            o_c = _dilated_attention(sh(q), sh(k), sh(v))
            o_d = _multiscale_pool(sh(u), cd_pool_w[j], cd_pool_scale[j])
            a1, a2, w_out = o_c.reshape(T, -1), o_d.reshape(T, -1), cd_w_out[j]
        h1, h1t, top_e, gate_t, rank, cnt = _post_mixer(a1, a2, h, w_out, ln_mix_g[i], ln_mix_b[i],
                                                        router_w[i], router_b[i])
        ys, dest = _moe(h1t, top_e, rank, cnt, w_gu, b_gu[i], w_down, b_down[i], i)
        h = _layer_tail(h1, ys, dest, gate_t, p[i].reshape(T, -1), ple_w_gate[i], ple_w_proj[i],
                        ln_ffn_g[i], ln_ffn_b[i])
    return h.reshape(B, S, D)
```
